```python
import jax, jax.numpy as jnp
from jax import lax
import numpy as np

D_MODEL = 1024
BATCH = 4
SEQ = 8192
DEPTH = 1
DEC_BATCH = 16
DEC_SEQ = 16
PAST_LEN = 2048

CHUNK = 64
D_PLE = 256
EPS = 1e-6
A_HEADS = 4
A_HDIM = 128
A_WIDTH = A_HEADS * A_HDIM
CONV_W = 4
B_HEADS = 8
B_HDIM = 64
B_WIDTH = B_HEADS * B_HDIM
W_LORA = 64
A_LORA = 64
G_LORA = 128
GN_EPS = 64e-5
MIX_WIDTH = A_WIDTH + B_WIDTH
A_COLS = 4 * A_WIDTH + 2 * A_HEADS
B_COLS = 3 * B_WIDTH + W_LORA + A_LORA + G_LORA
N_IN = A_COLS + B_COLS
N_EXPERTS = 32
TOP_K = 4
D_FF = 1024
SWIGLU_LIMIT = 7.0
SWIGLU_ALPHA = 1.702
MOE_BLOCK = 128

kernel_name = 'mlstm_rwkv7_moe_streaming_step'


def rmsnorm(x, g):
    x32 = x.astype(jnp.float32)
    y = x32 * lax.rsqrt(jnp.mean(x32 * x32, axis=-1, keepdims=True) + EPS)
    return (y * g.astype(jnp.float32)).astype(x.dtype)


def split_heads(t, n_heads):
    return t.reshape(t.shape[0], t.shape[1], n_heads, -1)


def causal_conv(x, state, w, b):
    L = x.shape[1]
    xp = jnp.concatenate([state, x], axis=1)
    y = b + sum(xp[:, j:j + L] * w[j] for j in range(CONV_W))
    return y, xp[:, xp.shape[1] - (CONV_W - 1):]


def mlstm_chunkwise(q, k, v, li, lf, C0, n0, m0):
    nb, nh, L, dh = q.shape
    csz = CHUNK if L % CHUNK == 0 else L
    nc = L // csz
    tri = jnp.tril(jnp.ones((csz, csz), dtype=bool))

    def to_chunks(t):
        t = t.reshape((nb, nh, nc, csz) + t.shape[3:])
        return jnp.moveaxis(t, 2, 0)

    def step(carry, xs):
        C, n, m = carry
        qc, kc, vc, lic, lfc = xs
        b = jnp.cumsum(lfc, axis=-1)
        dmat = jnp.where(tri, b[..., :, None] - b[..., None, :] + lic[..., None, :], -jnp.inf)
        inter = b + m[..., None]
        mt = jnp.maximum(inter, jnp.max(dmat, axis=-1))
        wts = jnp.exp(dmat - mt[..., None])
        wi = jnp.exp(inter - mt)
        s = jnp.einsum('bhtd,bhsd->bhts', qc, kc) * wts
        num = jnp.einsum('bhts,bhsd->bhtd', s, vc) + wi[..., None] * jnp.einsum('bhvk,bhtk->bhtv', C, qc)
        den = jnp.sum(s, axis=-1) + wi * jnp.einsum('bhk,bhtk->bht', n, qc)
        h = num / jnp.maximum(jnp.abs(den), jnp.exp(-mt))[..., None]
        m_new = mt[..., -1]
        w_end = jnp.exp(b[..., -1:] - b + lic - m_new[..., None])
        dec = jnp.exp(b[..., -1] + m - m_new)
        C_new = dec[..., None, None] * C + jnp.einsum('bhs,bhsv,bhsk->bhvk', w_end, vc, kc)
        n_new = dec[..., None] * n + jnp.einsum('bhs,bhsk->bhk', w_end, kc)
        return (C_new, n_new, m_new), h

    (C1, n1, m1), hs = lax.scan(step, (C0, n0, m0), tuple(to_chunks(t) for t in (q, k, v, li, lf)))
    h = jnp.moveaxis(hs, 0, 2).reshape(nb, nh, L, dh)
    return h, (C1, n1, m1)


def rwkv7_scan(S0, r, w, k, v, kk, a):
    def step(S, xs):
        r_t, w_t, k_t, v_t, kk_t, a_t = xs
        sa = jnp.einsum('bhvk,bhk->bhv', S, kk_t)
        S = (S * w_t[:, :, None, :] - sa[..., None] * (kk_t * a_t)[:, :, None, :]
             + v_t[..., None] * k_t[:, :, None, :])
        return S, jnp.einsum('bhvk,bhk->bhv', S, r_t)

    S1, o = lax.scan(step, S0, tuple(jnp.moveaxis(t, 1, 0) for t in (r, w, k, v, kk, a)))
    return jnp.moveaxis(o, 0, 1), S1


def moe(u, w_router, b_router, w_gu, b_gu, w_down, b_down):
    nb, L, D = u.shape
    x2 = u.reshape(-1, D)
    T = x2.shape[0]
    logits = (x2 @ w_router + b_router).astype(jnp.float32)
    top_v, top_e = lax.top_k(logits, TOP_K)
    gates = jax.nn.softmax(top_v, axis=-1)
    A = T * TOP_K
    flat_e = top_e.reshape(-1)
    order = jnp.argsort(flat_e)
    se = flat_e[order]
    stok = (order // TOP_K).astype(jnp.int32)
    sg = gates.reshape(-1)[order]
    counts = jnp.bincount(flat_e, length=N_EXPERTS)
    pcounts = (counts + MOE_BLOCK - 1) // MOE_BLOCK * MOE_BLOCK
    pend = jnp.cumsum(pcounts)
    cstart = jnp.cumsum(counts) - counts
    dest = (pend - pcounts)[se] + jnp.arange(A) - cstart[se]
    n_blk = -(-A // MOE_BLOCK) + N_EXPERTS
    slot_tok = jnp.full((n_blk * MOE_BLOCK,), T, jnp.int32).at[dest].set(stok)
    slot_gate = jnp.zeros((n_blk * MOE_BLOCK,), jnp.float32).at[dest].set(sg)
    blk_e = jnp.minimum(jnp.searchsorted(pend, jnp.arange(n_blk) * MOE_BLOCK, side='right'), N_EXPERTS - 1)
    x_pad = jnp.concatenate([x2, jnp.zeros((1, D), x2.dtype)], axis=0)

    def expert_block(args):
        tok, gate, e = args
        xb = x_pad[tok]
        gu = xb @ w_gu[e] + b_gu[e]
        g_, up = gu[:, :D_FF], gu[:, D_FF:]
        g_ = jnp.minimum(g_, SWIGLU_LIMIT)
        up = jnp.clip(up, -SWIGLU_LIMIT, SWIGLU_LIMIT)
        act = (up + 1.0) * g_ * jax.nn.sigmoid(SWIGLU_ALPHA * g_)
        return (act @ w_down[e] + b_down[e]) * gate[:, None].astype(act.dtype)

    out = lax.map(expert_block, (slot_tok.reshape(n_blk, MOE_BLOCK), slot_gate.reshape(n_blk, MOE_BLOCK), blk_e))
    y = jnp.zeros((T + 1, D), out.dtype).at[slot_tok].add(out.reshape(-1, D))[:T]
    return y.reshape(nb, L, D).astype(u.dtype)


def layer(h, p, state, lw):
    f32 = jnp.float32
    C0, n0, m0, conv0, S0, shift0 = (s.astype(f32) for s in state)
    nb, L, _ = h.shape
    u = rmsnorm(h, lw['norm_mix'])
    proj = (u @ lw['w_in']).astype(f32)
    pa, pb = proj[..., :A_COLS], proj[..., A_COLS:]

    qk, conv_new = causal_conv(pa[..., :2 * A_WIDTH], conv0, lw['conv_w'].astype(f32), lw['conv_b'].astype(f32))
    qk = jax.nn.silu(qk)

    def to_bhld(t):
        return jnp.moveaxis(split_heads(t, A_HEADS), 2, 1)

    q = to_bhld(qk[..., :A_WIDTH])
    k = to_bhld(qk[..., A_WIDTH:]) * (A_HDIM ** -0.5)
    v = to_bhld(pa[..., 2 * A_WIDTH:3 * A_WIDTH])
    o_gate = jax.nn.sigmoid(pa[..., 3 * A_WIDTH:4 * A_WIDTH])
    li = jnp.moveaxis(pa[..., 4 * A_WIDTH:4 * A_WIDTH + A_HEADS] + lw['b_ig'], 2, 1)
    lf = jnp.moveaxis(jax.nn.log_sigmoid(pa[..., 4 * A_WIDTH + A_HEADS:] + lw['b_fg']), 2, 1)
    ha, (C1, n1, m1) = mlstm_chunkwise(q, k, v, li, lf, C0, n0, m0)
    ha = ha * lax.rsqrt(jnp.mean(ha * ha, axis=-1, keepdims=True) + EPS)
    ha = jnp.moveaxis(ha, 1, 2).reshape(nb, L, A_WIDTH) * lw['a_norm'] * o_gate

    prev = jnp.concatenate([shift0, pb[:, :L - 1]], axis=1)
    shift_new = pb[:, L - 1:]
    xm = pb + (prev - pb) * lw['shift_mu']
    r = xm[..., :B_WIDTH]
    kb = xm[..., B_WIDTH:2 * B_WIDTH]
    vb = xm[..., 2 * B_WIDTH:3 * B_WIDTH]
    off = 3 * B_WIDTH
    wl = xm[..., off:off + W_LORA]
    al = xm[..., off + W_LORA:off + W_LORA + A_LORA]
    gl = xm[..., off + W_LORA + A_LORA:]
    w_log = -jax.nn.softplus(-(lw['w0'] + jnp.tanh(wl) @ lw['w_w2'])) - 0.5
    decay = jnp.exp(-jnp.exp(w_log))
    a = jax.nn.sigmoid(lw['a0'] + al @ lw['w_a2'])
    g = jax.nn.sigmoid(gl) @ lw['w_g2']
    kk = split_heads(kb * lw['k_k'], B_HEADS)
    kk = kk / jnp.maximum(jnp.linalg.norm(kk, axis=-1, keepdims=True), 1e-12)
    kb = kb * (1.0 + (a - 1.0) * lw['k_a'])
    rh, kh, vh = split_heads(r, B_HEADS), split_heads(kb, B_HEADS), split_heads(vb, B_HEADS)
    ob, S1 = rwkv7_scan(S0, rh, split_heads(decay, B_HEADS), kh, vh, kk, split_heads(a, B_HEADS))
    mu = jnp.mean(ob, axis=-1, keepdims=True)
    var = jnp.mean(jnp.square(ob - mu), axis=-1, keepdims=True)
    ob = (ob - mu) * lax.rsqrt(var + GN_EPS)
    bonus = jnp.sum(rh * kh * lw['r_k'], axis=-1, keepdims=True) * vh
    hb = (ob.reshape(nb, L, B_WIDTH) * lw['gn_w'] + lw['gn_b'] + bonus.reshape(nb, L, B_WIDTH)) * g

    h = h + jnp.concatenate([ha, hb], axis=-1).astype(h.dtype) @ lw['w_out']

    h = h + moe(rmsnorm(h, lw['norm_ffn']), lw['w_router'], lw['b_router'], lw['w_gu'], lw['b_gu'],
                lw['w_down'], lw['b_down'])

    gate = jax.nn.sigmoid(rmsnorm(h, lw['norm_ple']) @ lw['w_ple_gate'])
    h = h + gate * (p @ lw['w_ple_proj'])
    return h, (C1, n1, m1, conv_new, S1, shift_new)


def setup_inputs(seed: int = 0) -> dict:
    key = jax.random.key(seed)
    keys = iter(jax.random.split(key, 64))

    def nrm(shape, scale=1.0):
        return jax.random.normal(next(keys), shape, jnp.float32) * scale

    def unif(shape, lo, hi):
        return jax.random.uniform(next(keys), shape, jnp.float32, lo, hi)

    return {
        'x_prompt': nrm((BATCH, SEQ, D_MODEL)),
        'x_sample': nrm((DEC_BATCH, DEC_SEQ, D_MODEL)),
        'state_mlstm_C': nrm((DEPTH, DEC_BATCH, A_HEADS, A_HDIM, A_HDIM), 0.1),
        'state_mlstm_n': nrm((DEPTH, DEC_BATCH, A_HEADS, A_HDIM), 0.1),
        'state_mlstm_m': nrm((DEPTH, DEC_BATCH, A_HEADS)),
        'state_mlstm_conv': nrm((DEPTH, DEC_BATCH, CONV_W - 1, 2 * A_WIDTH)),
        'state_rwkv_S': nrm((DEPTH, DEC_BATCH, B_HEADS, B_HDIM, B_HDIM), 0.1),
        'state_rwkv_shift': nrm((DEPTH, DEC_BATCH, 1, B_COLS)),
        'p_prompt': nrm((DEPTH, BATCH, SEQ, D_PLE)),
        'p_sample': nrm((DEPTH, DEC_BATCH, DEC_SEQ, D_PLE)),
        'norm_mix': 1.0 + nrm((DEPTH, D_MODEL), 0.05),
        'w_in': nrm((DEPTH, D_MODEL, N_IN), D_MODEL ** -0.5),
        'conv_w': nrm((DEPTH, CONV_W, 2 * A_WIDTH), 0.5),
        'conv_b': nrm((DEPTH, 2 * A_WIDTH), 0.01),
        'b_ig': nrm((DEPTH, A_HEADS), 0.1) - 1.0,
        'b_fg': jnp.linspace(3.0, 6.0, A_HEADS, dtype=jnp.float32) + nrm((DEPTH, A_HEADS), 0.1),
        'a_norm': 1.0 + nrm((DEPTH, A_WIDTH), 0.05),
        'shift_mu': unif((DEPTH, B_COLS), 0.0, 1.0),
        'w0': unif((DEPTH, B_WIDTH), -5.0, -0.5),
        'w_w2': nrm((DEPTH, W_LORA, B_WIDTH), 0.1),
        'a0': nrm((DEPTH, B_WIDTH), 0.1),
        'w_a2': nrm((DEPTH, A_LORA, B_WIDTH), A_LORA ** -0.5),
        'w_g2': nrm((DEPTH, G_LORA, B_WIDTH), G_LORA ** -0.5),
        'k_k': 0.85 + nrm((DEPTH, B_WIDTH), 0.05),
        'k_a': 1.0 + nrm((DEPTH, B_WIDTH), 0.05),
        'r_k': nrm((DEPTH, B_HEADS, B_HDIM), 0.1),
        'gn_w': 1.0 + nrm((DEPTH, B_WIDTH), 0.05),
        'gn_b': nrm((DEPTH, B_WIDTH), 0.01),
        'w_out': nrm((DEPTH, MIX_WIDTH, D_MODEL), MIX_WIDTH ** -0.5),
        'norm_ffn': 1.0 + nrm((DEPTH, D_MODEL), 0.05),
        'w_router': nrm((DEPTH, D_MODEL, N_EXPERTS), D_MODEL ** -0.5),
        'b_router': nrm((DEPTH, N_EXPERTS), 0.01),
        'w_gu': nrm((DEPTH, N_EXPERTS, D_MODEL, 2 * D_FF), D_MODEL ** -0.5),
        'b_gu': nrm((DEPTH, N_EXPERTS, 2 * D_FF), 0.01),
        'w_down': nrm((DEPTH, N_EXPERTS, D_FF, D_MODEL), D_FF ** -0.5),
        'b_down': nrm((DEPTH, N_EXPERTS, D_MODEL), 0.01),
        'norm_ple': 1.0 + nrm((DEPTH, D_MODEL), 0.05),
        'w_ple_gate': nrm((DEPTH, D_MODEL, D_MODEL), D_MODEL ** -0.5),
        'w_ple_proj': nrm((DEPTH, D_PLE, D_MODEL), D_PLE ** -0.5),
        'final_norm': 1.0 + nrm((D_MODEL,), 0.05),
    }


def reference(x_prompt, x_sample, state_mlstm_C, state_mlstm_n, state_mlstm_m, state_mlstm_conv,
              state_rwkv_S, state_rwkv_shift, p_prompt, p_sample, norm_mix, w_in, conv_w, conv_b,
              b_ig, b_fg, a_norm, shift_mu, w0, w_w2, a0, w_a2, w_g2, k_k, k_a, r_k, gn_w, gn_b,
              w_out, norm_ffn, w_router, b_router, w_gu, b_gu, w_down, b_down, norm_ple,
              w_ple_gate, w_ple_proj, final_norm):
    f32 = jnp.float32
    nbp = x_prompt.shape[0]
    hp, hs = x_prompt, x_sample
    new_p, new_s = [], []
    for i in range(DEPTH):
        lw = dict(norm_mix=norm_mix[i], w_in=w_in[i], conv_w=conv_w[i], conv_b=conv_b[i],
                  b_ig=b_ig[i], b_fg=b_fg[i], a_norm=a_norm[i], shift_mu=shift_mu[i], w0=w0[i],
                  w_w2=w_w2[i], a0=a0[i], w_a2=w_a2[i], w_g2=w_g2[i], k_k=k_k[i], k_a=k_a[i],
                  r_k=r_k[i], gn_w=gn_w[i], gn_b=gn_b[i], w_out=w_out[i], norm_ffn=norm_ffn[i],
                  w_router=w_router[i], b_router=b_router[i], w_gu=w_gu[i], b_gu=b_gu[i],
                  w_down=w_down[i], b_down=b_down[i], norm_ple=norm_ple[i],
                  w_ple_gate=w_ple_gate[i], w_ple_proj=w_ple_proj[i])
        st_p = (jnp.zeros((nbp, A_HEADS, A_HDIM, A_HDIM), f32), jnp.zeros((nbp, A_HEADS, A_HDIM), f32),
                jnp.zeros((nbp, A_HEADS), f32), jnp.zeros((nbp, CONV_W - 1, 2 * A_WIDTH), f32),
                jnp.zeros((nbp, B_HEADS, B_HDIM, B_HDIM), f32), jnp.zeros((nbp, 1, B_COLS), f32))
        st_s = (state_mlstm_C[i], state_mlstm_n[i], state_mlstm_m[i], state_mlstm_conv[i],
                state_rwkv_S[i], state_rwkv_shift[i])
        hp, sp = layer(hp, p_prompt[i], st_p, lw)
        hs, ss = layer(hs, p_sample[i], st_s, lw)
        new_p.append(sp)
        new_s.append(ss)

    def stk(lst, j):
        return jnp.stack([s[j] for s in lst], axis=0)

    y_prompt = rmsnorm(hp, final_norm)
    y_sample = rmsnorm(hs, final_norm)
    return (y_prompt, y_sample,
            stk(new_p, 0), stk(new_p, 1), stk(new_p, 2), stk(new_p, 3), stk(new_p, 4), stk(new_p, 5),
            stk(new_s, 0), stk(new_s, 1), stk(new_s, 2), stk(new_s, 3), stk(new_s, 4), stk(new_s, 5))
```

```python
import functools

import jax
import jax.numpy as jnp
from jax import lax
from jax.experimental import pallas as pl
from jax.experimental.pallas import tpu as pltpu

F32 = jnp.float32
BF16 = jnp.bfloat16
I32 = jnp.int32

EPS = 1e-6
GN_EPS = 64e-5
A_HEADS, A_HDIM = 4, 128
B_HEADS, B_HDIM = 8, 64
A_WIDTH = A_HEADS * A_HDIM
B_WIDTH = B_HEADS * B_HDIM
CONV_W = 4
LORA_PAD = 128
N_EXPERTS, TOP_K = 32, 4
SWIGLU_LIMIT, SWIGLU_ALPHA = 7.0, 1.702
LANES = 128
ROW_TILE = 256
VMEM_LIMIT = 56 * 1024 * 1024


def _cparams(sem):
    return pltpu.CompilerParams(dimension_semantics=sem, vmem_limit_bytes=VMEM_LIMIT)


def _dot(a, b):
    return lax.dot_general(a, b, (((1,), (0,)), ((), ())), preferred_element_type=F32)


def _dot_nt(a, b):
    return lax.dot_general(a, b, (((1,), (1,)), ((), ())), preferred_element_type=F32)


def _dot_tn(a, b):
    return lax.dot_general(a, b, (((0,), (0,)), ((), ())), preferred_element_type=F32)


def _split2(x):
    hi = x.astype(BF16)
    lo = (x - hi.astype(F32)).astype(BF16)
    return hi, lo


def _split3(x):
    hi = x.astype(BF16)
    r1 = x - hi.astype(F32)
    mid = r1.astype(BF16)
    lo = (r1 - mid.astype(F32)).astype(BF16)
    return hi, mid, lo


def _dot3(a, b, f=_dot):
    ah, al = _split2(a)
    bh, bl = _split2(b)
    return f(ah, bh) + (f(ah, bl) + f(al, bh))


def _dotx(a, b_exact, f=_dot):
    a1, a2, a3 = _split3(a)
    return f(a1, b_exact) + (f(a2, b_exact) + f(a3, b_exact))


def _dotx_l(a_exact, b, f=_dot):
    b1, b2, b3 = _split3(b)
    return f(a_exact, b1) + (f(a_exact, b2) + f(a_exact, b3))


def _rms(x, g):
    return x * lax.rsqrt(jnp.mean(x * x, axis=-1, keepdims=True) + EPS) * g


def _sigmoid(x):
    return 1.0 / (1.0 + jnp.exp(-x))


def _two_specs(ntp, n):
    return [pl.BlockSpec((ROW_TILE, n), lambda i: (jnp.minimum(i, ntp - 1), 0)),
            pl.BlockSpec((ROW_TILE, n), lambda i: (jnp.maximum(i - ntp, 0), 0))]


def _pick(i, ntp, p_ref, s_ref):
    return jnp.where(i < ntp, p_ref[...], s_ref[...])


def _in_proj_kernel(xp_ref, xs_ref, g_ref, wqk_ref, wv_ref, wo_ref, wif_ref, wb_ref,
                    qk_o, v_o, og_o, if_o, pb_o, *, ntp):
    x = _pick(pl.program_id(0), ntp, xp_ref, xs_ref)
    u = _rms(x, g_ref[...])
    ub = u.astype(BF16)
    qk_o[...] = _dot(ub, wqk_ref[...])
    v_o[...] = _dot(ub, wv_ref[...])
    og_o[...] = _dot(ub, wo_ref[...])
    if_o[...] = _dot3(u, wif_ref[...])
    pb_o[...] = _dot(ub, wb_ref[...])


def _in_proj(xp, xs, g, wqk, wv, wo, wif, wb):
    D = xp.shape[1]
    T = xp.shape[0] + xs.shape[0]
    ntp = xp.shape[0] // ROW_TILE
    row = lambda n: pl.BlockSpec((ROW_TILE, n), lambda i: (i, 0))
    full = lambda a: pl.BlockSpec(a.shape, lambda i: (0, 0))
    outs = [wqk.shape[1], wv.shape[1], wo.shape[1], wif.shape[1], wb.shape[1]]
    return pl.pallas_call(
        functools.partial(_in_proj_kernel, ntp=ntp),
        grid=(T // ROW_TILE,),
        in_specs=_two_specs(ntp, D) + [full(g), full(wqk), full(wv), full(wo), full(wif), full(wb)],
        out_specs=[row(n) for n in outs],
        out_shape=[jax.ShapeDtypeStruct((T, n), F32) for n in outs],
        compiler_params=_cparams(("arbitrary",)),
        name="in_proj",
    )(xp, xs, g, wqk, wv, wo, wif, wb)


MLSTM_CHUNK = 128
NEG_BIG = -1e30


def _log_sigmoid(x):
    return jnp.minimum(x, 0.0) - jnp.log1p(jnp.exp(-jnp.abs(x)))


def _mlstm_kernel(qk_ref, v_ref, og_ref, if_ref, C0_ref, n0_ref, m0_ref, conv0_ref,
                  cw_ref, cb_ref, gb_ref, an_ref,
                  ha_o, C1_o, n1_o, m1_o, conv1_o, C_sc, n_sc, m_sc, cbuf, *, nv):
    CH = MLSTM_CHUNK
    c = pl.program_id(1)

    @pl.when(c == 0)
    def _init():
        C_sc[...] = C0_ref[0]
        n_sc[0:A_HEADS] = n0_ref[0]
        m_sc[0:1] = m0_ref[0]
        cbuf[5:8] = conv0_ref[0]

    cbuf[8:8 + nv] = qk_ref[...]
    if nv < CH:
        cbuf[8 + nv:8 + CH] = jnp.zeros((CH - nv, 2 * A_WIDTH), F32)
    y = cb_ref[...] + cbuf[5:5 + CH] * cw_ref[0:1]
    for j in range(1, CONV_W):
        y = y + cbuf[5 + j:5 + j + CH] * cw_ref[j:j + 1]
    conv_tail = cbuf[5 + nv:8 + nv]
    qk = y * _sigmoid(y)
    q = qk[:, :A_WIDTH]
    k = qk[:, A_WIDTH:] * (A_HDIM ** -0.5)
    if nv == CH:
        v = v_ref[...]
        og = og_ref[...]
        gp = if_ref[...] + gb_ref[...]
    else:
        zpad = lambda n: jnp.zeros((CH - nv, n), F32)
        v = jnp.concatenate([v_ref[...], zpad(A_WIDTH)], axis=0)
        og = jnp.concatenate([og_ref[...], zpad(A_WIDTH)], axis=0)
        gp = jnp.concatenate([if_ref[...] + gb_ref[...], zpad(LANES)], axis=0)

    row = lax.broadcasted_iota(I32, (CH, LANES), 0)
    li = gp
    lf = pltpu.roll(_log_sigmoid(gp), LANES - A_HEADS, 1)
    if nv < CH:
        li = jnp.where(row < nv, li, NEG_BIG)
        lf = jnp.where(row < nv, lf, 0.0)
    r2 = lax.broadcasted_iota(I32, (CH, CH), 0)
    c2 = lax.broadcasted_iota(I32, (CH, CH), 1)
    tri = r2 >= c2
    tril_b = jnp.where(tri, 1.0, 0.0).astype(BF16)
    b = _dotx_l(tril_b, lf)
    g = li - b
    sel = jnp.where(lax.broadcasted_iota(I32, (8, LANES), 0) == lax.broadcasted_iota(I32, (8, LANES), 1),
                    1.0, 0.0).astype(BF16)
    gT = _dotx_l(sel, g, f=_dot_nt)

    qb, kb, vb = q.astype(BF16), k.astype(BF16), v.astype(BF16)
    lane1 = lax.broadcasted_iota(I32, (1, LANES), 1)
    m_row = m_sc[0:1]
    m_next = m_row
    outs = []
    for h in range(A_HEADS):
        sl = slice(h * A_HDIM, (h + 1) * A_HDIM)
        bcol = b[:, h:h + 1]
        gcol = g[:, h:h + 1]
        grow = gT[h:h + 1, :]
        m_h = m_row[:, h:h + 1]
        dm = jnp.where(tri, bcol + grow, -jnp.inf)
        inter = bcol + m_h
        mt = jnp.maximum(inter, jnp.max(dm, axis=1, keepdims=True))
        wts = jnp.exp(dm - mt)
        wi = jnp.exp(inter - mt)
        C_h = C_sc[h]
        n_h = n_sc[h:h + 1]
        s = _dot_nt(qb[:, sl], kb[:, sl]) * wts
        num = _dot(s.astype(BF16), vb[:, sl]) + wi * _dot_nt(qb[:, sl], C_h.astype(BF16))
        den = jnp.sum(s, axis=1, keepdims=True) + wi * jnp.sum(q[:, sl] * n_h, axis=1, keepdims=True)
        hh = num / jnp.maximum(jnp.abs(den), jnp.exp(-mt))
        hn = hh * lax.rsqrt(jnp.mean(hh * hh, axis=1, keepdims=True) + EPS)
        outs.append(hn * an_ref[:, sl] * _sigmoid(og[:, sl]))
        m_new = mt[CH - 1:CH]
        b_last = bcol[CH - 1:CH]
        w_end = jnp.exp(b_last + gcol - m_new)
        dec = jnp.exp(b_last + m_h - m_new)
        C_sc[h] = dec * C_h + _dot_tn((v[:, sl] * w_end).astype(BF16), kb[:, sl])
        n_sc[h:h + 1] = dec * n_h + jnp.sum(k[:, sl] * w_end, axis=0, keepdims=True)
        m_next = jnp.where(lane1 == h, m_new, m_next)
    m_sc[0:1] = m_next
    ha = jnp.concatenate(outs, axis=1)
    ha_o[...] = ha[:nv].astype(BF16)
    cbuf[5:8] = conv_tail

    @pl.when(c == pl.num_programs(1) - 1)
    def _fin():
        C1_o[0] = C_sc[...]
        n1_o[0] = n_sc[0:A_HEADS]
        m1_o[0] = m_sc[0:1]
        conv1_o[0] = conv_tail


def _mlstm(qk, v, og, ifg, C0, n0, m0, conv0, cw, cb, gb, an, *, n_seq, L, row_off):
    CH = MLSTM_CHUNK
    nv = min(L, CH)
    nch = L // nv
    off = row_off // nv
    tok = lambda n: pl.BlockSpec((nv, n), lambda s, c: (off + s * nch + c, 0))
    st = lambda shp: pl.BlockSpec((1,) + shp, lambda s, c: (s,) + (0,) * len(shp))
    full = lambda a: pl.BlockSpec(a.shape, lambda s, c: (0,) * a.ndim)
    in_specs = [tok(2 * A_WIDTH), tok(A_WIDTH), tok(A_WIDTH), tok(LANES),
                st((A_HEADS, A_HDIM, A_HDIM)), st((A_HEADS, A_HDIM)), st((1, LANES)),
                st((CONV_W - 1, 2 * A_WIDTH)), full(cw), full(cb), full(gb), full(an)]
    args = [qk, v, og, ifg, C0, n0, m0, conv0, cw, cb, gb, an]
    out_shape = [jax.ShapeDtypeStruct((n_seq * L, A_WIDTH), BF16),
                 jax.ShapeDtypeStruct((n_seq, A_HEADS, A_HDIM, A_HDIM), F32),
                 jax.ShapeDtypeStruct((n_seq, A_HEADS, A_HDIM), F32),
                 jax.ShapeDtypeStruct((n_seq, 1, LANES), F32),
                 jax.ShapeDtypeStruct((n_seq, CONV_W - 1, 2 * A_WIDTH), F32)]
    out_specs = [pl.BlockSpec((nv, A_WIDTH), lambda s, c: (s * nch + c, 0)),
                 st((A_HEADS, A_HDIM, A_HDIM)), st((A_HEADS, A_HDIM)),
                 st((1, LANES)), st((CONV_W - 1, 2 * A_WIDTH))]
    return pl.pallas_call(
        functools.partial(_mlstm_kernel, nv=nv),
        grid=(n_seq, nch),
        in_specs=in_specs, out_specs=out_specs, out_shape=out_shape,
        scratch_shapes=[pltpu.VMEM((A_HEADS, A_HDIM, A_HDIM), F32), pltpu.VMEM((8, A_HDIM), F32),
                        pltpu.VMEM((8, LANES), F32), pltpu.VMEM((CH + 8, 2 * A_WIDTH), F32)],
        compiler_params=_cparams(("arbitrary", "arbitrary")),
        name="mlstm",
    )(*args)


RWKV_CHUNK = 64
RWKV_NCHUNK = 2
B_COLS_PAD = 3 * B_WIDTH + 2 * LORA_PAD + 128
PAIR = 2 * B_HDIM


def _sp(x):
    return _split2(x)


def _d3(a, b, f=_dot):
    return f(a[0], b[0]) + (f(a[0], b[1]) + f(a[1], b[0]))


def _softplus(z):
    return jnp.maximum(z, 0.0) + jnp.log1p(jnp.exp(-jnp.abs(z)))


def _rwkv_kernel(pb_ref, S0_ref, sh0_ref, mu_ref, w0_ref, ww2_ref, a0_ref, wa2_ref, wg2_ref,
                 kk_ref, ka_ref, rk_ref, gw_ref, gb_ref, bd_ref,
                 hb_o, S1_o, sh1_o, S_sc, sbuf, *, nv):
    C = RWKV_CHUNK
    nc = -(-nv // C)
    TB = nc * C
    t = pl.program_id(1)

    @pl.when(t == 0)
    def _init():
        S_sc[...] = S0_ref[0]
        sbuf[7:8] = sh0_ref[0]

    sbuf[8:8 + nv] = pb_ref[...]
    if nv < TB:
        sbuf[8 + nv:8 + TB] = jnp.zeros((TB - nv, B_COLS_PAD), F32)
    pb = sbuf[8:8 + TB]
    prev = sbuf[7:7 + TB]
    last_row = sbuf[7 + nv:8 + nv]
    xm = pb + (prev - pb) * mu_ref[...]
    W = B_WIDTH
    r = xm[:, 0:W]
    kb = xm[:, W:2 * W]
    vb = xm[:, 2 * W:3 * W]
    wl = xm[:, 3 * W:3 * W + LORA_PAD]
    al = xm[:, 3 * W + LORA_PAD:3 * W + 2 * LORA_PAD]
    gl = xm[:, 3 * W + 2 * LORA_PAD:]
    w_log = -_softplus(-(w0_ref[...] + _dot3(jnp.tanh(wl), ww2_ref[...]))) - 0.5
    lw = -jnp.exp(w_log)
    a = _sigmoid(a0_ref[...] + _dot3(al, wa2_ref[...]))
    g = _dot(_sigmoid(gl).astype(BF16), wg2_ref[...])
    bd = bd_ref[...]
    kkr = kb * kk_ref[...]
    kk = kkr / jnp.maximum(jnp.sqrt(_dotx(kkr * kkr, bd)), 1e-12)
    k = kb * (1.0 + (a - 1.0) * ka_ref[...])
    bonus = _dotx(r * k * rk_ref[...], bd) * vb
    if nv < TB:
        valid = lax.broadcasted_iota(I32, (TB, W), 0) < nv
        zero = lambda z: jnp.where(valid, z, 0.0)
        lw, kk, k, vb = zero(lw), zero(kk), zero(k), zero(vb)

    lane = lax.broadcasted_iota(I32, (C, PAIR), 1)
    h0 = lane < B_HDIM
    sep = lambda z: jnp.concatenate([jnp.where(h0, z, 0.0), jnp.where(h0, 0.0, z)], axis=0)
    ri = lax.broadcasted_iota(I32, (C, C), 0)
    ci = lax.broadcasted_iota(I32, (C, C), 1)
    tril_c = jnp.where(ri >= ci, 1.0, 0.0).astype(BF16)
    r2 = lax.broadcasted_iota(I32, (PAIR, PAIR), 0)
    c2 = lax.broadcasted_iota(I32, (PAIR, PAIR), 1)
    low_s = r2 > c2
    low_i = r2 >= c2
    eye = jnp.where(r2 == c2, 1.0, 0.0)
    o_rows = []
    for c in range(nc):
        rs = slice(c * C, (c + 1) * C)
        o_cols = []
        for p in range(B_HEADS // 2):
            ls = slice(p * PAIR, (p + 1) * PAIR)
            lwc = lw[rs, ls]
            b = _dotx_l(tril_c, lwc)
            emb = jnp.exp(-b)
            kkc = kk[rs, ls]
            kap = _sp(sep(kkc * jnp.exp(b - lwc)))
            kt = _sp(sep(k[rs, ls] * emb))
            bt = _sp(sep(kkc * a[rs, ls] * emb))
            rt = _sp(sep(r[rs, ls] * jnp.exp(b)))
            vs = _sp(sep(vb[rs, ls]))
            cat = lambda x, y: (jnp.concatenate([x[0], y[0]], axis=0), jnp.concatenate([x[1], y[1]], axis=0))
            A = _d3(cat(kap, rt), cat(kt, bt), _dot_nt)
            Ak = jnp.where(low_s, A[:PAIR, :PAIR], 0.0)
            Ab = jnp.where(low_s, A[:PAIR, PAIR:], 0.0)
            Ark = jnp.where(low_i, A[PAIR:, :PAIR], 0.0)
            Arb = jnp.where(low_i, A[PAIR:, PAIR:], 0.0)
            Abs = _sp(Ab)
            Tm = eye - Ab
            P = _d3(Abs, Abs)
            n = 2
            while n < C:
                Ps = _sp(P)
                Tm = Tm + _d3(_sp(Tm), Ps)
                n *= 2
                if n < C:
                    P = _d3(Ps, Ps)
            S = S_sc[p]
            Ss = _sp(S)
            rhs = _d3(kap, Ss, _dot_nt) + _d3(_sp(Ak), vs)
            U = -_d3(_sp(Tm), _sp(rhs))
            Us = _sp(U)
            O = _d3(rt, Ss, _dot_nt) + _d3(_sp(Ark), vs) + _d3(_sp(Arb), Us)
            o_cols.append(O[:C] + O[C:])
            S_sc[p] = (S + _d3(vs, kt, _dot_tn) + _d3(Us, bt, _dot_tn)) * jnp.exp(b[C - 1:C])
        o_rows.append(jnp.concatenate(o_cols, axis=1))
    o = o_rows[0] if nc == 1 else jnp.concatenate(o_rows, axis=0)

    mean = _dotx(o, bd) * (1.0 / B_HDIM)
    d = o - mean
    var = _dotx(d * d, bd) * (1.0 / B_HDIM)
    hb = (d * lax.rsqrt(var + GN_EPS) * gw_ref[...] + gb_ref[...] + bonus) * g
    hb_o[...] = hb[:nv].astype(BF16)
    sbuf[7:8] = last_row

    @pl.when(t == pl.num_programs(1) - 1)
    def _fin():
        S1_o[0] = S_sc[...]
        sh1_o[0] = last_row


def _rwkv(pb, S0, sh0, mu, w0, ww2, a0, wa2, wg2, k_k, k_a, r_k, gn_w, gn_b, bd, *, n_seq, L, row_off):
    nv = min(L, RWKV_CHUNK * RWKV_NCHUNK)
    nst = L // nv
    off = row_off // nv
    TB = -(-nv // RWKV_CHUNK) * RWKV_CHUNK
    tok = lambda n: pl.BlockSpec((nv, n), lambda s, t: (off + s * nst + t, 0))
    st = lambda shp: pl.BlockSpec((1,) + shp, lambda s, t: (s,) + (0,) * len(shp))
    full = lambda a: pl.BlockSpec(a.shape, lambda s, t: (0,) * a.ndim)
    consts = [mu, w0, ww2, a0, wa2, wg2, k_k, k_a, r_k, gn_w, gn_b, bd]
    npair = B_HEADS // 2
    in_specs = [tok(B_COLS_PAD), st((npair, PAIR, PAIR)), st((1, B_COLS_PAD))] + [full(a) for a in consts]
    args = [pb, S0, sh0] + consts
    out_shape = [jax.ShapeDtypeStruct((n_seq * L, B_WIDTH), BF16),
                 jax.ShapeDtypeStruct((n_seq, npair, PAIR, PAIR), F32),
                 jax.ShapeDtypeStruct((n_seq, 1, B_COLS_PAD), F32)]
    out_specs = [pl.BlockSpec((nv, B_WIDTH), lambda s, t: (s * nst + t, 0)),
                 st((npair, PAIR, PAIR)), st((1, B_COLS_PAD))]
    return pl.pallas_call(
        functools.partial(_rwkv_kernel, nv=nv),
        grid=(n_seq, nst),
        in_specs=in_specs, out_specs=out_specs, out_shape=out_shape,
        scratch_shapes=[pltpu.VMEM((npair, PAIR, PAIR), F32), pltpu.VMEM((TB + 8, B_COLS_PAD), F32)],
        compiler_params=_cparams(("arbitrary", "arbitrary")),
        name="rwkv",
    )(*args)


def _out_router_kernel(hap_ref, has_ref, hbp_ref, hbs_ref, xp_ref, xs_ref, woa_ref, wob_ref, nf_ref, wr_ref, br_ref,
                       h1_o, u2_o, ids_o, gates_o, rank_o, cnt_o, carry, *, ntp):
    i = pl.program_id(0)

    @pl.when(i == 0)
    def _init():
        carry[...] = jnp.zeros_like(carry)

    ha = _pick(i, ntp, hap_ref, has_ref)
    hb = _pick(i, ntp, hbp_ref, hbs_ref)
    h1 = _pick(i, ntp, xp_ref, xs_ref) + _dot(ha, woa_ref[...]) + _dot(hb, wob_ref[...])
    h1_o[...] = h1
    u2 = _rms(h1, nf_ref[...])
    u2_o[...] = u2
    logits = _dot3(u2, wr_ref[...]) + br_ref[...]
    TM = logits.shape[0]
    lane = lax.broadcasted_iota(I32, (TM, LANES), 1)
    lane_f = lane.astype(F32)
    work = logits
    vals, idxs = [], []
    for _ in range(TOP_K):
        m = jnp.max(work, axis=1, keepdims=True)
        idx = jnp.min(jnp.where(work == m, lane_f, float(LANES)), axis=1, keepdims=True)
        vals.append(m)
        idxs.append(idx)
        work = jnp.where(lane_f == idx, -jnp.inf, work)
    es = [jnp.exp(v - vals[0]) for v in vals]
    tot = es[0] + es[1] + es[2] + es[3]
    member = jnp.zeros((TM, LANES), F32)
    ids = jnp.zeros((TM, LANES), F32)
    gates = jnp.zeros((TM, LANES), F32)
    for kk in range(TOP_K):
        member = member + jnp.where(lane_f == idxs[kk], 1.0, 0.0)
        ids = jnp.where(lane == kk, idxs[kk], ids)
        gates = jnp.where(lane == kk, es[kk] / tot, gates)
    r2 = lax.broadcasted_iota(I32, (TM, TM), 0)
    c2 = lax.broadcasted_iota(I32, (TM, TM), 1)
    before = _dot(jnp.where(r2 > c2, 1.0, 0.0).astype(BF16), member.astype(BF16)) + carry[0:1]
    rank = jnp.zeros((TM, LANES), F32)
    for kk in range(TOP_K):
        rk = jnp.sum(jnp.where(lane_f == idxs[kk], before, 0.0), axis=1, keepdims=True)
        rank = jnp.where(lane == kk, rk, rank)
    ids_o[...] = ids.astype(I32)
    gates_o[...] = gates
    rank_o[...] = rank.astype(I32)
    carry[0:1] = carry[0:1] + jnp.sum(member, axis=0, keepdims=True)
    cnt_o[...] = carry[0:1].astype(I32)


def _out_router(hap, has, hbp, hbs, xp, xs, woa, wob, nf, wr, br):
    D = xp.shape[1]
    T = xp.shape[0] + xs.shape[0]
    ntp = xp.shape[0] // ROW_TILE
    nt = T // ROW_TILE
    row = lambda n: pl.BlockSpec((ROW_TILE, n), lambda i: (i, 0))
    full = lambda a: pl.BlockSpec(a.shape, lambda i: (0, 0))
    return pl.pallas_call(
        functools.partial(_out_router_kernel, ntp=ntp),
        grid=(nt,),
        in_specs=_two_specs(ntp, A_WIDTH) + _two_specs(ntp, B_WIDTH) + _two_specs(ntp, D)
        + [full(woa), full(wob), full(nf), full(wr), full(br)],
        out_specs=[row(D), row(D), row(LANES), row(LANES), row(LANES), pl.BlockSpec((1, LANES), lambda i: (0, 0))],
        out_shape=[jax.ShapeDtypeStruct((T, D), F32), jax.ShapeDtypeStruct((T, D), F32),
                   jax.ShapeDtypeStruct((T, LANES), I32), jax.ShapeDtypeStruct((T, LANES), F32),
                   jax.ShapeDtypeStruct((T, LANES), I32), jax.ShapeDtypeStruct((1, LANES), I32)],
        scratch_shapes=[pltpu.VMEM((8, LANES), F32)],
        compiler_params=_cparams(("arbitrary",)),
        name="out_router",
    )(hap, has, hbp, hbs, xp, xs, woa, wob, nf, wr, br)


MOE_BLOCK = 256


def _row_copy_wait(buf, sem, n):
    for _ in range(n):
        pltpu.make_async_copy(buf, buf, sem).wait()


def _scatter_kernel(dest_ref, u_ref, xs_in, xs_o, sem):
    del xs_in
    TM = u_ref.shape[0]

    def body(i, _):
        for kk in range(TOP_K):
            d = dest_ref[i * TOP_K + kk]
            pltpu.make_async_copy(u_ref.at[pl.ds(i, 1)], xs_o.at[pl.ds(d, 1)], sem).start()
        return 0

    lax.fori_loop(0, TM, body, 0)
    _row_copy_wait(u_ref, sem, TOP_K)


def _scatter_rows(dest_flat, u2, xs_init):
    T, D = u2.shape
    nt = T // ROW_TILE
    return pl.pallas_call(
        _scatter_kernel,
        grid=(nt,),
        in_specs=[pl.BlockSpec((ROW_TILE * TOP_K,), lambda i: (i,), memory_space=pltpu.SMEM),
                  pl.BlockSpec((ROW_TILE, D), lambda i: (i, 0)),
                  pl.BlockSpec(memory_space=pl.ANY)],
        out_specs=pl.BlockSpec(memory_space=pl.ANY),
        out_shape=jax.ShapeDtypeStruct(xs_init.shape, xs_init.dtype),
        scratch_shapes=[pltpu.SemaphoreType.DMA(())],
        input_output_aliases={2: 0},
        compiler_params=_cparams(("arbitrary",)),
        name="moe_scatter",
    )(dest_flat, u2, xs_init)


def _expert_kernel(be_ref, nu_ref, xs_ref, wgu_ref, bgu_ref, wd_ref, bd_ref, ys_o):
    j = pl.program_id(0)
    F = wd_ref.shape[1]

    @pl.when(j < nu_ref[0])
    def _compute():
        gu = _dot(xs_ref[...].astype(BF16), wgu_ref[0]) + bgu_ref[0]
        g_ = jnp.minimum(gu[:, :F], SWIGLU_LIMIT)
        up = jnp.clip(gu[:, F:], -SWIGLU_LIMIT, SWIGLU_LIMIT)
        act = (up + 1.0) * g_ * _sigmoid(SWIGLU_ALPHA * g_)
        ys_o[...] = _dot(act.astype(BF16), wd_ref[0]) + bd_ref[0]

    @pl.when(j >= nu_ref[0])
    def _unused():
        ys_o[...] = jnp.zeros_like(ys_o)


def _experts(blk_e, n_used, xs, wgu, bgu, wd, bd):
    NS, D = xs.shape
    nb = NS // MOE_BLOCK
    E, _, F2 = wgu.shape
    F = F2 // 2
    grid_spec = pltpu.PrefetchScalarGridSpec(
        num_scalar_prefetch=2,
        grid=(nb,),
        in_specs=[pl.BlockSpec((MOE_BLOCK, D), lambda j, be, nu: (jnp.minimum(j, nu[0] - 1), 0)),
                  pl.BlockSpec((1, D, F2), lambda j, be, nu: (be[j], 0, 0)),
                  pl.BlockSpec((1, 1, F2), lambda j, be, nu: (be[j], 0, 0)),
                  pl.BlockSpec((1, F, D), lambda j, be, nu: (be[j], 0, 0)),
                  pl.BlockSpec((1, 1, D), lambda j, be, nu: (be[j], 0, 0))],
        out_specs=pl.BlockSpec((MOE_BLOCK, D), lambda j, be, nu: (j, 0)),
    )
    return pl.pallas_call(
        _expert_kernel,
        grid_spec=grid_spec,
        out_shape=jax.ShapeDtypeStruct((NS, D), F32),
        compiler_params=_cparams(("arbitrary",)),
        name="moe_experts",
    )(blk_e, n_used, xs, wgu, bgu, wd, bd)


def _combine_kernel(dest_ref, h1_ref, pp_ref, ps_ref, gates_ref, ys_ref, npl_ref, wpg_ref, wpp_ref, fn_ref,
                    yp_o, ysm_o, gbuf, sem, *, ntp):
    TM = h1_ref.shape[0]
    t = pl.program_id(0)

    def body(i, _):
        for kk in range(TOP_K):
            d = dest_ref[i * TOP_K + kk]
            pltpu.make_async_copy(ys_ref.at[pl.ds(d, 1)], gbuf.at[kk, pl.ds(i, 1)], sem).start()
        return 0

    lax.fori_loop(0, TM, body, 0)
    _row_copy_wait(gbuf.at[0], sem, TOP_K)
    gates = gates_ref[...]
    h2 = h1_ref[...]
    for kk in range(TOP_K):
        h2 = h2 + gbuf[kk] * gates[:, kk:kk + 1]
    u3 = _rms(h2, npl_ref[...])
    gate = _sigmoid(_dot(u3.astype(BF16), wpg_ref[...]))
    p = _pick(t, ntp, pp_ref, ps_ref)
    h3 = h2 + gate * _dot(p.astype(BF16), wpp_ref[...])
    y = _rms(h3, fn_ref[...])

    @pl.when(t < ntp)
    def _prompt_rows():
        yp_o[...] = y

    @pl.when(t >= ntp)
    def _sample_rows():
        ysm_o[...] = y


def _combine(dest_flat, h1, pp, ps, gates, ys, npl, wpg, wpp, fn):
    T, D = h1.shape
    nt = T // ROW_TILE
    ntp = pp.shape[0] // ROW_TILE
    row = lambda n: pl.BlockSpec((ROW_TILE, n), lambda i: (i, 0))
    full = lambda a: pl.BlockSpec(a.shape, lambda i: (0, 0))
    return pl.pallas_call(
        functools.partial(_combine_kernel, ntp=ntp),
        grid=(nt,),
        in_specs=[pl.BlockSpec((ROW_TILE * TOP_K,), lambda i: (i,), memory_space=pltpu.SMEM), row(D)]
        + _two_specs(ntp, pp.shape[1])
        + [row(LANES), pl.BlockSpec(memory_space=pl.ANY), full(npl), full(wpg), full(wpp), full(fn)],
        out_specs=_two_specs(ntp, D),
        out_shape=[jax.ShapeDtypeStruct((pp.shape[0], D), F32), jax.ShapeDtypeStruct((ps.shape[0], D), F32)],
        scratch_shapes=[pltpu.VMEM((TOP_K, ROW_TILE, D), F32), pltpu.SemaphoreType.DMA(())],
        compiler_params=_cparams(("arbitrary",)),
        name="moe_combine_ple",
    )(dest_flat, h1, pp, ps, gates, ys, npl, wpg, wpp, fn)


def _pad_lora_cols(x):
    W3 = 3 * B_WIDTH
    z = jnp.zeros(x.shape[:-1] + (LORA_PAD - 64,), x.dtype)
    return jnp.concatenate([x[..., :W3 + 64], z, x[..., W3 + 64:W3 + 128], z, x[..., W3 + 128:]], axis=-1)


def _unpad_lora_cols(x):
    W3 = 3 * B_WIDTH
    return jnp.concatenate([x[..., :W3 + 64], x[..., W3 + LORA_PAD:W3 + LORA_PAD + 64], x[..., W3 + 2 * LORA_PAD:]], axis=-1)


def _heads_to_pairs(S):
    n = S.shape[0]
    S = S.reshape(n, B_HEADS // 2, 2, B_HDIM, B_HDIM)
    z = jnp.zeros_like(S[:, :, 0])
    top = jnp.concatenate([S[:, :, 0], z], axis=-1)
    bot = jnp.concatenate([z, S[:, :, 1]], axis=-1)
    return jnp.concatenate([top, bot], axis=-2)


def _pairs_to_heads(Sb):
    n = Sb.shape[0]
    a = Sb[:, :, :B_HDIM, :B_HDIM]
    b = Sb[:, :, B_HDIM:, B_HDIM:]
    return jnp.stack([a, b], axis=2).reshape(n, B_HEADS, B_HDIM, B_HDIM)


def kernel(x_prompt, x_sample, state_mlstm_C, state_mlstm_n, state_mlstm_m, state_mlstm_conv, state_rwkv_S, state_rwkv_shift, p_prompt, p_sample, norm_mix, w_in, conv_w, conv_b, b_ig, b_fg, a_norm, shift_mu, w0, w_w2, a0, w_a2, w_g2, k_k, k_a, r_k, gn_w, gn_b, w_out, norm_ffn, w_router, b_router, w_gu, b_gu, w_down, b_down, norm_ple, w_ple_gate, w_ple_proj, final_norm):
    assert norm_mix.shape[0] == 1, "single-layer trunk"
    nbp, Lp, D = x_prompt.shape
    nbs, Ls, _ = x_sample.shape
    Tp, Ts = nbp * Lp, nbs * Ls
    T = Tp + Ts
    row = lambda a: a.reshape(1, -1).astype(F32)

    wi = w_in[0]
    AW2 = 2 * A_WIDTH
    wqk = wi[:, :AW2].astype(BF16)
    wv = wi[:, AW2:AW2 + A_WIDTH].astype(BF16)
    wo = wi[:, AW2 + A_WIDTH:AW2 + 2 * A_WIDTH].astype(BF16)
    a_cols = 4 * A_WIDTH + 2 * A_HEADS
    wif = jnp.pad(wi[:, 4 * A_WIDTH:a_cols], ((0, 0), (0, LANES - 2 * A_HEADS)))
    wb = _pad_lora_cols(wi[:, a_cols:]).astype(BF16)
    gate_bias = jnp.pad(jnp.concatenate([b_ig[0], b_fg[0]]), (0, LANES - 2 * A_HEADS)).reshape(1, LANES)
    padw = lambda w: jnp.pad(w, ((0, LORA_PAD - w.shape[0]), (0, 0)))
    hid = jnp.arange(B_WIDTH) // B_HDIM
    bd = (hid[:, None] == hid[None, :]).astype(BF16)
    E = w_router.shape[-1]
    wr = jnp.pad(w_router[0], ((0, 0), (0, LANES - E)))
    br = jnp.pad(b_router[0], (0, LANES - E), constant_values=NEG_BIG).reshape(1, LANES)

    assert Tp % ROW_TILE == 0 and Ts % ROW_TILE == 0
    xp, xsm = x_prompt.reshape(Tp, D), x_sample.reshape(Ts, D)
    qk, v, og, ifg, pb = _in_proj(xp, xsm, row(norm_mix), wqk, wv, wo, wif, wb)

    f0 = lambda *shp: jnp.zeros(shp, F32)
    ml_consts = (conv_w[0], row(conv_b), gate_bias, row(a_norm))
    m_s = jnp.pad(state_mlstm_m[0], ((0, 0), (0, LANES - A_HEADS))).reshape(nbs, 1, LANES)
    hap, Cp, np_, mp, convp = _mlstm(qk, v, og, ifg, f0(nbp, A_HEADS, A_HDIM, A_HDIM), f0(nbp, A_HEADS, A_HDIM),
                                     f0(nbp, 1, LANES), f0(nbp, CONV_W - 1, AW2), *ml_consts,
                                     n_seq=nbp, L=Lp, row_off=0)
    has, Cs, ns_, ms, convs = _mlstm(qk, v, og, ifg, state_mlstm_C[0], state_mlstm_n[0], m_s, state_mlstm_conv[0],
                                     *ml_consts, n_seq=nbs, L=Ls, row_off=Tp)

    rw_consts = (row(_pad_lora_cols(shift_mu[0])), row(w0), padw(w_w2[0]), row(a0), padw(w_a2[0]), w_g2[0].astype(BF16),
                 row(k_k), row(k_a), row(r_k), row(gn_w), row(gn_b), bd)
    npair = B_HEADS // 2
    hbp, Sp, shp_ = _rwkv(pb, f0(nbp, npair, PAIR, PAIR), f0(nbp, 1, B_COLS_PAD), *rw_consts,
                          n_seq=nbp, L=Lp, row_off=0)
    hbs, Ss, shs = _rwkv(pb, _heads_to_pairs(state_rwkv_S[0]), _pad_lora_cols(state_rwkv_shift[0]), *rw_consts,
                         n_seq=nbs, L=Ls, row_off=Tp)

    wout = w_out[0].astype(BF16)
    h1, u2, ids, gates, rank, counts = _out_router(hap, has, hbp, hbs, xp, xsm, wout[:A_WIDTH], wout[A_WIDTH:],
                                                   row(norm_ffn), wr, br)

    counts = counts[0, :E]
    pcounts = (counts + MOE_BLOCK - 1) // MOE_BLOCK * MOE_BLOCK
    pend = jnp.cumsum(pcounts)
    pstart = pend - pcounts
    n_blk = -(-T * TOP_K // MOE_BLOCK) + E
    dest = (jnp.take(pstart, ids[:, :TOP_K]) + rank[:, :TOP_K]).reshape(-1).astype(I32)
    blk_e = jnp.minimum(jnp.searchsorted(pend, jnp.arange(n_blk) * MOE_BLOCK, side='right'), E - 1).astype(I32)
    n_used = (pend[E - 1:] // MOE_BLOCK).astype(I32)

    xs = _scatter_rows(dest, u2, jnp.zeros((n_blk * MOE_BLOCK, D), F32))
    ys = _experts(blk_e, n_used, xs, w_gu[0].astype(BF16), b_gu[0][:, None, :], w_down[0].astype(BF16), b_down[0][:, None, :])
    yp, ysm = _combine(dest, h1, p_prompt[0].reshape(Tp, -1), p_sample[0].reshape(Ts, -1), gates, ys, row(norm_ple),
                       w_ple_gate[0].astype(BF16), w_ple_proj[0].astype(BF16), row(final_norm))

    lead = lambda a: a[None]
    return (yp.reshape(nbp, Lp, D), ysm.reshape(nbs, Ls, D),
            lead(Cp), lead(np_), lead(mp[:, 0, :A_HEADS]), lead(convp), lead(_pairs_to_heads(Sp)), lead(_unpad_lora_cols(shp_)),
            lead(Cs), lead(ns_), lead(ms[:, 0, :A_HEADS]), lead(convs), lead(_pairs_to_heads(Ss)), lead(_unpad_lora_cols(shs)))
```

```python
import functools

import jax
import jax.numpy as jnp
from jax import lax
from jax.experimental import pallas as pl
from jax.experimental.pallas import tpu as pltpu

F32 = jnp.float32
BF16 = jnp.bfloat16
I32 = jnp.int32

EPS = 1e-6
GN_EPS = 64e-5
A_HEADS, A_HDIM = 4, 128
B_HEADS, B_HDIM = 8, 64
A_WIDTH = A_HEADS * A_HDIM
B_WIDTH = B_HEADS * B_HDIM
CONV_W = 4
LORA_PAD = 128
N_EXPERTS, TOP_K = 32, 4
SWIGLU_LIMIT, SWIGLU_ALPHA = 7.0, 1.702
LANES = 128
ROW_TILE = 256
VMEM_LIMIT = 56 * 1024 * 1024


def _cparams(sem):
    return pltpu.CompilerParams(dimension_semantics=sem, vmem_limit_bytes=VMEM_LIMIT)


def _dot(a, b):
    return lax.dot_general(a, b, (((1,), (0,)), ((), ())), preferred_element_type=F32)


def _dot_nt(a, b):
    return lax.dot_general(a, b, (((1,), (1,)), ((), ())), preferred_element_type=F32)


def _dot_tn(a, b):
    return lax.dot_general(a, b, (((0,), (0,)), ((), ())), preferred_element_type=F32)


def _split2(x):
    hi = x.astype(BF16)
    lo = (x - hi.astype(F32)).astype(BF16)
    return hi, lo


def _split3(x):
    hi = x.astype(BF16)
    r1 = x - hi.astype(F32)
    mid = r1.astype(BF16)
    lo = (r1 - mid.astype(F32)).astype(BF16)
    return hi, mid, lo


def _dot3(a, b, f=_dot):
    ah, al = _split2(a)
    bh, bl = _split2(b)
    return f(ah, bh) + (f(ah, bl) + f(al, bh))


def _dotx(a, b_exact, f=_dot):
    a1, a2, a3 = _split3(a)
    return f(a1, b_exact) + (f(a2, b_exact) + f(a3, b_exact))


def _dotx_l(a_exact, b, f=_dot):
    b1, b2, b3 = _split3(b)
    return f(a_exact, b1) + (f(a_exact, b2) + f(a_exact, b3))


def _rms(x, g):
    return x * lax.rsqrt(jnp.mean(x * x, axis=-1, keepdims=True) + EPS) * g


def _sigmoid(x):
    return 1.0 / (1.0 + jnp.exp(-x))


def _two_specs(ntp, n):
    return [pl.BlockSpec((ROW_TILE, n), lambda i: (jnp.minimum(i, ntp - 1), 0)),
            pl.BlockSpec((ROW_TILE, n), lambda i: (jnp.maximum(i - ntp, 0), 0))]


def _pick(i, ntp, p_ref, s_ref):
    return jnp.where(i < ntp, p_ref[...], s_ref[...])


def _in_proj_kernel(xp_ref, xs_ref, g_ref, wqk_ref, wv_ref, wo_ref, wif_ref, wb_ref,
                    qk_o, v_o, og_o, if_o, pb_o, *, ntp):
    x = _pick(pl.program_id(0), ntp, xp_ref, xs_ref)
    u = _rms(x, g_ref[...])
    ub = u.astype(BF16)
    qk_o[...] = _dot(ub, wqk_ref[...])
    v_o[...] = _dot(ub, wv_ref[...])
    og_o[...] = _dot(ub, wo_ref[...])
    if_o[...] = _dot3(u, wif_ref[...])
    pb_o[...] = _dot(ub, wb_ref[...])


def _in_proj(xp, xs, g, wqk, wv, wo, wif, wb):
    D = xp.shape[1]
    T = xp.shape[0] + xs.shape[0]
    ntp = xp.shape[0] // ROW_TILE
    row = lambda n: pl.BlockSpec((ROW_TILE, n), lambda i: (i, 0))
    full = lambda a: pl.BlockSpec(a.shape, lambda i: (0, 0))
    outs = [wqk.shape[1], wv.shape[1], wo.shape[1], wif.shape[1], wb.shape[1]]
    return pl.pallas_call(
        functools.partial(_in_proj_kernel, ntp=ntp),
        grid=(T // ROW_TILE,),
        in_specs=_two_specs(ntp, D) + [full(g), full(wqk), full(wv), full(wo), full(wif), full(wb)],
        out_specs=[row(n) for n in outs],
        out_shape=[jax.ShapeDtypeStruct((T, n), F32) for n in outs],
        compiler_params=_cparams(("arbitrary",)),
        name="in_proj",
    )(xp, xs, g, wqk, wv, wo, wif, wb)


MLSTM_CHUNK = 128
NEG_BIG = -1e30


def _log_sigmoid(x):
    return jnp.minimum(x, 0.0) - jnp.log1p(jnp.exp(-jnp.abs(x)))


def _mlstm_kernel(qk_ref, v_ref, og_ref, if_ref, C0_ref, n0_ref, m0_ref, conv0_ref,
                  cw_ref, cb_ref, gb_ref, an_ref,
                  ha_o, C1_o, n1_o, m1_o, conv1_o, C_sc, n_sc, m_sc, cbuf, *, nv):
    CH = MLSTM_CHUNK
    c = pl.program_id(1)

    @pl.when(c == 0)
    def _init():
        C_sc[...] = C0_ref[0]
        n_sc[0:A_HEADS] = n0_ref[0]
        m_sc[0:1] = m0_ref[0]
        cbuf[5:8] = conv0_ref[0]

    cbuf[8:8 + nv] = qk_ref[...]
    if nv < CH:
        cbuf[8 + nv:8 + CH] = jnp.zeros((CH - nv, 2 * A_WIDTH), F32)
    y = cb_ref[...] + cbuf[5:5 + CH] * cw_ref[0:1]
    for j in range(1, CONV_W):
        y = y + cbuf[5 + j:5 + j + CH] * cw_ref[j:j + 1]
    conv_tail = cbuf[5 + nv:8 + nv]
    qk = y * _sigmoid(y)
    q = qk[:, :A_WIDTH]
    k = qk[:, A_WIDTH:] * (A_HDIM ** -0.5)
    if nv == CH:
        v = v_ref[...]
        og = og_ref[...]
        gp = if_ref[...] + gb_ref[...]
    else:
        zpad = lambda n: jnp.zeros((CH - nv, n), F32)
        v = jnp.concatenate([v_ref[...], zpad(A_WIDTH)], axis=0)
        og = jnp.concatenate([og_ref[...], zpad(A_WIDTH)], axis=0)
        gp = jnp.concatenate([if_ref[...] + gb_ref[...], zpad(LANES)], axis=0)

    row = lax.broadcasted_iota(I32, (CH, LANES), 0)
    li = gp
    lf = pltpu.roll(_log_sigmoid(gp), LANES - A_HEADS, 1)
    if nv < CH:
        li = jnp.where(row < nv, li, NEG_BIG)
        lf = jnp.where(row < nv, lf, 0.0)
    r2 = lax.broadcasted_iota(I32, (CH, CH), 0)
    c2 = lax.broadcasted_iota(I32, (CH, CH), 1)
    tri = r2 >= c2
    tril_b = jnp.where(tri, 1.0, 0.0).astype(BF16)
    b = _dotx_l(tril_b, lf)
    g = li - b
    sel = jnp.where(lax.broadcasted_iota(I32, (8, LANES), 0) == lax.broadcasted_iota(I32, (8, LANES), 1),
                    1.0, 0.0).astype(BF16)
    gT = _dotx_l(sel, g, f=_dot_nt)

    qb, kb, vb = q.astype(BF16), k.astype(BF16), v.astype(BF16)
    lane1 = lax.broadcasted_iota(I32, (1, LANES), 1)
    m_row = m_sc[0:1]
    H = range(A_HEADS)
    sls = [slice(h * A_HDIM, (h + 1) * A_HDIM) for h in H]
    bcol = [b[:, h:h + 1] for h in H]
    m_h = [m_row[:, h:h + 1] for h in H]
    C_h = [C_sc[h] for h in H]
    n_h = [n_sc[h:h + 1] for h in H]
    qk_t = [_dot_nt(qb[:, sl], kb[:, sl]) for sl in sls]
    qC = [_dot_nt(qb[:, sl], C_h[h].astype(BF16)) for h, sl in enumerate(sls)]
    dm = [jnp.where(tri, bcol[h] + gT[h:h + 1, :], -jnp.inf) for h in H]
    inter = [bcol[h] + m_h[h] for h in H]
    mt = [jnp.maximum(inter[h], jnp.max(dm[h], axis=1, keepdims=True)) for h in H]
    s = [qk_t[h] * jnp.exp(dm[h] - mt[h]) for h in H]
    wi = [jnp.exp(inter[h] - mt[h]) for h in H]
    sv = [_dot(s[h].astype(BF16), vb[:, sl]) for h, sl in enumerate(sls)]
    m_new = [mt[h][CH - 1:CH] for h in H]
    w_end = [jnp.exp(bcol[h][CH - 1:CH] + g[:, h:h + 1] - m_new[h]) for h in H]
    dec = [jnp.exp(bcol[h][CH - 1:CH] + m_h[h] - m_new[h]) for h in H]
    vk = [_dot_tn((v[:, sl] * w_end[h]).astype(BF16), kb[:, sl]) for h, sl in enumerate(sls)]
    outs = []
    m_next = m_row
    for h, sl in enumerate(sls):
        num = sv[h] + wi[h] * qC[h]
        den = jnp.sum(s[h], axis=1, keepdims=True) + wi[h] * jnp.sum(q[:, sl] * n_h[h], axis=1, keepdims=True)
        hh = num / jnp.maximum(jnp.abs(den), jnp.exp(-mt[h]))
        hn = hh * lax.rsqrt(jnp.mean(hh * hh, axis=1, keepdims=True) + EPS)
        outs.append(hn * an_ref[:, sl] * _sigmoid(og[:, sl]))
        C_sc[h] = dec[h] * C_h[h] + vk[h]
        n_sc[h:h + 1] = dec[h] * n_h[h] + jnp.sum(k[:, sl] * w_end[h], axis=0, keepdims=True)
        m_next = jnp.where(lane1 == h, m_new[h], m_next)
    m_sc[0:1] = m_next
    ha = jnp.concatenate(outs, axis=1)
    ha_o[...] = ha[:nv].astype(BF16)
    cbuf[5:8] = conv_tail

    @pl.when(c == pl.num_programs(1) - 1)
    def _fin():
        C1_o[0] = C_sc[...]
        n1_o[0] = n_sc[0:A_HEADS]
        m1_o[0] = m_sc[0:1]
        conv1_o[0] = conv_tail


def _mlstm(qk, v, og, ifg, C0, n0, m0, conv0, cw, cb, gb, an, *, n_seq, L, row_off):
    CH = MLSTM_CHUNK
    nv = min(L, CH)
    nch = L // nv
    off = row_off // nv
    tok = lambda n: pl.BlockSpec((nv, n), lambda s, c: (off + s * nch + c, 0))
    st = lambda shp: pl.BlockSpec((1,) + shp, lambda s, c: (s,) + (0,) * len(shp))
    full = lambda a: pl.BlockSpec(a.shape, lambda s, c: (0,) * a.ndim)
    in_specs = [tok(2 * A_WIDTH), tok(A_WIDTH), tok(A_WIDTH), tok(LANES),
                st((A_HEADS, A_HDIM, A_HDIM)), st((A_HEADS, A_HDIM)), st((1, LANES)),
                st((CONV_W - 1, 2 * A_WIDTH)), full(cw), full(cb), full(gb), full(an)]
    args = [qk, v, og, ifg, C0, n0, m0, conv0, cw, cb, gb, an]
    out_shape = [jax.ShapeDtypeStruct((n_seq * L, A_WIDTH), BF16),
                 jax.ShapeDtypeStruct((n_seq, A_HEADS, A_HDIM, A_HDIM), F32),
                 jax.ShapeDtypeStruct((n_seq, A_HEADS, A_HDIM), F32),
                 jax.ShapeDtypeStruct((n_seq, 1, LANES), F32),
                 jax.ShapeDtypeStruct((n_seq, CONV_W - 1, 2 * A_WIDTH), F32)]
    out_specs = [pl.BlockSpec((nv, A_WIDTH), lambda s, c: (s * nch + c, 0)),
                 st((A_HEADS, A_HDIM, A_HDIM)), st((A_HEADS, A_HDIM)),
                 st((1, LANES)), st((CONV_W - 1, 2 * A_WIDTH))]
    return pl.pallas_call(
        functools.partial(_mlstm_kernel, nv=nv),
        grid=(n_seq, nch),
        in_specs=in_specs, out_specs=out_specs, out_shape=out_shape,
        scratch_shapes=[pltpu.VMEM((A_HEADS, A_HDIM, A_HDIM), F32), pltpu.VMEM((8, A_HDIM), F32),
                        pltpu.VMEM((8, LANES), F32), pltpu.VMEM((CH + 8, 2 * A_WIDTH), F32)],
        compiler_params=_cparams(("arbitrary", "arbitrary")),
        name="mlstm",
    )(*args)


RWKV_CHUNK = 64
RWKV_NCHUNK = 2
B_COLS_PAD = 3 * B_WIDTH + 2 * LORA_PAD + 128
PAIR = 2 * B_HDIM


def _softplus(z):
    return jnp.maximum(z, 0.0) + jnp.log1p(jnp.exp(-jnp.abs(z)))


def _rwkv_kernel(pb_ref, S0_ref, sh0_ref, mu_ref, w0_ref, ww2_ref, a0_ref, wa2_ref, wg2_ref,
                 kk_ref, ka_ref, rk_ref, gw_ref, gb_ref, bd_ref,
                 hb_o, S1_o, sh1_o, S_sc, sbuf, *, nv):
    C = RWKV_CHUNK
    nc = -(-nv // C)
    TB = nc * C
    t = pl.program_id(1)

    @pl.when(t == 0)
    def _init():
        S_sc[...] = S0_ref[0]
        sbuf[7:8] = sh0_ref[0]

    sbuf[8:8 + nv] = pb_ref[...]
    if nv < TB:
        sbuf[8 + nv:8 + TB] = jnp.zeros((TB - nv, B_COLS_PAD), F32)
    pb = sbuf[8:8 + TB]
    prev = sbuf[7:7 + TB]
    last_row = sbuf[7 + nv:8 + nv]
    xm = pb + (prev - pb) * mu_ref[...]
    W = B_WIDTH
    r = xm[:, 0:W]
    kb = xm[:, W:2 * W]
    vb = xm[:, 2 * W:3 * W]
    wl = xm[:, 3 * W:3 * W + LORA_PAD]
    al = xm[:, 3 * W + LORA_PAD:3 * W + 2 * LORA_PAD]
    gl = xm[:, 3 * W + 2 * LORA_PAD:]
    w_log = -_softplus(-(w0_ref[...] + _dot3(jnp.tanh(wl), ww2_ref[...]))) - 0.5
    lw = -jnp.exp(w_log)
    a = _sigmoid(a0_ref[...] + _dot3(al, wa2_ref[...]))
    g = _dot(_sigmoid(gl).astype(BF16), wg2_ref[...])
    bd = bd_ref[...]
    kkr = kb * kk_ref[...]
    kk = kkr / jnp.maximum(jnp.sqrt(_dotx(kkr * kkr, bd)), 1e-12)
    k = kb * (1.0 + (a - 1.0) * ka_ref[...])
    bonus = _dotx(r * k * rk_ref[...], bd) * vb
    if nv < TB:
        valid = lax.broadcasted_iota(I32, (TB, W), 0) < nv
        zero = lambda z: jnp.where(valid, z, 0.0)
        lw, kk, k, vb = zero(lw), zero(kk), zero(k), zero(vb)

    lane = lax.broadcasted_iota(I32, (C, PAIR), 1)
    h0 = lane < B_HDIM
    sep = lambda z: jnp.concatenate([jnp.where(h0, z, 0.0), jnp.where(h0, 0.0, z)], axis=0)
    ri = lax.broadcasted_iota(I32, (C, C), 0)
    ci = lax.broadcasted_iota(I32, (C, C), 1)
    tril_c = jnp.where(ri >= ci, 1.0, 0.0).astype(BF16)
    r2 = lax.broadcasted_iota(I32, (PAIR, PAIR), 0)
    c2 = lax.broadcasted_iota(I32, (PAIR, PAIR), 1)
    low_s = r2 > c2
    low_i = r2 >= c2
    eye = jnp.where(r2 == c2, 1.0, 0.0)
    npair = B_HEADS // 2
    units = [(c, p) for c in range(nc) for p in range(npair)]
    cat0 = lambda x, y: jnp.concatenate([x, y], axis=0)
    cat1 = lambda x, y: jnp.concatenate([x, y], axis=1)
    bf = lambda z: z.astype(BF16)
    lhs, rhs_t, vs, wc = [], [], [], []
    for (c, p) in units:
        rs = slice(c * C, (c + 1) * C)
        ls = slice(p * PAIR, (p + 1) * PAIR)
        lwc = lw[rs, ls]
        b = _dotx_l(tril_c, lwc)
        emb = jnp.exp(-b)
        kkc = kk[rs, ls]
        lhs.append(bf(cat0(sep(kkc * jnp.exp(b - lwc)), sep(r[rs, ls] * jnp.exp(b)))))
        rhs_t.append(bf(cat0(sep(k[rs, ls] * emb), sep(kkc * a[rs, ls] * emb))))
        vs.append(bf(sep(vb[rs, ls])))
        wc.append(jnp.exp(b[C - 1:C]))
    A = [_dot_nt(l_, r_) for l_, r_ in zip(lhs, rhs_t)]
    Ak = [jnp.where(low_s, x[:PAIR, :PAIR], 0.0) for x in A]
    Ab = [jnp.where(low_s, x[:PAIR, PAIR:], 0.0) for x in A]
    Ark = [jnp.where(low_i, x[PAIR:, :PAIR], 0.0) for x in A]
    Arb = [jnp.where(low_i, x[PAIR:, PAIR:], 0.0) for x in A]
    X2 = [_dot(bf(cat0(x, y)), v_) for x, y, v_ in zip(Ak, Ark, vs)]
    VtK = [_dot_tn(v_, r_[:PAIR]) for v_, r_ in zip(vs, rhs_t)]
    Tm = [eye - x for x in Ab]
    P = [_dot(bf(x), bf(x)) for x in Ab]
    n = 2
    while 2 * n < C:
        X = [_dot(bf(cat0(t_, p_)), bf(p_)) for t_, p_ in zip(Tm, P)]
        Tm = [t_ + x[:PAIR] for t_, x in zip(Tm, X)]
        P = [x[PAIR:] for x in X]
        n *= 2
    Tm = [t_ + _dot(bf(t_), bf(p_)) for t_, p_ in zip(Tm, P)]
    Y = [_dot(bf(t_), cat1(l_[:PAIR], bf(x[:PAIR]))) for t_, l_, x in zip(Tm, lhs, X2)]
    Z = [_dot(bf(x), bf(y)) for x, y in zip(Arb, Y)]
    Rp = [bf(l_[PAIR:].astype(F32) - z[:, :PAIR]) for l_, z in zip(lhs, Z)]
    Op = [x[PAIR:] - z[:, PAIR:] for x, z in zip(X2, Z)]
    W = [_dot_tn(bf(y), r_[PAIR:]) for y, r_ in zip(Y, rhs_t)]
    Mc = [bf(w_[:PAIR] * c_) for w_, c_ in zip(W, wc)]
    Nn = [(v_ - w_[PAIR:]) * c_ for v_, w_, c_ in zip(VtK, W, wc)]
    o_rows = []
    for c in range(nc):
        S = [S_sc[p] for p in range(npair)]
        Sb = [bf(s_) for s_ in S]
        u0 = c * npair
        O = [_dot_nt(Rp[u0 + p], Sb[p]) + Op[u0 + p] for p in range(npair)]
        for p in range(npair):
            S_sc[p] = S[p] * wc[u0 + p] - _dot(Sb[p], Mc[u0 + p]) + Nn[u0 + p]
        o_rows.append(jnp.concatenate([x[:C] + x[C:] for x in O], axis=1))
    o = o_rows[0] if nc == 1 else jnp.concatenate(o_rows, axis=0)

    mean = _dotx(o, bd) * (1.0 / B_HDIM)
    d = o - mean
    var = _dotx(d * d, bd) * (1.0 / B_HDIM)
    hb = (d * lax.rsqrt(var + GN_EPS) * gw_ref[...] + gb_ref[...] + bonus) * g
    hb_o[...] = hb[:nv].astype(BF16)
    sbuf[7:8] = last_row

    @pl.when(t == pl.num_programs(1) - 1)
    def _fin():
        S1_o[0] = S_sc[...]
        sh1_o[0] = last_row


def _rwkv(pb, S0, sh0, mu, w0, ww2, a0, wa2, wg2, k_k, k_a, r_k, gn_w, gn_b, bd, *, n_seq, L, row_off):
    nv = min(L, RWKV_CHUNK * RWKV_NCHUNK)
    nst = L // nv
    off = row_off // nv
    TB = -(-nv // RWKV_CHUNK) * RWKV_CHUNK
    tok = lambda n: pl.BlockSpec((nv, n), lambda s, t: (off + s * nst + t, 0))
    st = lambda shp: pl.BlockSpec((1,) + shp, lambda s, t: (s,) + (0,) * len(shp))
    full = lambda a: pl.BlockSpec(a.shape, lambda s, t: (0,) * a.ndim)
    consts = [mu, w0, ww2, a0, wa2, wg2, k_k, k_a, r_k, gn_w, gn_b, bd]
    npair = B_HEADS // 2
    in_specs = [tok(B_COLS_PAD), st((npair, PAIR, PAIR)), st((1, B_COLS_PAD))] + [full(a) for a in consts]
    args = [pb, S0, sh0] + consts
    out_shape = [jax.ShapeDtypeStruct((n_seq * L, B_WIDTH), BF16),
                 jax.ShapeDtypeStruct((n_seq, npair, PAIR, PAIR), F32),
                 jax.ShapeDtypeStruct((n_seq, 1, B_COLS_PAD), F32)]
    out_specs = [pl.BlockSpec((nv, B_WIDTH), lambda s, t: (s * nst + t, 0)),
                 st((npair, PAIR, PAIR)), st((1, B_COLS_PAD))]
    return pl.pallas_call(
        functools.partial(_rwkv_kernel, nv=nv),
        grid=(n_seq, nst),
        in_specs=in_specs, out_specs=out_specs, out_shape=out_shape,
        scratch_shapes=[pltpu.VMEM((npair, PAIR, PAIR), F32), pltpu.VMEM((TB + 8, B_COLS_PAD), F32)],
        compiler_params=_cparams(("arbitrary", "arbitrary")),
        name="rwkv",
    )(*args)


def _out_router_kernel(hap_ref, has_ref, hbp_ref, hbs_ref, xp_ref, xs_ref, woa_ref, wob_ref, nf_ref, wr_ref, br_ref,
                       h1_o, u2_o, ids_o, gates_o, rank_o, cnt_o, carry, *, ntp):
    i = pl.program_id(0)

    @pl.when(i == 0)
    def _init():
        carry[...] = jnp.zeros_like(carry)

    ha = _pick(i, ntp, hap_ref, has_ref)
    hb = _pick(i, ntp, hbp_ref, hbs_ref)
    h1 = _pick(i, ntp, xp_ref, xs_ref) + _dot(ha, woa_ref[...]) + _dot(hb, wob_ref[...])
    h1_o[...] = h1
    u2 = _rms(h1, nf_ref[...])
    u2_o[...] = u2
    logits = _dot3(u2, wr_ref[...]) + br_ref[...]
    TM = logits.shape[0]
    lane = lax.broadcasted_iota(I32, (TM, LANES), 1)
    lane_f = lane.astype(F32)
    work = logits
    vals, idxs = [], []
    for _ in range(TOP_K):
        m = jnp.max(work, axis=1, keepdims=True)
        idx = jnp.min(jnp.where(work == m, lane_f, float(LANES)), axis=1, keepdims=True)
        vals.append(m)
        idxs.append(idx)
        work = jnp.where(lane_f == idx, -jnp.inf, work)
    es = [jnp.exp(v - vals[0]) for v in vals]
    tot = es[0] + es[1] + es[2] + es[3]
    member = jnp.zeros((TM, LANES), F32)
    ids = jnp.zeros((TM, LANES), F32)
    gates = jnp.zeros((TM, LANES), F32)
    for kk in range(TOP_K):
        member = member + jnp.where(lane_f == idxs[kk], 1.0, 0.0)
        ids = jnp.where(lane == kk, idxs[kk], ids)
        gates = jnp.where(lane == kk, es[kk] / tot, gates)
    r2 = lax.broadcasted_iota(I32, (TM, TM), 0)
    c2 = lax.broadcasted_iota(I32, (TM, TM), 1)
    before = _dot(jnp.where(r2 > c2, 1.0, 0.0).astype(BF16), member.astype(BF16)) + carry[0:1]
    rank = jnp.zeros((TM, LANES), F32)
    for kk in range(TOP_K):
        rk = jnp.sum(jnp.where(lane_f == idxs[kk], before, 0.0), axis=1, keepdims=True)
        rank = jnp.where(lane == kk, rk, rank)
    ids_o[...] = ids.astype(I32)
    gates_o[...] = gates
    rank_o[...] = rank.astype(I32)
    carry[0:1] = carry[0:1] + jnp.sum(member, axis=0, keepdims=True)
    cnt_o[...] = carry[0:1].astype(I32)


def _out_router(hap, has, hbp, hbs, xp, xs, woa, wob, nf, wr, br):
    D = xp.shape[1]
    T = xp.shape[0] + xs.shape[0]
    ntp = xp.shape[0] // ROW_TILE
    nt = T // ROW_TILE
    row = lambda n: pl.BlockSpec((ROW_TILE, n), lambda i: (i, 0))
    full = lambda a: pl.BlockSpec(a.shape, lambda i: (0, 0))
    return pl.pallas_call(
        functools.partial(_out_router_kernel, ntp=ntp),
        grid=(nt,),
        in_specs=_two_specs(ntp, A_WIDTH) + _two_specs(ntp, B_WIDTH) + _two_specs(ntp, D)
        + [full(woa), full(wob), full(nf), full(wr), full(br)],
        out_specs=[row(D), row(D), row(LANES), row(LANES), row(LANES), pl.BlockSpec((1, LANES), lambda i: (0, 0))],
        out_shape=[jax.ShapeDtypeStruct((T, D), F32), jax.ShapeDtypeStruct((T, D), F32),
                   jax.ShapeDtypeStruct((T, LANES), I32), jax.ShapeDtypeStruct((T, LANES), F32),
                   jax.ShapeDtypeStruct((T, LANES), I32), jax.ShapeDtypeStruct((1, LANES), I32)],
        scratch_shapes=[pltpu.VMEM((8, LANES), F32)],
        compiler_params=_cparams(("arbitrary",)),
        name="out_router",
    )(hap, has, hbp, hbs, xp, xs, woa, wob, nf, wr, br)


MOE_BLOCK = 256


def _row_copy_wait(buf, sem, n):
    for _ in range(n):
        pltpu.make_async_copy(buf, buf, sem).wait()


def _scatter_kernel(plo_ref, phi_ref, nu_ref, dest_ref, u_ref, xs_o, zbuf, sem):
    TM = u_ref.shape[0]

    @pl.when(pl.program_id(0) == 0)
    def _zero_unused_slots():
        zbuf[...] = jnp.zeros_like(zbuf)
        zrow = zbuf.at[pl.ds(0, 1)]

        def per_expert(e, n):
            def per_row(rw, _):
                pltpu.make_async_copy(zrow, xs_o.at[pl.ds(rw, 1)], sem).start()
                return 0

            lax.fori_loop(plo_ref[e], phi_ref[e], per_row, 0)
            return n + (phi_ref[e] - plo_ref[e])

        n_rows = lax.fori_loop(0, plo_ref.shape[0], per_expert, 0)

        def wait_row(_, c):
            pltpu.make_async_copy(zrow, xs_o.at[pl.ds(0, 1)], sem).wait()
            return c

        lax.fori_loop(0, n_rows, wait_row, 0)

        def per_block(j, _):
            cp = pltpu.make_async_copy(zbuf, xs_o.at[pl.ds(pl.multiple_of(j * MOE_BLOCK, MOE_BLOCK), MOE_BLOCK)], sem)
            cp.start()
            cp.wait()
            return 0

        lax.fori_loop(nu_ref[0], xs_o.shape[0] // MOE_BLOCK, per_block, 0)

    def body(i, _):
        for kk in range(TOP_K):
            d = dest_ref[i * TOP_K + kk]
            pltpu.make_async_copy(u_ref.at[pl.ds(i, 1)], xs_o.at[pl.ds(d, 1)], sem).start()
        return 0

    lax.fori_loop(0, TM, body, 0)
    _row_copy_wait(u_ref, sem, TOP_K)


def _scatter_rows(pad_lo, pad_hi, n_used, dest_flat, u2, n_slots):
    T, D = u2.shape
    grid_spec = pltpu.PrefetchScalarGridSpec(
        num_scalar_prefetch=3,
        grid=(T // ROW_TILE,),
        in_specs=[pl.BlockSpec((ROW_TILE * TOP_K,), lambda i, *_: (i,), memory_space=pltpu.SMEM),
                  pl.BlockSpec((ROW_TILE, D), lambda i, *_: (i, 0))],
        out_specs=pl.BlockSpec(memory_space=pl.ANY),
        scratch_shapes=[pltpu.VMEM((MOE_BLOCK, D), F32), pltpu.SemaphoreType.DMA(())],
    )
    return pl.pallas_call(
        _scatter_kernel,
        grid_spec=grid_spec,
        out_shape=jax.ShapeDtypeStruct((n_slots, D), F32),
        compiler_params=_cparams(("arbitrary",)),
        name="moe_scatter",
    )(pad_lo, pad_hi, n_used, dest_flat, u2)


def _expert_kernel(be_ref, nu_ref, xs_ref, wgu_ref, bgu_ref, wd_ref, bd_ref, ys_o):
    j = pl.program_id(0)
    F = wd_ref.shape[1]

    @pl.when(j < nu_ref[0])
    def _compute():
        gu = _dot(xs_ref[...].astype(BF16), wgu_ref[0]) + bgu_ref[0]
        g_ = jnp.minimum(gu[:, :F], SWIGLU_LIMIT)
        up = jnp.clip(gu[:, F:], -SWIGLU_LIMIT, SWIGLU_LIMIT)
        act = (up + 1.0) * g_ * _sigmoid(SWIGLU_ALPHA * g_)
        ys_o[...] = _dot(act.astype(BF16), wd_ref[0]) + bd_ref[0]

    @pl.when(j >= nu_ref[0])
    def _unused():
        ys_o[...] = jnp.zeros_like(ys_o)


def _experts(blk_e, n_used, xs, wgu, bgu, wd, bd):
    NS, D = xs.shape
    nb = NS // MOE_BLOCK
    E, _, F2 = wgu.shape
    F = F2 // 2
    grid_spec = pltpu.PrefetchScalarGridSpec(
        num_scalar_prefetch=2,
        grid=(nb,),
        in_specs=[pl.BlockSpec((MOE_BLOCK, D), lambda j, be, nu: (jnp.minimum(j, nu[0] - 1), 0)),
                  pl.BlockSpec((1, D, F2), lambda j, be, nu: (be[j], 0, 0)),
                  pl.BlockSpec((1, 1, F2), lambda j, be, nu: (be[j], 0, 0)),
                  pl.BlockSpec((1, F, D), lambda j, be, nu: (be[j], 0, 0)),
                  pl.BlockSpec((1, 1, D), lambda j, be, nu: (be[j], 0, 0))],
        out_specs=pl.BlockSpec((MOE_BLOCK, D), lambda j, be, nu: (j, 0)),
    )
    return pl.pallas_call(
        _expert_kernel,
        grid_spec=grid_spec,
        out_shape=jax.ShapeDtypeStruct((NS, D), F32),
        compiler_params=_cparams(("arbitrary",)),
        name="moe_experts",
    )(blk_e, n_used, xs, wgu, bgu, wd, bd)


def _combine_kernel(dest_ref, h1_ref, pp_ref, ps_ref, gates_ref, ys_ref, npl_ref, wpg_ref, wpp_ref, fn_ref,
                    yp_o, ysm_o, gbuf, sem, *, ntp):
    TM = h1_ref.shape[0]
    t = pl.program_id(0)

    def body(i, _):
        for kk in range(TOP_K):
            d = dest_ref[i * TOP_K + kk]
            pltpu.make_async_copy(ys_ref.at[pl.ds(d, 1)], gbuf.at[kk, pl.ds(i, 1)], sem).start()
        return 0

    lax.fori_loop(0, TM, body, 0)
    _row_copy_wait(gbuf.at[0], sem, TOP_K)
    gates = gates_ref[...]
    h2 = h1_ref[...]
    for kk in range(TOP_K):
        h2 = h2 + gbuf[kk] * gates[:, kk:kk + 1]
    u3 = _rms(h2, npl_ref[...])
    gate = _sigmoid(_dot(u3.astype(BF16), wpg_ref[...]))
    p = _pick(t, ntp, pp_ref, ps_ref)
    h3 = h2 + gate * _dot(p.astype(BF16), wpp_ref[...])
    y = _rms(h3, fn_ref[...])

    @pl.when(t < ntp)
    def _prompt_rows():
        yp_o[...] = y

    @pl.when(t >= ntp)
    def _sample_rows():
        ysm_o[...] = y


def _combine(dest_flat, h1, pp, ps, gates, ys, npl, wpg, wpp, fn):
    T, D = h1.shape
    nt = T // ROW_TILE
    ntp = pp.shape[0] // ROW_TILE
    row = lambda n: pl.BlockSpec((ROW_TILE, n), lambda i: (i, 0))
    full = lambda a: pl.BlockSpec(a.shape, lambda i: (0, 0))
    return pl.pallas_call(
        functools.partial(_combine_kernel, ntp=ntp),
        grid=(nt,),
        in_specs=[pl.BlockSpec((ROW_TILE * TOP_K,), lambda i: (i,), memory_space=pltpu.SMEM), row(D)]
        + _two_specs(ntp, pp.shape[1])
        + [row(LANES), pl.BlockSpec(memory_space=pl.ANY), full(npl), full(wpg), full(wpp), full(fn)],
        out_specs=_two_specs(ntp, D),
        out_shape=[jax.ShapeDtypeStruct((pp.shape[0], D), F32), jax.ShapeDtypeStruct((ps.shape[0], D), F32)],
        scratch_shapes=[pltpu.VMEM((TOP_K, ROW_TILE, D), F32), pltpu.SemaphoreType.DMA(())],
        compiler_params=_cparams(("arbitrary",)),
        name="moe_combine_ple",
    )(dest_flat, h1, pp, ps, gates, ys, npl, wpg, wpp, fn)


def _pad_lora_cols(x):
    W3 = 3 * B_WIDTH
    z = jnp.zeros(x.shape[:-1] + (LORA_PAD - 64,), x.dtype)
    return jnp.concatenate([x[..., :W3 + 64], z, x[..., W3 + 64:W3 + 128], z, x[..., W3 + 128:]], axis=-1)


def _unpad_lora_cols(x):
    W3 = 3 * B_WIDTH
    return jnp.concatenate([x[..., :W3 + 64], x[..., W3 + LORA_PAD:W3 + LORA_PAD + 64], x[..., W3 + 2 * LORA_PAD:]], axis=-1)


def _heads_to_pairs(S):
    n = S.shape[0]
    S = S.reshape(n, B_HEADS // 2, 2, B_HDIM, B_HDIM)
    z = jnp.zeros_like(S[:, :, 0])
    top = jnp.concatenate([S[:, :, 0], z], axis=-1)
    bot = jnp.concatenate([z, S[:, :, 1]], axis=-1)
    return jnp.concatenate([top, bot], axis=-2)


def _pairs_to_heads(Sb):
    n = Sb.shape[0]
    a = Sb[:, :, :B_HDIM, :B_HDIM]
    b = Sb[:, :, B_HDIM:, B_HDIM:]
    return jnp.stack([a, b], axis=2).reshape(n, B_HEADS, B_HDIM, B_HDIM)


def kernel(x_prompt, x_sample, state_mlstm_C, state_mlstm_n, state_mlstm_m, state_mlstm_conv, state_rwkv_S, state_rwkv_shift, p_prompt, p_sample, norm_mix, w_in, conv_w, conv_b, b_ig, b_fg, a_norm, shift_mu, w0, w_w2, a0, w_a2, w_g2, k_k, k_a, r_k, gn_w, gn_b, w_out, norm_ffn, w_router, b_router, w_gu, b_gu, w_down, b_down, norm_ple, w_ple_gate, w_ple_proj, final_norm):
    assert norm_mix.shape[0] == 1, "single-layer trunk"
    nbp, Lp, D = x_prompt.shape
    nbs, Ls, _ = x_sample.shape
    Tp, Ts = nbp * Lp, nbs * Ls
    T = Tp + Ts
    row = lambda a: a.reshape(1, -1).astype(F32)

    wi = w_in[0]
    AW2 = 2 * A_WIDTH
    wqk = wi[:, :AW2].astype(BF16)
    wv = wi[:, AW2:AW2 + A_WIDTH].astype(BF16)
    wo = wi[:, AW2 + A_WIDTH:AW2 + 2 * A_WIDTH].astype(BF16)
    a_cols = 4 * A_WIDTH + 2 * A_HEADS
    wif = jnp.pad(wi[:, 4 * A_WIDTH:a_cols], ((0, 0), (0, LANES - 2 * A_HEADS)))
    wb = _pad_lora_cols(wi[:, a_cols:]).astype(BF16)
    gate_bias = jnp.pad(jnp.concatenate([b_ig[0], b_fg[0]]), (0, LANES - 2 * A_HEADS)).reshape(1, LANES)
    padw = lambda w: jnp.pad(w, ((0, LORA_PAD - w.shape[0]), (0, 0)))
    hid = jnp.arange(B_WIDTH) // B_HDIM
    bd = (hid[:, None] == hid[None, :]).astype(BF16)
    E = w_router.shape[-1]
    wr = jnp.pad(w_router[0], ((0, 0), (0, LANES - E)))
    br = jnp.pad(b_router[0], (0, LANES - E), constant_values=NEG_BIG).reshape(1, LANES)

    assert Tp % ROW_TILE == 0 and Ts % ROW_TILE == 0
    xp, xsm = x_prompt.reshape(Tp, D), x_sample.reshape(Ts, D)
    qk, v, og, ifg, pb = _in_proj(xp, xsm, row(norm_mix), wqk, wv, wo, wif, wb)

    f0 = lambda *shp: jnp.zeros(shp, F32)
    ml_consts = (conv_w[0], row(conv_b), gate_bias, row(a_norm))
    m_s = jnp.pad(state_mlstm_m[0], ((0, 0), (0, LANES - A_HEADS))).reshape(nbs, 1, LANES)
    hap, Cp, np_, mp, convp = _mlstm(qk, v, og, ifg, f0(nbp, A_HEADS, A_HDIM, A_HDIM), f0(nbp, A_HEADS, A_HDIM),
                                     f0(nbp, 1, LANES), f0(nbp, CONV_W - 1, AW2), *ml_consts,
                                     n_seq=nbp, L=Lp, row_off=0)
    has, Cs, ns_, ms, convs = _mlstm(qk, v, og, ifg, state_mlstm_C[0], state_mlstm_n[0], m_s, state_mlstm_conv[0],
                                     *ml_consts, n_seq=nbs, L=Ls, row_off=Tp)

    rw_consts = (row(_pad_lora_cols(shift_mu[0])), row(w0), padw(w_w2[0]), row(a0), padw(w_a2[0]), w_g2[0].astype(BF16),
                 row(k_k), row(k_a), row(r_k), row(gn_w), row(gn_b), bd)
    npair = B_HEADS // 2
    hbp, Sp, shp_ = _rwkv(pb, f0(nbp, npair, PAIR, PAIR), f0(nbp, 1, B_COLS_PAD), *rw_consts,
                          n_seq=nbp, L=Lp, row_off=0)
    hbs, Ss, shs = _rwkv(pb, _heads_to_pairs(state_rwkv_S[0]), _pad_lora_cols(state_rwkv_shift[0]), *rw_consts,
                         n_seq=nbs, L=Ls, row_off=Tp)

    wout = w_out[0].astype(BF16)
    h1, u2, ids, gates, rank, counts = _out_router(hap, has, hbp, hbs, xp, xsm, wout[:A_WIDTH], wout[A_WIDTH:],
                                                   row(norm_ffn), wr, br)

    counts = counts[0, :E]
    pcounts = (counts + MOE_BLOCK - 1) // MOE_BLOCK * MOE_BLOCK
    pend = jnp.cumsum(pcounts)
    pstart = pend - pcounts
    n_blk = -(-T * TOP_K // MOE_BLOCK) + E
    dest = (jnp.take(pstart, ids[:, :TOP_K]) + rank[:, :TOP_K]).reshape(-1).astype(I32)
    blk_start = jnp.arange(n_blk, dtype=I32) * MOE_BLOCK
    blk_e = jnp.minimum(jnp.sum(pend[None, :] <= blk_start[:, None], axis=1), E - 1).astype(I32)
    n_used = (pend[E - 1:] // MOE_BLOCK).astype(I32)

    xs = _scatter_rows((pstart + counts).astype(I32), pend.astype(I32), n_used, dest, u2, n_blk * MOE_BLOCK)
    ys = _experts(blk_e, n_used, xs, w_gu[0].astype(BF16), b_gu[0][:, None, :], w_down[0].astype(BF16), b_down[0][:, None, :])
    yp, ysm = _combine(dest, h1, p_prompt[0].reshape(Tp, -1), p_sample[0].reshape(Ts, -1), gates, ys, row(norm_ple),
                       w_ple_gate[0].astype(BF16), w_ple_proj[0].astype(BF16), row(final_norm))

    lead = lambda a: a[None]
    return (yp.reshape(nbp, Lp, D), ysm.reshape(nbs, Ls, D),
            lead(Cp), lead(np_), lead(mp[:, 0, :A_HEADS]), lead(convp), lead(_pairs_to_heads(Sp)), lead(_unpad_lora_cols(shp_)),
            lead(Cs), lead(ns_), lead(ms[:, 0, :A_HEADS]), lead(convs), lead(_pairs_to_heads(Ss)), lead(_unpad_lora_cols(shs)))
```

```python
import functools

import jax
import jax.numpy as jnp
from jax import lax
from jax.experimental import pallas as pl
from jax.experimental.pallas import tpu as pltpu

F32 = jnp.float32
BF16 = jnp.bfloat16
I32 = jnp.int32

EPS = 1e-6
GN_EPS = 64e-5
A_HEADS, A_HDIM = 4, 128
B_HEADS, B_HDIM = 8, 64
A_WIDTH = A_HEADS * A_HDIM
B_WIDTH = B_HEADS * B_HDIM
CONV_W = 4
LORA_PAD = 128
N_EXPERTS, TOP_K = 32, 4
SWIGLU_LIMIT, SWIGLU_ALPHA = 7.0, 1.702
LANES = 128
ROW_TILE = 256
VMEM_LIMIT = 56 * 1024 * 1024


def _cparams(sem):
    return pltpu.CompilerParams(dimension_semantics=sem, vmem_limit_bytes=VMEM_LIMIT)


def _dot(a, b):
    return lax.dot_general(a, b, (((1,), (0,)), ((), ())), preferred_element_type=F32)


def _dot_nt(a, b):
    return lax.dot_general(a, b, (((1,), (1,)), ((), ())), preferred_element_type=F32)


def _dot_tn(a, b):
    return lax.dot_general(a, b, (((0,), (0,)), ((), ())), preferred_element_type=F32)


def _split2(x):
    hi = x.astype(BF16)
    lo = (x - hi.astype(F32)).astype(BF16)
    return hi, lo


def _split3(x):
    hi = x.astype(BF16)
    r1 = x - hi.astype(F32)
    mid = r1.astype(BF16)
    lo = (r1 - mid.astype(F32)).astype(BF16)
    return hi, mid, lo


def _dot3(a, b, f=_dot):
    ah, al = _split2(a)
    bh, bl = _split2(b)
    return f(ah, bh) + (f(ah, bl) + f(al, bh))


def _dotx(a, b_exact, f=_dot):
    a1, a2, a3 = _split3(a)
    return f(a1, b_exact) + (f(a2, b_exact) + f(a3, b_exact))


def _dotx_l(a_exact, b, f=_dot):
    b1, b2, b3 = _split3(b)
    return f(a_exact, b1) + (f(a_exact, b2) + f(a_exact, b3))


def _rms(x, g):
    return x * lax.rsqrt(jnp.mean(x * x, axis=-1, keepdims=True) + EPS) * g


def _sigmoid(x):
    return 1.0 / (1.0 + jnp.exp(-x))


def _two_specs(ntp, n):
    return [pl.BlockSpec((ROW_TILE, n), lambda i: (jnp.minimum(i, ntp - 1), 0)),
            pl.BlockSpec((ROW_TILE, n), lambda i: (jnp.maximum(i - ntp, 0), 0))]


def _pick(i, ntp, p_ref, s_ref):
    return jnp.where(i < ntp, p_ref[...], s_ref[...])


def _in_proj_kernel(xp_ref, xs_ref, g_ref, wqk_ref, wv_ref, wo_ref, wif_ref, wb_ref,
                    qk_o, v_o, og_o, if_o, pb_o, *, ntp):
    x = _pick(pl.program_id(0), ntp, xp_ref, xs_ref)
    u = _rms(x, g_ref[...])
    ub = u.astype(BF16)
    qk_o[...] = _dot(ub, wqk_ref[...])
    v_o[...] = _dot(ub, wv_ref[...])
    og_o[...] = _dot(ub, wo_ref[...])
    if_o[...] = _dot3(u, wif_ref[...])
    pb_o[...] = _dot(ub, wb_ref[...])


def _in_proj(xp, xs, g, wqk, wv, wo, wif, wb):
    D = xp.shape[1]
    T = xp.shape[0] + xs.shape[0]
    ntp = xp.shape[0] // ROW_TILE
    row = lambda n: pl.BlockSpec((ROW_TILE, n), lambda i: (i, 0))
    full = lambda a: pl.BlockSpec(a.shape, lambda i: (0, 0))
    outs = [wqk.shape[1], wv.shape[1], wo.shape[1], wif.shape[1], wb.shape[1]]
    return pl.pallas_call(
        functools.partial(_in_proj_kernel, ntp=ntp),
        grid=(T // ROW_TILE,),
        in_specs=_two_specs(ntp, D) + [full(g), full(wqk), full(wv), full(wo), full(wif), full(wb)],
        out_specs=[row(n) for n in outs],
        out_shape=[jax.ShapeDtypeStruct((T, n), F32) for n in outs],
        compiler_params=_cparams(("arbitrary",)),
        name="in_proj",
    )(xp, xs, g, wqk, wv, wo, wif, wb)


MLSTM_CHUNK = 128
NEG_BIG = -1e30


def _log_sigmoid(x):
    return jnp.minimum(x, 0.0) - jnp.log1p(jnp.exp(-jnp.abs(x)))


def _mlstm_kernel(qk_ref, v_ref, og_ref, if_ref, C0_ref, n0_ref, m0_ref, conv0_ref,
                  cw_ref, cb_ref, gb_ref, an_ref,
                  ha_o, C1_o, n1_o, m1_o, conv1_o, C_sc, n_sc, m_sc, cbuf, *, nv):
    CH = MLSTM_CHUNK
    c = pl.program_id(1)

    @pl.when(c == 0)
    def _init():
        C_sc[...] = C0_ref[0]
        n_sc[0:A_HEADS] = n0_ref[0]
        m_sc[0:1] = m0_ref[0]
        cbuf[5:8] = conv0_ref[0]

    cbuf[8:8 + nv] = qk_ref[...]
    if nv < CH:
        cbuf[8 + nv:8 + CH] = jnp.zeros((CH - nv, 2 * A_WIDTH), F32)
    y = cb_ref[...] + cbuf[5:5 + CH] * cw_ref[0:1]
    for j in range(1, CONV_W):
        y = y + cbuf[5 + j:5 + j + CH] * cw_ref[j:j + 1]
    conv_tail = cbuf[5 + nv:8 + nv]
    qk = y * _sigmoid(y)
    q = qk[:, :A_WIDTH]
    k = qk[:, A_WIDTH:] * (A_HDIM ** -0.5)
    if nv == CH:
        v = v_ref[...]
        og = og_ref[...]
        gp = if_ref[...] + gb_ref[...]
    else:
        zpad = lambda n: jnp.zeros((CH - nv, n), F32)
        v = jnp.concatenate([v_ref[...], zpad(A_WIDTH)], axis=0)
        og = jnp.concatenate([og_ref[...], zpad(A_WIDTH)], axis=0)
        gp = jnp.concatenate([if_ref[...] + gb_ref[...], zpad(LANES)], axis=0)

    row = lax.broadcasted_iota(I32, (CH, LANES), 0)
    li = gp
    lf = pltpu.roll(_log_sigmoid(gp), LANES - A_HEADS, 1)
    if nv < CH:
        li = jnp.where(row < nv, li, NEG_BIG)
        lf = jnp.where(row < nv, lf, 0.0)
    r2 = lax.broadcasted_iota(I32, (CH, CH), 0)
    c2 = lax.broadcasted_iota(I32, (CH, CH), 1)
    tri = r2 >= c2
    tril_b = jnp.where(tri, 1.0, 0.0).astype(BF16)
    b = _dotx_l(tril_b, lf)
    g = li - b
    sel = jnp.where(lax.broadcasted_iota(I32, (8, LANES), 0) == lax.broadcasted_iota(I32, (8, LANES), 1),
                    1.0, 0.0).astype(BF16)
    gT = _dotx_l(sel, g, f=_dot_nt)

    qb, kb, vb = q.astype(BF16), k.astype(BF16), v.astype(BF16)
    lane1 = lax.broadcasted_iota(I32, (1, LANES), 1)
    m_row = m_sc[0:1]
    H = range(A_HEADS)
    sls = [slice(h * A_HDIM, (h + 1) * A_HDIM) for h in H]
    bcol = [b[:, h:h + 1] for h in H]
    m_h = [m_row[:, h:h + 1] for h in H]
    C_h = [C_sc[h] for h in H]
    n_h = [n_sc[h:h + 1] for h in H]
    qk_t = [_dot_nt(qb[:, sl], kb[:, sl]) for sl in sls]
    qC = [_dot_nt(qb[:, sl], C_h[h].astype(BF16)) for h, sl in enumerate(sls)]
    dm = [jnp.where(tri, bcol[h] + gT[h:h + 1, :], -jnp.inf) for h in H]
    inter = [bcol[h] + m_h[h] for h in H]
    mt = [jnp.maximum(inter[h], jnp.max(dm[h], axis=1, keepdims=True)) for h in H]
    s = [qk_t[h] * jnp.exp(dm[h] - mt[h]) for h in H]
    wi = [jnp.exp(inter[h] - mt[h]) for h in H]
    sv = [_dot(s[h].astype(BF16), vb[:, sl]) for h, sl in enumerate(sls)]
    m_new = [mt[h][CH - 1:CH] for h in H]
    w_end = [jnp.exp(bcol[h][CH - 1:CH] + g[:, h:h + 1] - m_new[h]) for h in H]
    dec = [jnp.exp(bcol[h][CH - 1:CH] + m_h[h] - m_new[h]) for h in H]
    vk = [_dot_tn((v[:, sl] * w_end[h]).astype(BF16), kb[:, sl]) for h, sl in enumerate(sls)]
    outs = []
    m_next = m_row
    for h, sl in enumerate(sls):
        num = sv[h] + wi[h] * qC[h]
        den = jnp.sum(s[h], axis=1, keepdims=True) + wi[h] * jnp.sum(q[:, sl] * n_h[h], axis=1, keepdims=True)
        hh = num / jnp.maximum(jnp.abs(den), jnp.exp(-mt[h]))
        hn = hh * lax.rsqrt(jnp.mean(hh * hh, axis=1, keepdims=True) + EPS)
        outs.append(hn * an_ref[:, sl] * _sigmoid(og[:, sl]))
        C_sc[h] = dec[h] * C_h[h] + vk[h]
        n_sc[h:h + 1] = dec[h] * n_h[h] + jnp.sum(k[:, sl] * w_end[h], axis=0, keepdims=True)
        m_next = jnp.where(lane1 == h, m_new[h], m_next)
    m_sc[0:1] = m_next
    ha = jnp.concatenate(outs, axis=1)
    ha_o[...] = ha[:nv].astype(BF16)
    cbuf[5:8] = conv_tail

    @pl.when(c == pl.num_programs(1) - 1)
    def _fin():
        C1_o[0] = C_sc[...]
        n1_o[0] = n_sc[0:A_HEADS]
        m1_o[0] = m_sc[0:1]
        conv1_o[0] = conv_tail


def _mlstm(qk, v, og, ifg, C0, n0, m0, conv0, cw, cb, gb, an, *, n_seq, L, row_off):
    CH = MLSTM_CHUNK
    nv = min(L, CH)
    nch = L // nv
    off = row_off // nv
    tok = lambda n: pl.BlockSpec((nv, n), lambda s, c: (off + s * nch + c, 0))
    st = lambda shp: pl.BlockSpec((1,) + shp, lambda s, c: (s,) + (0,) * len(shp))
    full = lambda a: pl.BlockSpec(a.shape, lambda s, c: (0,) * a.ndim)
    in_specs = [tok(2 * A_WIDTH), tok(A_WIDTH), tok(A_WIDTH), tok(LANES),
                st((A_HEADS, A_HDIM, A_HDIM)), st((A_HEADS, A_HDIM)), st((1, LANES)),
                st((CONV_W - 1, 2 * A_WIDTH)), full(cw), full(cb), full(gb), full(an)]
    args = [qk, v, og, ifg, C0, n0, m0, conv0, cw, cb, gb, an]
    out_shape = [jax.ShapeDtypeStruct((n_seq * L, A_WIDTH), BF16),
                 jax.ShapeDtypeStruct((n_seq, A_HEADS, A_HDIM, A_HDIM), F32),
                 jax.ShapeDtypeStruct((n_seq, A_HEADS, A_HDIM), F32),
                 jax.ShapeDtypeStruct((n_seq, 1, LANES), F32),
                 jax.ShapeDtypeStruct((n_seq, CONV_W - 1, 2 * A_WIDTH), F32)]
    out_specs = [pl.BlockSpec((nv, A_WIDTH), lambda s, c: (s * nch + c, 0)),
                 st((A_HEADS, A_HDIM, A_HDIM)), st((A_HEADS, A_HDIM)),
                 st((1, LANES)), st((CONV_W - 1, 2 * A_WIDTH))]
    return pl.pallas_call(
        functools.partial(_mlstm_kernel, nv=nv),
        grid=(n_seq, nch),
        in_specs=in_specs, out_specs=out_specs, out_shape=out_shape,
        scratch_shapes=[pltpu.VMEM((A_HEADS, A_HDIM, A_HDIM), F32), pltpu.VMEM((8, A_HDIM), F32),
                        pltpu.VMEM((8, LANES), F32), pltpu.VMEM((CH + 8, 2 * A_WIDTH), F32)],
        compiler_params=_cparams(("arbitrary", "arbitrary")),
        name="mlstm",
    )(*args)


RWKV_CHUNK = 64
RWKV_NCHUNK = 2
B_COLS_PAD = 3 * B_WIDTH + 2 * LORA_PAD + 128
PAIR = 2 * B_HDIM


def _softplus(z):
    return jnp.maximum(z, 0.0) + jnp.log1p(jnp.exp(-jnp.abs(z)))


def _rwkv_kernel(pb_ref, S0_ref, sh0_ref, mu_ref, w0_ref, ww2_ref, a0_ref, wa2_ref, wg2_ref,
                 kk_ref, ka_ref, rk_ref, gw_ref, gb_ref, bd_ref,
                 hb_o, S1_o, sh1_o, S_sc, sbuf, *, nv):
    C = RWKV_CHUNK
    nc = -(-nv // C)
    TB = nc * C
    t = pl.program_id(1)

    @pl.when(t == 0)
    def _init():
        S_sc[...] = S0_ref[0]
        sbuf[7:8] = sh0_ref[0]

    sbuf[8:8 + nv] = pb_ref[...]
    if nv < TB:
        sbuf[8 + nv:8 + TB] = jnp.zeros((TB - nv, B_COLS_PAD), F32)
    pb = sbuf[8:8 + TB]
    prev = sbuf[7:7 + TB]
    last_row = sbuf[7 + nv:8 + nv]
    xm = pb + (prev - pb) * mu_ref[...]
    W = B_WIDTH
    r = xm[:, 0:W]
    kb = xm[:, W:2 * W]
    vb = xm[:, 2 * W:3 * W]
    wl = xm[:, 3 * W:3 * W + LORA_PAD]
    al = xm[:, 3 * W + LORA_PAD:3 * W + 2 * LORA_PAD]
    gl = xm[:, 3 * W + 2 * LORA_PAD:]
    w_log = -_softplus(-(w0_ref[...] + _dot3(jnp.tanh(wl), ww2_ref[...]))) - 0.5
    lw = -jnp.exp(w_log)
    a = _sigmoid(a0_ref[...] + _dot3(al, wa2_ref[...]))
    g = _dot(_sigmoid(gl).astype(BF16), wg2_ref[...])
    bd = bd_ref[...]
    kkr = kb * kk_ref[...]
    kk = kkr / jnp.maximum(jnp.sqrt(_dotx(kkr * kkr, bd)), 1e-12)
    k = kb * (1.0 + (a - 1.0) * ka_ref[...])
    bonus = _dotx(r * k * rk_ref[...], bd) * vb
    if nv < TB:
        valid = lax.broadcasted_iota(I32, (TB, W), 0) < nv
        zero = lambda z: jnp.where(valid, z, 0.0)
        lw, kk, k, vb = zero(lw), zero(kk), zero(k), zero(vb)

    lane = lax.broadcasted_iota(I32, (C, PAIR), 1)
    h0 = lane < B_HDIM
    sep = lambda z: jnp.concatenate([jnp.where(h0, z, 0.0), jnp.where(h0, 0.0, z)], axis=0)
    ri = lax.broadcasted_iota(I32, (C, C), 0)
    ci = lax.broadcasted_iota(I32, (C, C), 1)
    tril_c = jnp.where(ri >= ci, 1.0, 0.0).astype(BF16)
    r2 = lax.broadcasted_iota(I32, (PAIR, PAIR), 0)
    c2 = lax.broadcasted_iota(I32, (PAIR, PAIR), 1)
    low_s = r2 > c2
    low_i = r2 >= c2
    eye = jnp.where(r2 == c2, 1.0, 0.0)
    npair = B_HEADS // 2
    units = [(c, p) for c in range(nc) for p in range(npair)]
    cat0 = lambda x, y: jnp.concatenate([x, y], axis=0)
    cat1 = lambda x, y: jnp.concatenate([x, y], axis=1)
    bf = lambda z: z.astype(BF16)
    lhs, rhs_t, vs, wc = [], [], [], []
    for (c, p) in units:
        rs = slice(c * C, (c + 1) * C)
        ls = slice(p * PAIR, (p + 1) * PAIR)
        lwc = lw[rs, ls]
        b = _dotx_l(tril_c, lwc)
        emb = jnp.exp(-b)
        kkc = kk[rs, ls]
        lhs.append(bf(cat0(sep(kkc * jnp.exp(b - lwc)), sep(r[rs, ls] * jnp.exp(b)))))
        rhs_t.append(bf(cat0(sep(k[rs, ls] * emb), sep(kkc * a[rs, ls] * emb))))
        vs.append(bf(sep(vb[rs, ls])))
        wc.append(jnp.exp(b[C - 1:C]))
    A = [_dot_nt(l_, r_) for l_, r_ in zip(lhs, rhs_t)]
    Ak = [jnp.where(low_s, x[:PAIR, :PAIR], 0.0) for x in A]
    Ab = [jnp.where(low_s, x[:PAIR, PAIR:], 0.0) for x in A]
    Ark = [jnp.where(low_i, x[PAIR:, :PAIR], 0.0) for x in A]
    Arb = [jnp.where(low_i, x[PAIR:, PAIR:], 0.0) for x in A]
    X2 = [_dot(bf(cat0(x, y)), v_) for x, y, v_ in zip(Ak, Ark, vs)]
    VtK = [_dot_tn(v_, r_[:PAIR]) for v_, r_ in zip(vs, rhs_t)]
    Tm = [eye - x for x in Ab]
    P = [_dot(bf(x), bf(x)) for x in Ab]
    n = 2
    while 2 * n < C:
        X = [_dot(bf(cat0(t_, p_)), bf(p_)) for t_, p_ in zip(Tm, P)]
        Tm = [t_ + x[:PAIR] for t_, x in zip(Tm, X)]
        P = [x[PAIR:] for x in X]
        n *= 2
    Tm = [t_ + _dot(bf(t_), bf(p_)) for t_, p_ in zip(Tm, P)]
    Y = [_dot(bf(t_), cat1(l_[:PAIR], bf(x[:PAIR]))) for t_, l_, x in zip(Tm, lhs, X2)]
    Z = [_dot(bf(x), bf(y)) for x, y in zip(Arb, Y)]
    Rp = [bf(l_[PAIR:].astype(F32) - z[:, :PAIR]) for l_, z in zip(lhs, Z)]
    Op = [x[PAIR:] - z[:, PAIR:] for x, z in zip(X2, Z)]
    W = [_dot_tn(bf(y), r_[PAIR:]) for y, r_ in zip(Y, rhs_t)]
    Mc = [bf(w_[:PAIR] * c_) for w_, c_ in zip(W, wc)]
    Nn = [(v_ - w_[PAIR:]) * c_ for v_, w_, c_ in zip(VtK, W, wc)]
    o_rows = []
    for c in range(nc):
        S = [S_sc[p] for p in range(npair)]
        Sb = [bf(s_) for s_ in S]
        u0 = c * npair
        O = [_dot_nt(Rp[u0 + p], Sb[p]) + Op[u0 + p] for p in range(npair)]
        for p in range(npair):
            S_sc[p] = S[p] * wc[u0 + p] - _dot(Sb[p], Mc[u0 + p]) + Nn[u0 + p]
        o_rows.append(jnp.concatenate([x[:C] + x[C:] for x in O], axis=1))
    o = o_rows[0] if nc == 1 else jnp.concatenate(o_rows, axis=0)

    mean = _dotx(o, bd) * (1.0 / B_HDIM)
    d = o - mean
    var = _dotx(d * d, bd) * (1.0 / B_HDIM)
    hb = (d * lax.rsqrt(var + GN_EPS) * gw_ref[...] + gb_ref[...] + bonus) * g
    hb_o[...] = hb[:nv].astype(BF16)
    sbuf[7:8] = last_row

    @pl.when(t == pl.num_programs(1) - 1)
    def _fin():
        S1_o[0] = S_sc[...]
        sh1_o[0] = last_row


def _rwkv(pb, S0, sh0, mu, w0, ww2, a0, wa2, wg2, k_k, k_a, r_k, gn_w, gn_b, bd, *, n_seq, L, row_off):
    nv = min(L, RWKV_CHUNK * RWKV_NCHUNK)
    nst = L // nv
    off = row_off // nv
    TB = -(-nv // RWKV_CHUNK) * RWKV_CHUNK
    tok = lambda n: pl.BlockSpec((nv, n), lambda s, t: (off + s * nst + t, 0))
    st = lambda shp: pl.BlockSpec((1,) + shp, lambda s, t: (s,) + (0,) * len(shp))
    full = lambda a: pl.BlockSpec(a.shape, lambda s, t: (0,) * a.ndim)
    consts = [mu, w0, ww2, a0, wa2, wg2, k_k, k_a, r_k, gn_w, gn_b, bd]
    npair = B_HEADS // 2
    in_specs = [tok(B_COLS_PAD), st((npair, PAIR, PAIR)), st((1, B_COLS_PAD))] + [full(a) for a in consts]
    args = [pb, S0, sh0] + consts
    out_shape = [jax.ShapeDtypeStruct((n_seq * L, B_WIDTH), BF16),
                 jax.ShapeDtypeStruct((n_seq, npair, PAIR, PAIR), F32),
                 jax.ShapeDtypeStruct((n_seq, 1, B_COLS_PAD), F32)]
    out_specs = [pl.BlockSpec((nv, B_WIDTH), lambda s, t: (s * nst + t, 0)),
                 st((npair, PAIR, PAIR)), st((1, B_COLS_PAD))]
    return pl.pallas_call(
        functools.partial(_rwkv_kernel, nv=nv),
        grid=(n_seq, nst),
        in_specs=in_specs, out_specs=out_specs, out_shape=out_shape,
        scratch_shapes=[pltpu.VMEM((npair, PAIR, PAIR), F32), pltpu.VMEM((TB + 8, B_COLS_PAD), F32)],
        compiler_params=_cparams(("arbitrary", "arbitrary")),
        name="rwkv",
    )(*args)


SUB = 8


def _rows_to_tiles(ref, x):
    rows = x.shape[0]
    for j in range(SUB):
        ref[pl.ds(j, rows, stride=SUB), :] = x[:, j * LANES:(j + 1) * LANES]


def _tiles_to_rows(ref, rows):
    return jnp.concatenate([ref[pl.ds(j, rows, stride=SUB), :] for j in range(SUB)], axis=1)


def _row_tile(ref, r8):
    return ref.at[pl.ds(pl.multiple_of(r8, SUB), SUB)]


def _out_router_kernel(hap_ref, has_ref, hbp_ref, hbs_ref, xp_ref, xs_ref, woa_ref, wob_ref, nf_ref, wr_ref, br_ref,
                       h1_o, u2_o, ids_o, gates_o, rank_o, cnt_o, carry, *, ntp):
    i = pl.program_id(0)

    @pl.when(i == 0)
    def _init():
        carry[...] = jnp.zeros_like(carry)

    ha = _pick(i, ntp, hap_ref, has_ref)
    hb = _pick(i, ntp, hbp_ref, hbs_ref)
    h1 = _pick(i, ntp, xp_ref, xs_ref) + _dot(ha, woa_ref[...]) + _dot(hb, wob_ref[...])
    h1_o[...] = h1
    u2 = _rms(h1, nf_ref[...])
    _rows_to_tiles(u2_o, u2)
    logits = _dot3(u2, wr_ref[...]) + br_ref[...]
    TM = logits.shape[0]
    lane = lax.broadcasted_iota(I32, (TM, LANES), 1)
    lane_f = lane.astype(F32)
    work = logits
    vals, idxs = [], []
    for _ in range(TOP_K):
        m = jnp.max(work, axis=1, keepdims=True)
        idx = jnp.min(jnp.where(work == m, lane_f, float(LANES)), axis=1, keepdims=True)
        vals.append(m)
        idxs.append(idx)
        work = jnp.where(lane_f == idx, -jnp.inf, work)
    es = [jnp.exp(v - vals[0]) for v in vals]
    tot = es[0] + es[1] + es[2] + es[3]
    member = jnp.zeros((TM, LANES), F32)
    ids = jnp.zeros((TM, LANES), F32)
    gates = jnp.zeros((TM, LANES), F32)
    for kk in range(TOP_K):
        member = member + jnp.where(lane_f == idxs[kk], 1.0, 0.0)
        ids = jnp.where(lane == kk, idxs[kk], ids)
        gates = jnp.where(lane == kk, es[kk] / tot, gates)
    r2 = lax.broadcasted_iota(I32, (TM, TM), 0)
    c2 = lax.broadcasted_iota(I32, (TM, TM), 1)
    before = _dot(jnp.where(r2 > c2, 1.0, 0.0).astype(BF16), member.astype(BF16)) + carry[0:1]
    rank = jnp.zeros((TM, LANES), F32)
    for kk in range(TOP_K):
        rk = jnp.sum(jnp.where(lane_f == idxs[kk], before, 0.0), axis=1, keepdims=True)
        rank = jnp.where(lane == kk, rk, rank)
    ids_o[...] = ids.astype(I32)
    gates_o[...] = gates
    rank_o[...] = rank.astype(I32)
    carry[0:1] = carry[0:1] + jnp.sum(member, axis=0, keepdims=True)
    cnt_o[...] = carry[0:1].astype(I32)


def _out_router(hap, has, hbp, hbs, xp, xs, woa, wob, nf, wr, br):
    D = xp.shape[1]
    T = xp.shape[0] + xs.shape[0]
    ntp = xp.shape[0] // ROW_TILE
    nt = T // ROW_TILE
    row = lambda n: pl.BlockSpec((ROW_TILE, n), lambda i: (i, 0))
    full = lambda a: pl.BlockSpec(a.shape, lambda i: (0, 0))
    return pl.pallas_call(
        functools.partial(_out_router_kernel, ntp=ntp),
        grid=(nt,),
        in_specs=_two_specs(ntp, A_WIDTH) + _two_specs(ntp, B_WIDTH) + _two_specs(ntp, D)
        + [full(woa), full(wob), full(nf), full(wr), full(br)],
        out_specs=[row(D), pl.BlockSpec((ROW_TILE * SUB, LANES), lambda i: (i, 0)),
                   row(LANES), row(LANES), row(LANES), pl.BlockSpec((1, LANES), lambda i: (0, 0))],
        out_shape=[jax.ShapeDtypeStruct((T, D), F32), jax.ShapeDtypeStruct((T * SUB, LANES), F32),
                   jax.ShapeDtypeStruct((T, LANES), I32), jax.ShapeDtypeStruct((T, LANES), F32),
                   jax.ShapeDtypeStruct((T, LANES), I32), jax.ShapeDtypeStruct((1, LANES), I32)],
        scratch_shapes=[pltpu.VMEM((8, LANES), F32)],
        compiler_params=_cparams(("arbitrary",)),
        name="out_router",
    )(hap, has, hbp, hbs, xp, xs, woa, wob, nf, wr, br)


MOE_BLOCK = 256


def _row_copy_wait(buf, sem, n):
    for _ in range(n):
        pltpu.make_async_copy(buf, buf, sem).wait()


DMA_UNROLL = 4


def _scatter_kernel(plo_ref, phi_ref, nu_ref, dest_ref, u_ref, xs_o, zbuf, sem):
    TM = u_ref.shape[0] // SUB
    BLK = MOE_BLOCK * SUB

    @pl.when(pl.program_id(0) == 0)
    def _zero_unused_slots():
        zbuf[...] = jnp.zeros_like(zbuf)
        zrow = zbuf.at[pl.ds(0, SUB)]

        def per_expert(e, n):
            def per_row(rw, _):
                pltpu.make_async_copy(zrow, _row_tile(xs_o, rw * SUB), sem).start()
                return 0

            lax.fori_loop(plo_ref[e], phi_ref[e], per_row, 0)
            return n + (phi_ref[e] - plo_ref[e])

        n_rows = lax.fori_loop(0, plo_ref.shape[0], per_expert, 0)

        def wait_row(_, c):
            pltpu.make_async_copy(zrow, xs_o.at[pl.ds(0, SUB)], sem).wait()
            return c

        lax.fori_loop(0, n_rows, wait_row, 0)

        def per_block(j, _):
            cp = pltpu.make_async_copy(zbuf, xs_o.at[pl.ds(pl.multiple_of(j * BLK, BLK), BLK)], sem)
            cp.start()
            cp.wait()
            return 0

        lax.fori_loop(nu_ref[0], xs_o.shape[0] // BLK, per_block, 0)

    def body(i, _):
        src = _row_tile(u_ref, i * SUB)
        for kk in range(TOP_K):
            pltpu.make_async_copy(src, _row_tile(xs_o, dest_ref[i * TOP_K + kk]), sem).start(priority=kk % 2)
        return 0

    lax.fori_loop(0, TM, body, 0, unroll=DMA_UNROLL)
    _row_copy_wait(u_ref, sem, TOP_K)


def _scatter_rows(pad_lo, pad_hi, n_used, dest8_flat, u2t, n_slots):
    T = u2t.shape[0] // SUB
    grid_spec = pltpu.PrefetchScalarGridSpec(
        num_scalar_prefetch=3,
        grid=(T // ROW_TILE,),
        in_specs=[pl.BlockSpec((ROW_TILE * TOP_K,), lambda i, *_: (i,), memory_space=pltpu.SMEM),
                  pl.BlockSpec((ROW_TILE * SUB, LANES), lambda i, *_: (i, 0))],
        out_specs=pl.BlockSpec(memory_space=pl.ANY),
        scratch_shapes=[pltpu.VMEM((MOE_BLOCK * SUB, LANES), F32), pltpu.SemaphoreType.DMA(())],
    )
    return pl.pallas_call(
        _scatter_kernel,
        grid_spec=grid_spec,
        out_shape=jax.ShapeDtypeStruct((n_slots * SUB, LANES), F32),
        compiler_params=_cparams(("arbitrary",)),
        name="moe_scatter",
    )(pad_lo, pad_hi, n_used, dest8_flat, u2t)


def _expert_kernel(be_ref, nu_ref, xs_ref, wgu_ref, bgu_ref, wd_ref, bd_ref, ys_o):
    j = pl.program_id(0)
    F = wd_ref.shape[1]

    @pl.when(j < nu_ref[0])
    def _compute():
        x = _tiles_to_rows(xs_ref, MOE_BLOCK)
        gu = _dot(x.astype(BF16), wgu_ref[0]) + bgu_ref[0]
        g_ = jnp.minimum(gu[:, :F], SWIGLU_LIMIT)
        up = jnp.clip(gu[:, F:], -SWIGLU_LIMIT, SWIGLU_LIMIT)
        act = (up + 1.0) * g_ * _sigmoid(SWIGLU_ALPHA * g_)
        _rows_to_tiles(ys_o, _dot(act.astype(BF16), wd_ref[0]) + bd_ref[0])

    @pl.when(j >= nu_ref[0])
    def _unused():
        ys_o[...] = jnp.zeros_like(ys_o)


def _experts(blk_e, n_used, xs, wgu, bgu, wd, bd):
    nb = xs.shape[0] // (MOE_BLOCK * SUB)
    E, D, F2 = wgu.shape
    F = F2 // 2
    grid_spec = pltpu.PrefetchScalarGridSpec(
        num_scalar_prefetch=2,
        grid=(nb,),
        in_specs=[pl.BlockSpec((MOE_BLOCK * SUB, LANES), lambda j, be, nu: (jnp.minimum(j, nu[0] - 1), 0)),
                  pl.BlockSpec((1, D, F2), lambda j, be, nu: (be[j], 0, 0)),
                  pl.BlockSpec((1, 1, F2), lambda j, be, nu: (be[j], 0, 0)),
                  pl.BlockSpec((1, F, D), lambda j, be, nu: (be[j], 0, 0)),
                  pl.BlockSpec((1, 1, D), lambda j, be, nu: (be[j], 0, 0))],
        out_specs=pl.BlockSpec((MOE_BLOCK * SUB, LANES), lambda j, be, nu: (j, 0)),
    )
    return pl.pallas_call(
        _expert_kernel,
        grid_spec=grid_spec,
        out_shape=jax.ShapeDtypeStruct(xs.shape, F32),
        compiler_params=_cparams(("arbitrary",)),
        name="moe_experts",
    )(blk_e, n_used, xs, wgu, bgu, wd, bd)


def _combine_kernel(dest_ref, h1_ref, pp_ref, ps_ref, gates_ref, ys_ref, npl_ref, wpg_ref, wpp_ref, fn_ref,
                    yp_o, ysm_o, gbuf, sem, *, ntp):
    TM = h1_ref.shape[0]
    t = pl.program_id(0)

    def body(i, _):
        for kk in range(TOP_K):
            src = _row_tile(ys_ref, dest_ref[i * TOP_K + kk])
            pltpu.make_async_copy(src, _row_tile(gbuf.at[kk], i * SUB), sem).start(priority=kk % 2)
        return 0

    lax.fori_loop(0, TM, body, 0, unroll=DMA_UNROLL)
    _row_copy_wait(gbuf.at[0], sem, TOP_K)
    gates = gates_ref[...]
    h2 = h1_ref[...]
    for kk in range(TOP_K):
        h2 = h2 + _tiles_to_rows(gbuf.at[kk], TM) * gates[:, kk:kk + 1]
    u3 = _rms(h2, npl_ref[...])
    gate = _sigmoid(_dot(u3.astype(BF16), wpg_ref[...]))
    p = _pick(t, ntp, pp_ref, ps_ref)
    h3 = h2 + gate * _dot(p.astype(BF16), wpp_ref[...])
    y = _rms(h3, fn_ref[...])

    @pl.when(t < ntp)
    def _prompt_rows():
        yp_o[...] = y

    @pl.when(t >= ntp)
    def _sample_rows():
        ysm_o[...] = y


def _combine(dest_flat, h1, pp, ps, gates, ys, npl, wpg, wpp, fn):
    T, D = h1.shape
    nt = T // ROW_TILE
    ntp = pp.shape[0] // ROW_TILE
    row = lambda n: pl.BlockSpec((ROW_TILE, n), lambda i: (i, 0))
    full = lambda a: pl.BlockSpec(a.shape, lambda i: (0, 0))
    return pl.pallas_call(
        functools.partial(_combine_kernel, ntp=ntp),
        grid=(nt,),
        in_specs=[pl.BlockSpec((ROW_TILE * TOP_K,), lambda i: (i,), memory_space=pltpu.SMEM), row(D)]
        + _two_specs(ntp, pp.shape[1])
        + [row(LANES), pl.BlockSpec(memory_space=pl.ANY), full(npl), full(wpg), full(wpp), full(fn)],
        out_specs=_two_specs(ntp, D),
        out_shape=[jax.ShapeDtypeStruct((pp.shape[0], D), F32), jax.ShapeDtypeStruct((ps.shape[0], D), F32)],
        scratch_shapes=[pltpu.VMEM((TOP_K, ROW_TILE * SUB, LANES), F32), pltpu.SemaphoreType.DMA(())],
        compiler_params=_cparams(("arbitrary",)),
        name="moe_combine_ple",
    )(dest_flat, h1, pp, ps, gates, ys, npl, wpg, wpp, fn)


def _pad_lora_cols(x):
    W3 = 3 * B_WIDTH
    z = jnp.zeros(x.shape[:-1] + (LORA_PAD - 64,), x.dtype)
    return jnp.concatenate([x[..., :W3 + 64], z, x[..., W3 + 64:W3 + 128], z, x[..., W3 + 128:]], axis=-1)


def _unpad_lora_cols(x):
    W3 = 3 * B_WIDTH
    return jnp.concatenate([x[..., :W3 + 64], x[..., W3 + LORA_PAD:W3 + LORA_PAD + 64], x[..., W3 + 2 * LORA_PAD:]], axis=-1)


def _heads_to_pairs(S):
    n = S.shape[0]
    S = S.reshape(n, B_HEADS // 2, 2, B_HDIM, B_HDIM)
    z = jnp.zeros_like(S[:, :, 0])
    top = jnp.concatenate([S[:, :, 0], z], axis=-1)
    bot = jnp.concatenate([z, S[:, :, 1]], axis=-1)
    return jnp.concatenate([top, bot], axis=-2)


def _pairs_to_heads(Sb):
    n = Sb.shape[0]
    a = Sb[:, :, :B_HDIM, :B_HDIM]
    b = Sb[:, :, B_HDIM:, B_HDIM:]
    return jnp.stack([a, b], axis=2).reshape(n, B_HEADS, B_HDIM, B_HDIM)


def kernel(x_prompt, x_sample, state_mlstm_C, state_mlstm_n, state_mlstm_m, state_mlstm_conv, state_rwkv_S, state_rwkv_shift, p_prompt, p_sample, norm_mix, w_in, conv_w, conv_b, b_ig, b_fg, a_norm, shift_mu, w0, w_w2, a0, w_a2, w_g2, k_k, k_a, r_k, gn_w, gn_b, w_out, norm_ffn, w_router, b_router, w_gu, b_gu, w_down, b_down, norm_ple, w_ple_gate, w_ple_proj, final_norm):
    assert norm_mix.shape[0] == 1, "single-layer trunk"
    nbp, Lp, D = x_prompt.shape
    nbs, Ls, _ = x_sample.shape
    Tp, Ts = nbp * Lp, nbs * Ls
    T = Tp + Ts
    row = lambda a: a.reshape(1, -1).astype(F32)

    wi = w_in[0]
    AW2 = 2 * A_WIDTH
    wqk = wi[:, :AW2].astype(BF16)
    wv = wi[:, AW2:AW2 + A_WIDTH].astype(BF16)
    wo = wi[:, AW2 + A_WIDTH:AW2 + 2 * A_WIDTH].astype(BF16)
    a_cols = 4 * A_WIDTH + 2 * A_HEADS
    wif = jnp.pad(wi[:, 4 * A_WIDTH:a_cols], ((0, 0), (0, LANES - 2 * A_HEADS)))
    wb = _pad_lora_cols(wi[:, a_cols:]).astype(BF16)
    gate_bias = jnp.pad(jnp.concatenate([b_ig[0], b_fg[0]]), (0, LANES - 2 * A_HEADS)).reshape(1, LANES)
    padw = lambda w: jnp.pad(w, ((0, LORA_PAD - w.shape[0]), (0, 0)))
    hid = jnp.arange(B_WIDTH) // B_HDIM
    bd = (hid[:, None] == hid[None, :]).astype(BF16)
    E = w_router.shape[-1]
    wr = jnp.pad(w_router[0], ((0, 0), (0, LANES - E)))
    br = jnp.pad(b_router[0], (0, LANES - E), constant_values=NEG_BIG).reshape(1, LANES)

    assert Tp % ROW_TILE == 0 and Ts % ROW_TILE == 0
    xp, xsm = x_prompt.reshape(Tp, D), x_sample.reshape(Ts, D)
    qk, v, og, ifg, pb = _in_proj(xp, xsm, row(norm_mix), wqk, wv, wo, wif, wb)

    f0 = lambda *shp: jnp.zeros(shp, F32)
    ml_consts = (conv_w[0], row(conv_b), gate_bias, row(a_norm))
    m_s = jnp.pad(state_mlstm_m[0], ((0, 0), (0, LANES - A_HEADS))).reshape(nbs, 1, LANES)
    hap, Cp, np_, mp, convp = _mlstm(qk, v, og, ifg, f0(nbp, A_HEADS, A_HDIM, A_HDIM), f0(nbp, A_HEADS, A_HDIM),
                                     f0(nbp, 1, LANES), f0(nbp, CONV_W - 1, AW2), *ml_consts,
                                     n_seq=nbp, L=Lp, row_off=0)
    has, Cs, ns_, ms, convs = _mlstm(qk, v, og, ifg, state_mlstm_C[0], state_mlstm_n[0], m_s, state_mlstm_conv[0],
                                     *ml_consts, n_seq=nbs, L=Ls, row_off=Tp)

    rw_consts = (row(_pad_lora_cols(shift_mu[0])), row(w0), padw(w_w2[0]), row(a0), padw(w_a2[0]), w_g2[0].astype(BF16),
                 row(k_k), row(k_a), row(r_k), row(gn_w), row(gn_b), bd)
    npair = B_HEADS // 2
    hbp, Sp, shp_ = _rwkv(pb, f0(nbp, npair, PAIR, PAIR), f0(nbp, 1, B_COLS_PAD), *rw_consts,
                          n_seq=nbp, L=Lp, row_off=0)
    hbs, Ss, shs = _rwkv(pb, _heads_to_pairs(state_rwkv_S[0]), _pad_lora_cols(state_rwkv_shift[0]), *rw_consts,
                         n_seq=nbs, L=Ls, row_off=Tp)

    wout = w_out[0].astype(BF16)
    h1, u2, ids, gates, rank, counts = _out_router(hap, has, hbp, hbs, xp, xsm, wout[:A_WIDTH], wout[A_WIDTH:],
                                                   row(norm_ffn), wr, br)

    counts = counts[0, :E]
    pcounts = (counts + MOE_BLOCK - 1) // MOE_BLOCK * MOE_BLOCK
    pend = jnp.cumsum(pcounts)
    pstart = pend - pcounts
    n_blk = -(-T * TOP_K // MOE_BLOCK) + E
    dest = ((jnp.take(pstart, ids[:, :TOP_K]) + rank[:, :TOP_K]) * SUB).reshape(-1).astype(I32)
    blk_start = jnp.arange(n_blk, dtype=I32) * MOE_BLOCK
    blk_e = jnp.minimum(jnp.sum(pend[None, :] <= blk_start[:, None], axis=1), E - 1).astype(I32)
    n_used = (pend[E - 1:] // MOE_BLOCK).astype(I32)

    xs = _scatter_rows((pstart + counts).astype(I32), pend.astype(I32), n_used, dest, u2, n_blk * MOE_BLOCK)
    ys = _experts(blk_e, n_used, xs, w_gu[0].astype(BF16), b_gu[0][:, None, :], w_down[0].astype(BF16), b_down[0][:, None, :])
    yp, ysm = _combine(dest, h1, p_prompt[0].reshape(Tp, -1), p_sample[0].reshape(Ts, -1), gates, ys, row(norm_ple),
                       w_ple_gate[0].astype(BF16), w_ple_proj[0].astype(BF16), row(final_norm))

    lead = lambda a: a[None]
    return (yp.reshape(nbp, Lp, D), ysm.reshape(nbs, Ls, D),
            lead(Cp), lead(np_), lead(mp[:, 0, :A_HEADS]), lead(convp), lead(_pairs_to_heads(Sp)), lead(_unpad_lora_cols(shp_)),
            lead(Cs), lead(ns_), lead(ms[:, 0, :A_HEADS]), lead(convs), lead(_pairs_to_heads(Ss)), lead(_unpad_lora_cols(shs)))
```

```python
import functools

import jax
import jax.numpy as jnp
from jax import lax
from jax.experimental import pallas as pl
from jax.experimental.pallas import tpu as pltpu

F32 = jnp.float32
BF16 = jnp.bfloat16
I32 = jnp.int32

EPS = 1e-6
GN_EPS = 64e-5
A_HEADS, A_HDIM = 4, 128
B_HEADS, B_HDIM = 8, 64
A_WIDTH = A_HEADS * A_HDIM
B_WIDTH = B_HEADS * B_HDIM
CONV_W = 4
LORA_PAD = 128
N_EXPERTS, TOP_K = 32, 4
SWIGLU_LIMIT, SWIGLU_ALPHA = 7.0, 1.702
LANES = 128
ROW_TILE = 256
VMEM_LIMIT = 56 * 1024 * 1024


def _cparams(sem):
    return pltpu.CompilerParams(dimension_semantics=sem, vmem_limit_bytes=VMEM_LIMIT)


def _dot(a, b):
    return lax.dot_general(a, b, (((1,), (0,)), ((), ())), preferred_element_type=F32)


def _dot_nt(a, b):
    return lax.dot_general(a, b, (((1,), (1,)), ((), ())), preferred_element_type=F32)


def _dot_tn(a, b):
    return lax.dot_general(a, b, (((0,), (0,)), ((), ())), preferred_element_type=F32)


def _split2(x):
    hi = x.astype(BF16)
    lo = (x - hi.astype(F32)).astype(BF16)
    return hi, lo


def _split3(x):
    hi = x.astype(BF16)
    r1 = x - hi.astype(F32)
    mid = r1.astype(BF16)
    lo = (r1 - mid.astype(F32)).astype(BF16)
    return hi, mid, lo


def _dot3(a, b, f=_dot):
    ah, al = _split2(a)
    bh, bl = _split2(b)
    return f(ah, bh) + (f(ah, bl) + f(al, bh))


def _dotx2(a, b_exact, f=_dot):
    a1, a2 = _split2(a)
    return f(a1, b_exact) + f(a2, b_exact)


def _dotx_l(a_exact, b, f=_dot):
    b1, b2, b3 = _split3(b)
    return f(a_exact, b1) + (f(a_exact, b2) + f(a_exact, b3))


def _rms(x, g):
    return x * lax.rsqrt(jnp.mean(x * x, axis=-1, keepdims=True) + EPS) * g


def _sigmoid(x):
    return 1.0 / (1.0 + jnp.exp(-x))


def _two_specs(ntp, n):
    return [pl.BlockSpec((ROW_TILE, n), lambda i: (jnp.minimum(i, ntp - 1), 0)),
            pl.BlockSpec((ROW_TILE, n), lambda i: (jnp.maximum(i - ntp, 0), 0))]


def _pick(i, ntp, p_ref, s_ref):
    return jnp.where(i < ntp, p_ref[...], s_ref[...])


def _in_proj_kernel(xp_ref, xs_ref, g_ref, wqk_ref, wv_ref, wo_ref, wif_ref, wb_ref,
                    qk_o, v_o, og_o, if_o, pb_o, *, ntp):
    x = _pick(pl.program_id(0), ntp, xp_ref, xs_ref)
    u = _rms(x, g_ref[...])
    ub = u.astype(BF16)
    qk_o[...] = _dot(ub, wqk_ref[...])
    v_o[...] = _dot(ub, wv_ref[...])
    og_o[...] = _dot(ub, wo_ref[...])
    if_o[...] = _dot3(u, wif_ref[...])
    pb_o[...] = _dot(ub, wb_ref[...])


def _in_proj(xp, xs, g, wqk, wv, wo, wif, wb):
    D = xp.shape[1]
    T = xp.shape[0] + xs.shape[0]
    ntp = xp.shape[0] // ROW_TILE
    row = lambda n: pl.BlockSpec((ROW_TILE, n), lambda i: (i, 0))
    full = lambda a: pl.BlockSpec(a.shape, lambda i: (0, 0))
    outs = [wqk.shape[1], wv.shape[1], wo.shape[1], wif.shape[1], wb.shape[1]]
    return pl.pallas_call(
        functools.partial(_in_proj_kernel, ntp=ntp),
        grid=(T // ROW_TILE,),
        in_specs=_two_specs(ntp, D) + [full(g), full(wqk), full(wv), full(wo), full(wif), full(wb)],
        out_specs=[row(n) for n in outs],
        out_shape=[jax.ShapeDtypeStruct((T, n), F32) for n in outs],
        compiler_params=_cparams(("arbitrary",)),
        name="in_proj",
    )(xp, xs, g, wqk, wv, wo, wif, wb)


MLSTM_CHUNK = 128
NEG_BIG = -1e30


def _log_sigmoid(x):
    return jnp.minimum(x, 0.0) - jnp.log1p(jnp.exp(-jnp.abs(x)))


def _mlstm_kernel(qk_ref, v_ref, og_ref, if_ref, C0_ref, n0_ref, m0_ref, conv0_ref,
                  cw_ref, cb_ref, gb_ref, an_ref,
                  ha_o, C1_o, n1_o, m1_o, conv1_o, C_sc, n_sc, m_sc, cbuf, *, nv):
    CH = MLSTM_CHUNK
    c = pl.program_id(1)

    @pl.when(c == 0)
    def _init():
        C_sc[...] = C0_ref[0]
        n_sc[0:A_HEADS] = n0_ref[0]
        m_sc[0:1] = m0_ref[0]
        cbuf[5:8] = conv0_ref[0]

    cbuf[8:8 + nv] = qk_ref[...]
    if nv < CH:
        cbuf[8 + nv:8 + CH] = jnp.zeros((CH - nv, 2 * A_WIDTH), F32)
    y = cb_ref[...] + cbuf[5:5 + CH] * cw_ref[0:1]
    for j in range(1, CONV_W):
        y = y + cbuf[5 + j:5 + j + CH] * cw_ref[j:j + 1]
    conv_tail = cbuf[5 + nv:8 + nv]
    qk = y * _sigmoid(y)
    q = qk[:, :A_WIDTH]
    k = qk[:, A_WIDTH:] * (A_HDIM ** -0.5)
    if nv == CH:
        v = v_ref[...]
        og = og_ref[...]
        gp = if_ref[...] + gb_ref[...]
    else:
        zpad = lambda n: jnp.zeros((CH - nv, n), F32)
        v = jnp.concatenate([v_ref[...], zpad(A_WIDTH)], axis=0)
        og = jnp.concatenate([og_ref[...], zpad(A_WIDTH)], axis=0)
        gp = jnp.concatenate([if_ref[...] + gb_ref[...], zpad(LANES)], axis=0)

    row = lax.broadcasted_iota(I32, (CH, LANES), 0)
    li = gp
    lf = pltpu.roll(_log_sigmoid(gp), LANES - A_HEADS, 1)
    if nv < CH:
        li = jnp.where(row < nv, li, NEG_BIG)
        lf = jnp.where(row < nv, lf, 0.0)
    r2 = lax.broadcasted_iota(I32, (CH, CH), 0)
    c2 = lax.broadcasted_iota(I32, (CH, CH), 1)
    tri = r2 >= c2
    tril_b = jnp.where(tri, 1.0, 0.0).astype(BF16)
    b = _dotx_l(tril_b, lf)
    g = li - b
    sel = jnp.where(lax.broadcasted_iota(I32, (8, LANES), 0) == lax.broadcasted_iota(I32, (8, LANES), 1),
                    1.0, 0.0).astype(BF16)
    gT = _dotx_l(sel, g, f=_dot_nt)

    qb, kb, vb = q.astype(BF16), k.astype(BF16), v.astype(BF16)
    lane1 = lax.broadcasted_iota(I32, (1, LANES), 1)
    m_row = m_sc[0:1]
    H = range(A_HEADS)
    sls = [slice(h * A_HDIM, (h + 1) * A_HDIM) for h in H]
    bcol = [b[:, h:h + 1] for h in H]
    m_h = [m_row[:, h:h + 1] for h in H]
    C_h = [C_sc[h] for h in H]
    n_h = [n_sc[h:h + 1] for h in H]
    qk_t = [_dot_nt(qb[:, sl], kb[:, sl]) for sl in sls]
    qC = [_dot_nt(qb[:, sl], C_h[h].astype(BF16)) for h, sl in enumerate(sls)]
    dm = [jnp.where(tri, bcol[h] + gT[h:h + 1, :], -jnp.inf) for h in H]
    inter = [bcol[h] + m_h[h] for h in H]
    mt = [jnp.maximum(inter[h], jnp.max(dm[h], axis=1, keepdims=True)) for h in H]
    s = [qk_t[h] * jnp.exp(dm[h] - mt[h]) for h in H]
    wi = [jnp.exp(inter[h] - mt[h]) for h in H]
    sv = [_dot(s[h].astype(BF16), vb[:, sl]) for h, sl in enumerate(sls)]
    m_new = [mt[h][CH - 1:CH] for h in H]
    w_end = [jnp.exp(bcol[h][CH - 1:CH] + g[:, h:h + 1] - m_new[h]) for h in H]
    dec = [jnp.exp(bcol[h][CH - 1:CH] + m_h[h] - m_new[h]) for h in H]
    vk = [_dot_tn((v[:, sl] * w_end[h]).astype(BF16), kb[:, sl]) for h, sl in enumerate(sls)]
    outs = []
    m_next = m_row
    for h, sl in enumerate(sls):
        num = sv[h] + wi[h] * qC[h]
        den = jnp.sum(s[h], axis=1, keepdims=True) + wi[h] * jnp.sum(q[:, sl] * n_h[h], axis=1, keepdims=True)
        hh = num / jnp.maximum(jnp.abs(den), jnp.exp(-mt[h]))
        hn = hh * lax.rsqrt(jnp.mean(hh * hh, axis=1, keepdims=True) + EPS)
        outs.append(hn * an_ref[:, sl] * _sigmoid(og[:, sl]))
        C_sc[h] = dec[h] * C_h[h] + vk[h]
        n_sc[h:h + 1] = dec[h] * n_h[h] + jnp.sum(k[:, sl] * w_end[h], axis=0, keepdims=True)
        m_next = jnp.where(lane1 == h, m_new[h], m_next)
    m_sc[0:1] = m_next
    ha = jnp.concatenate(outs, axis=1)
    ha_o[...] = ha[:nv].astype(BF16)
    cbuf[5:8] = conv_tail

    @pl.when(c == pl.num_programs(1) - 1)
    def _fin():
        C1_o[0] = C_sc[...]
        n1_o[0] = n_sc[0:A_HEADS]
        m1_o[0] = m_sc[0:1]
        conv1_o[0] = conv_tail


def _mlstm(qk, v, og, ifg, C0, n0, m0, conv0, cw, cb, gb, an, *, n_seq, L, row_off):
    CH = MLSTM_CHUNK
    nv = min(L, CH)
    nch = L // nv
    off = row_off // nv
    tok = lambda n: pl.BlockSpec((nv, n), lambda s, c: (off + s * nch + c, 0))
    st = lambda shp: pl.BlockSpec((1,) + shp, lambda s, c: (s,) + (0,) * len(shp))
    full = lambda a: pl.BlockSpec(a.shape, lambda s, c: (0,) * a.ndim)
    in_specs = [tok(2 * A_WIDTH), tok(A_WIDTH), tok(A_WIDTH), tok(LANES),
                st((A_HEADS, A_HDIM, A_HDIM)), st((A_HEADS, A_HDIM)), st((1, LANES)),
                st((CONV_W - 1, 2 * A_WIDTH)), full(cw), full(cb), full(gb), full(an)]
    args = [qk, v, og, ifg, C0, n0, m0, conv0, cw, cb, gb, an]
    out_shape = [jax.ShapeDtypeStruct((n_seq * L, A_WIDTH), BF16),
                 jax.ShapeDtypeStruct((n_seq, A_HEADS, A_HDIM, A_HDIM), F32),
                 jax.ShapeDtypeStruct((n_seq, A_HEADS, A_HDIM), F32),
                 jax.ShapeDtypeStruct((n_seq, 1, LANES), F32),
                 jax.ShapeDtypeStruct((n_seq, CONV_W - 1, 2 * A_WIDTH), F32)]
    out_specs = [pl.BlockSpec((nv, A_WIDTH), lambda s, c: (s * nch + c, 0)),
                 st((A_HEADS, A_HDIM, A_HDIM)), st((A_HEADS, A_HDIM)),
                 st((1, LANES)), st((CONV_W - 1, 2 * A_WIDTH))]
    return pl.pallas_call(
        functools.partial(_mlstm_kernel, nv=nv),
        grid=(n_seq, nch),
        in_specs=in_specs, out_specs=out_specs, out_shape=out_shape,
        scratch_shapes=[pltpu.VMEM((A_HEADS, A_HDIM, A_HDIM), F32), pltpu.VMEM((8, A_HDIM), F32),
                        pltpu.VMEM((8, LANES), F32), pltpu.VMEM((CH + 8, 2 * A_WIDTH), F32)],
        compiler_params=_cparams(("arbitrary", "arbitrary")),
        name="mlstm",
    )(*args)


RWKV_CHUNK = 64
RWKV_NCHUNK = 2
B_COLS_PAD = 3 * B_WIDTH + 2 * LORA_PAD + 128
PAIR = 2 * B_HDIM


def _softplus(z):
    return jnp.maximum(z, 0.0) + jnp.log1p(jnp.exp(-jnp.abs(z)))


def _rwkv_kernel(pb_ref, S0_ref, sh0_ref, mu_ref, w0_ref, ww2_ref, a0_ref, wa2_ref, wg2_ref,
                 kk_ref, ka_ref, rk_ref, gw_ref, gb_ref, bd_ref,
                 hb_o, S1_o, sh1_o, S_sc, sbuf, *, nv):
    C = RWKV_CHUNK
    nc = -(-nv // C)
    TB = nc * C
    t = pl.program_id(1)

    @pl.when(t == 0)
    def _init():
        S_sc[...] = S0_ref[0]
        sbuf[7:8] = sh0_ref[0]

    sbuf[8:8 + nv] = pb_ref[...]
    if nv < TB:
        sbuf[8 + nv:8 + TB] = jnp.zeros((TB - nv, B_COLS_PAD), F32)
    pb = sbuf[8:8 + TB]
    prev = sbuf[7:7 + TB]
    last_row = sbuf[7 + nv:8 + nv]
    xm = pb + (prev - pb) * mu_ref[...]
    W = B_WIDTH
    r = xm[:, 0:W]
    kb = xm[:, W:2 * W]
    vb = xm[:, 2 * W:3 * W]
    wl = xm[:, 3 * W:3 * W + LORA_PAD]
    al = xm[:, 3 * W + LORA_PAD:3 * W + 2 * LORA_PAD]
    gl = xm[:, 3 * W + 2 * LORA_PAD:]
    w_log = -_softplus(-(w0_ref[...] + _dot3(jnp.tanh(wl), ww2_ref[...]))) - 0.5
    lw = -jnp.exp(w_log)
    a = _sigmoid(a0_ref[...] + _dot3(al, wa2_ref[...]))
    g = _dot(_sigmoid(gl).astype(BF16), wg2_ref[...])
    bd = bd_ref[...]
    kkr = kb * kk_ref[...]
    kk = kkr / jnp.maximum(jnp.sqrt(_dotx2(kkr * kkr, bd)), 1e-12)
    k = kb * (1.0 + (a - 1.0) * ka_ref[...])
    bonus = _dotx2(r * k * rk_ref[...], bd) * vb
    if nv < TB:
        valid = lax.broadcasted_iota(I32, (TB, W), 0) < nv
        zero = lambda z: jnp.where(valid, z, 0.0)
        lw, kk, k, vb = zero(lw), zero(kk), zero(k), zero(vb)

    lane = lax.broadcasted_iota(I32, (C, PAIR), 1)
    h0 = lane < B_HDIM
    sep = lambda z: jnp.concatenate([jnp.where(h0, z, 0.0), jnp.where(h0, 0.0, z)], axis=0)
    ri = lax.broadcasted_iota(I32, (C, C), 0)
    ci = lax.broadcasted_iota(I32, (C, C), 1)
    tril_c = jnp.where(ri >= ci, 1.0, 0.0).astype(BF16)
    r2 = lax.broadcasted_iota(I32, (PAIR, PAIR), 0)
    c2 = lax.broadcasted_iota(I32, (PAIR, PAIR), 1)
    low_s = r2 > c2
    low_i = r2 >= c2
    eye = jnp.where(r2 == c2, 1.0, 0.0)
    npair = B_HEADS // 2
    units = [(c, p) for c in range(nc) for p in range(npair)]
    cat0 = lambda x, y: jnp.concatenate([x, y], axis=0)
    cat1 = lambda x, y: jnp.concatenate([x, y], axis=1)
    bf = lambda z: z.astype(BF16)
    lhs, rhs_t, vs, wc = [], [], [], []
    for (c, p) in units:
        rs = slice(c * C, (c + 1) * C)
        ls = slice(p * PAIR, (p + 1) * PAIR)
        lwc = lw[rs, ls]
        b = _dotx_l(tril_c, lwc)
        emb = jnp.exp(-b)
        kkc = kk[rs, ls]
        lhs.append(bf(cat0(sep(kkc * jnp.exp(b - lwc)), sep(r[rs, ls] * jnp.exp(b)))))
        rhs_t.append(bf(cat0(sep(k[rs, ls] * emb), sep(kkc * a[rs, ls] * emb))))
        vs.append(bf(sep(vb[rs, ls])))
        wc.append(jnp.exp(b[C - 1:C]))
    A = [_dot_nt(l_, r_) for l_, r_ in zip(lhs, rhs_t)]
    Ak = [jnp.where(low_s, x[:PAIR, :PAIR], 0.0) for x in A]
    Ab = [jnp.where(low_s, x[:PAIR, PAIR:], 0.0) for x in A]
    Ark = [jnp.where(low_i, x[PAIR:, :PAIR], 0.0) for x in A]
    Arb = [jnp.where(low_i, x[PAIR:, PAIR:], 0.0) for x in A]
    X2 = [_dot(bf(cat0(x, y)), v_) for x, y, v_ in zip(Ak, Ark, vs)]
    VtK = [_dot_tn(v_, r_[:PAIR]) for v_, r_ in zip(vs, rhs_t)]
    Tm = [eye - x for x in Ab]
    P = [_dot(bf(x), bf(x)) for x in Ab]
    n = 2
    while 2 * n < C:
        X = [_dot(bf(cat0(t_, p_)), bf(p_)) for t_, p_ in zip(Tm, P)]
        Tm = [t_ + x[:PAIR] for t_, x in zip(Tm, X)]
        P = [x[PAIR:] for x in X]
        n *= 2
    Tm = [t_ + _dot(bf(t_), bf(p_)) for t_, p_ in zip(Tm, P)]
    Y = [_dot(bf(t_), cat1(l_[:PAIR], bf(x[:PAIR]))) for t_, l_, x in zip(Tm, lhs, X2)]
    Z = [_dot(bf(x), bf(y)) for x, y in zip(Arb, Y)]
    Rp = [bf(l_[PAIR:].astype(F32) - z[:, :PAIR]) for l_, z in zip(lhs, Z)]
    Op = [x[PAIR:] - z[:, PAIR:] for x, z in zip(X2, Z)]
    W = [_dot_tn(bf(y), r_[PAIR:]) for y, r_ in zip(Y, rhs_t)]
    Mc = [bf(w_[:PAIR] * c_) for w_, c_ in zip(W, wc)]
    Nn = [(v_ - w_[PAIR:]) * c_ for v_, w_, c_ in zip(VtK, W, wc)]
    o_rows = []
    for c in range(nc):
        S = [S_sc[p] for p in range(npair)]
        Sb = [bf(s_) for s_ in S]
        u0 = c * npair
        O = [_dot_nt(Rp[u0 + p], Sb[p]) + Op[u0 + p] for p in range(npair)]
        for p in range(npair):
            S_sc[p] = S[p] * wc[u0 + p] - _dot(Sb[p], Mc[u0 + p]) + Nn[u0 + p]
        o_rows.append(jnp.concatenate([x[:C] + x[C:] for x in O], axis=1))
    o = o_rows[0] if nc == 1 else jnp.concatenate(o_rows, axis=0)

    mean = _dotx2(o, bd) * (1.0 / B_HDIM)
    d = o - mean
    var = _dotx2(d * d, bd) * (1.0 / B_HDIM)
    hb = (d * lax.rsqrt(var + GN_EPS) * gw_ref[...] + gb_ref[...] + bonus) * g
    hb_o[...] = hb[:nv].astype(BF16)
    sbuf[7:8] = last_row

    @pl.when(t == pl.num_programs(1) - 1)
    def _fin():
        S1_o[0] = S_sc[...]
        sh1_o[0] = last_row


def _rwkv(pb, S0, sh0, mu, w0, ww2, a0, wa2, wg2, k_k, k_a, r_k, gn_w, gn_b, bd, *, n_seq, L, row_off):
    nv = min(L, RWKV_CHUNK * RWKV_NCHUNK)
    nst = L // nv
    off = row_off // nv
    TB = -(-nv // RWKV_CHUNK) * RWKV_CHUNK
    tok = lambda n: pl.BlockSpec((nv, n), lambda s, t: (off + s * nst + t, 0))
    st = lambda shp: pl.BlockSpec((1,) + shp, lambda s, t: (s,) + (0,) * len(shp))
    full = lambda a: pl.BlockSpec(a.shape, lambda s, t: (0,) * a.ndim)
    consts = [mu, w0, ww2, a0, wa2, wg2, k_k, k_a, r_k, gn_w, gn_b, bd]
    npair = B_HEADS // 2
    in_specs = [tok(B_COLS_PAD), st((npair, PAIR, PAIR)), st((1, B_COLS_PAD))] + [full(a) for a in consts]
    args = [pb, S0, sh0] + consts
    out_shape = [jax.ShapeDtypeStruct((n_seq * L, B_WIDTH), BF16),
                 jax.ShapeDtypeStruct((n_seq, npair, PAIR, PAIR), F32),
                 jax.ShapeDtypeStruct((n_seq, 1, B_COLS_PAD), F32)]
    out_specs = [pl.BlockSpec((nv, B_WIDTH), lambda s, t: (s * nst + t, 0)),
                 st((npair, PAIR, PAIR)), st((1, B_COLS_PAD))]
    return pl.pallas_call(
        functools.partial(_rwkv_kernel, nv=nv),
        grid=(n_seq, nst),
        in_specs=in_specs, out_specs=out_specs, out_shape=out_shape,
        scratch_shapes=[pltpu.VMEM((npair, PAIR, PAIR), F32), pltpu.VMEM((TB + 8, B_COLS_PAD), F32)],
        compiler_params=_cparams(("arbitrary", "arbitrary")),
        name="rwkv",
    )(*args)


SUB = 8


def _rows_to_tiles(ref, x):
    rows = x.shape[0]
    for j in range(SUB):
        ref[pl.ds(j, rows, stride=SUB), :] = x[:, j * LANES:(j + 1) * LANES]


def _tiles_to_rows(ref, rows):
    return jnp.concatenate([ref[pl.ds(j, rows, stride=SUB), :] for j in range(SUB)], axis=1)


def _row_tile(ref, r8):
    return ref.at[pl.ds(pl.multiple_of(r8, SUB), SUB)]


def _out_router_kernel(hap_ref, has_ref, hbp_ref, hbs_ref, xp_ref, xs_ref, woa_ref, wob_ref, nf_ref, wr_ref, br_ref,
                       h1_o, u2_o, ids_o, gates_o, rank_o, cnt_o, carry, *, ntp):
    i = pl.program_id(0)

    @pl.when(i == 0)
    def _init():
        carry[...] = jnp.zeros_like(carry)

    ha = _pick(i, ntp, hap_ref, has_ref)
    hb = _pick(i, ntp, hbp_ref, hbs_ref)
    h1 = _pick(i, ntp, xp_ref, xs_ref) + _dot(ha, woa_ref[...]) + _dot(hb, wob_ref[...])
    h1_o[...] = h1
    u2 = _rms(h1, nf_ref[...])
    _rows_to_tiles(u2_o, u2)
    logits = _dot3(u2, wr_ref[...]) + br_ref[...]
    TM = logits.shape[0]
    lane = lax.broadcasted_iota(I32, (TM, LANES), 1)
    lane_f = lane.astype(F32)
    work = logits
    vals, idxs = [], []
    for _ in range(TOP_K):
        m = jnp.max(work, axis=1, keepdims=True)
        idx = jnp.min(jnp.where(work == m, lane_f, float(LANES)), axis=1, keepdims=True)
        vals.append(m)
        idxs.append(idx)
        work = jnp.where(lane_f == idx, -jnp.inf, work)
    es = [jnp.exp(v - vals[0]) for v in vals]
    tot = es[0] + es[1] + es[2] + es[3]
    member = jnp.zeros((TM, LANES), F32)
    ids = jnp.zeros((TM, LANES), F32)
    gates = jnp.zeros((TM, LANES), F32)
    for kk in range(TOP_K):
        member = member + jnp.where(lane_f == idxs[kk], 1.0, 0.0)
        ids = jnp.where(lane == kk, idxs[kk], ids)
        gates = jnp.where(lane == kk, es[kk] / tot, gates)
    r2 = lax.broadcasted_iota(I32, (TM, TM), 0)
    c2 = lax.broadcasted_iota(I32, (TM, TM), 1)
    before = _dot(jnp.where(r2 > c2, 1.0, 0.0).astype(BF16), member.astype(BF16)) + carry[0:1]
    rank = jnp.zeros((TM, LANES), F32)
    for kk in range(TOP_K):
        rk = jnp.sum(jnp.where(lane_f == idxs[kk], before, 0.0), axis=1, keepdims=True)
        rank = jnp.where(lane == kk, rk, rank)
    ids_o[...] = ids.astype(I32)
    gates_o[...] = gates
    rank_o[...] = rank.astype(I32)
    carry[0:1] = carry[0:1] + jnp.sum(member, axis=0, keepdims=True)
    cnt_o[...] = carry[0:1].astype(I32)


def _out_router(hap, has, hbp, hbs, xp, xs, woa, wob, nf, wr, br):
    D = xp.shape[1]
    T = xp.shape[0] + xs.shape[0]
    ntp = xp.shape[0] // ROW_TILE
    nt = T // ROW_TILE
    row = lambda n: pl.BlockSpec((ROW_TILE, n), lambda i: (i, 0))
    full = lambda a: pl.BlockSpec(a.shape, lambda i: (0, 0))
    return pl.pallas_call(
        functools.partial(_out_router_kernel, ntp=ntp),
        grid=(nt,),
        in_specs=_two_specs(ntp, A_WIDTH) + _two_specs(ntp, B_WIDTH) + _two_specs(ntp, D)
        + [full(woa), full(wob), full(nf), full(wr), full(br)],
        out_specs=[row(D), pl.BlockSpec((ROW_TILE * SUB, LANES), lambda i: (i, 0)),
                   row(LANES), row(LANES), row(LANES), pl.BlockSpec((1, LANES), lambda i: (0, 0))],
        out_shape=[jax.ShapeDtypeStruct((T, D), F32), jax.ShapeDtypeStruct((T * SUB, LANES), F32),
                   jax.ShapeDtypeStruct((T, LANES), I32), jax.ShapeDtypeStruct((T, LANES), F32),
                   jax.ShapeDtypeStruct((T, LANES), I32), jax.ShapeDtypeStruct((1, LANES), I32)],
        scratch_shapes=[pltpu.VMEM((8, LANES), F32)],
        compiler_params=_cparams(("arbitrary",)),
        name="out_router",
    )(hap, has, hbp, hbs, xp, xs, woa, wob, nf, wr, br)


MOE_BLOCK = 256


def _row_copy_wait(buf, sem, n):
    for _ in range(n):
        pltpu.make_async_copy(buf, buf, sem).wait()


DMA_UNROLL = 4


def _scatter_kernel(plo_ref, phi_ref, nu_ref, dest_ref, u_ref, xs_o, zbuf, sem):
    TM = u_ref.shape[0] // SUB
    BLK = MOE_BLOCK * SUB

    @pl.when(pl.program_id(0) == 0)
    def _zero_unused_slots():
        zbuf[...] = jnp.zeros_like(zbuf)
        zrow = zbuf.at[pl.ds(0, SUB)]

        def per_expert(e, n):
            def per_row(rw, _):
                pltpu.make_async_copy(zrow, _row_tile(xs_o, rw * SUB), sem).start()
                return 0

            lax.fori_loop(plo_ref[e], phi_ref[e], per_row, 0)
            return n + (phi_ref[e] - plo_ref[e])

        n_rows = lax.fori_loop(0, plo_ref.shape[0], per_expert, 0)

        def wait_row(_, c):
            pltpu.make_async_copy(zrow, xs_o.at[pl.ds(0, SUB)], sem).wait()
            return c

        lax.fori_loop(0, n_rows, wait_row, 0)

        def per_block(j, _):
            cp = pltpu.make_async_copy(zbuf, xs_o.at[pl.ds(pl.multiple_of(j * BLK, BLK), BLK)], sem)
            cp.start()
            cp.wait()
            return 0

        lax.fori_loop(nu_ref[0], xs_o.shape[0] // BLK, per_block, 0)

    def body(i, _):
        src = _row_tile(u_ref, i * SUB)
        for kk in range(TOP_K):
            pltpu.make_async_copy(src, _row_tile(xs_o, dest_ref[i * TOP_K + kk]), sem).start(priority=kk % 2)
        return 0

    lax.fori_loop(0, TM, body, 0, unroll=DMA_UNROLL)
    _row_copy_wait(u_ref, sem, TOP_K)


def _scatter_rows(pad_lo, pad_hi, n_used, dest8_flat, u2t, n_slots):
    T = u2t.shape[0] // SUB
    grid_spec = pltpu.PrefetchScalarGridSpec(
        num_scalar_prefetch=3,
        grid=(T // ROW_TILE,),
        in_specs=[pl.BlockSpec((ROW_TILE * TOP_K,), lambda i, *_: (i,), memory_space=pltpu.SMEM),
                  pl.BlockSpec((ROW_TILE * SUB, LANES), lambda i, *_: (i, 0))],
        out_specs=pl.BlockSpec(memory_space=pl.ANY),
        scratch_shapes=[pltpu.VMEM((MOE_BLOCK * SUB, LANES), F32), pltpu.SemaphoreType.DMA(())],
    )
    return pl.pallas_call(
        _scatter_kernel,
        grid_spec=grid_spec,
        out_shape=jax.ShapeDtypeStruct((n_slots * SUB, LANES), F32),
        compiler_params=_cparams(("arbitrary",)),
        name="moe_scatter",
    )(pad_lo, pad_hi, n_used, dest8_flat, u2t)


def _expert_kernel(be_ref, nu_ref, xs_ref, wgu_ref, bgu_ref, wd_ref, bd_ref, ys_o, wgu_b, wd_b):
    j = pl.program_id(0)
    F = wd_ref.shape[1]
    used = j < nu_ref[0]

    @pl.when(used & ((j == 0) | (be_ref[j] != be_ref[jnp.maximum(j - 1, 0)])))
    def _new_expert():
        wgu_b[...] = wgu_ref[0].astype(BF16)
        wd_b[...] = wd_ref[0].astype(BF16)

    @pl.when(used)
    def _compute():
        x = _tiles_to_rows(xs_ref, MOE_BLOCK)
        gu = _dot(x.astype(BF16), wgu_b[...]) + bgu_ref[0]
        g_ = jnp.minimum(gu[:, :F], SWIGLU_LIMIT)
        up = jnp.clip(gu[:, F:], -SWIGLU_LIMIT, SWIGLU_LIMIT)
        act = (up + 1.0) * g_ * _sigmoid(SWIGLU_ALPHA * g_)
        _rows_to_tiles(ys_o, _dot(act.astype(BF16), wd_b[...]) + bd_ref[0])

    @pl.when(j >= nu_ref[0])
    def _unused():
        ys_o[...] = jnp.zeros_like(ys_o)


def _experts(blk_e, n_used, xs, wgu, bgu, wd, bd):
    nb = xs.shape[0] // (MOE_BLOCK * SUB)
    E, D, F2 = wgu.shape
    F = F2 // 2
    grid_spec = pltpu.PrefetchScalarGridSpec(
        num_scalar_prefetch=2,
        grid=(nb,),
        in_specs=[pl.BlockSpec((MOE_BLOCK * SUB, LANES), lambda j, be, nu: (jnp.minimum(j, nu[0] - 1), 0)),
                  pl.BlockSpec((1, D, F2), lambda j, be, nu: (be[j], 0, 0)),
                  pl.BlockSpec((1, 1, F2), lambda j, be, nu: (be[j], 0, 0)),
                  pl.BlockSpec((1, F, D), lambda j, be, nu: (be[j], 0, 0)),
                  pl.BlockSpec((1, 1, D), lambda j, be, nu: (be[j], 0, 0))],
        out_specs=pl.BlockSpec((MOE_BLOCK * SUB, LANES), lambda j, be, nu: (j, 0)),
        scratch_shapes=[pltpu.VMEM((D, F2), BF16), pltpu.VMEM((F, D), BF16)],
    )
    return pl.pallas_call(
        _expert_kernel,
        grid_spec=grid_spec,
        out_shape=jax.ShapeDtypeStruct(xs.shape, F32),
        compiler_params=_cparams(("arbitrary",)),
        name="moe_experts",
    )(blk_e, n_used, xs, wgu, bgu, wd, bd)


def _combine_kernel(dest0_ref, destn_ref, h1_ref, pp_ref, ps_ref, gates_ref, ys_ref, npl_ref, wpg_ref, wpp_ref, fn_ref,
                    yp_o, ysm_o, gbuf, ybuf, sems, *, ntp, n_tiles):
    TM = h1_ref.shape[0]
    t = pl.program_id(0)

    def fetch(dref, i, slot):
        for kk in range(TOP_K):
            src = _row_tile(ys_ref, dref[i * TOP_K + kk])
            pltpu.make_async_copy(src, _row_tile(gbuf.at[slot, kk], i * SUB), sems.at[slot]).start(priority=kk % 2)

    @pl.when(t == 0)
    def _first_fetch():
        def body(i, _):
            fetch(dest0_ref, i, 0)
            return 0

        lax.fori_loop(0, TM, body, 0, unroll=DMA_UNROLL)

    def tile(slot):
        _row_copy_wait(gbuf.at[slot, 0], sems.at[slot], TOP_K)
        for i in range(TM):
            fetch(destn_ref, i, 1 - slot)
        gates = gates_ref[...]
        h2 = h1_ref[...]
        for kk in range(TOP_K):
            h2 = h2 + _tiles_to_rows(gbuf.at[slot, kk], TM) * gates[:, kk:kk + 1]
        u3 = _rms(h2, npl_ref[...])
        gate = _sigmoid(_dot(u3.astype(BF16), wpg_ref[...]))
        p = _pick(t, ntp, pp_ref, ps_ref)
        h3 = h2 + gate * _dot(p.astype(BF16), wpp_ref[...])
        return _rms(h3, fn_ref[...])

    for slot in range(2):
        @pl.when(t % 2 == slot)
        def _tile():
            ybuf[...] = tile(slot)

    @pl.when(t < ntp)
    def _prompt_rows():
        yp_o[...] = ybuf[...]

    @pl.when(t >= ntp)
    def _sample_rows():
        ysm_o[...] = ybuf[...]

    @pl.when(t == n_tiles - 1)
    def _drain():
        other = n_tiles % 2
        _row_copy_wait(gbuf.at[other, 0], sems.at[other], TOP_K)


def _combine(dest_flat, h1, pp, ps, gates, ys, npl, wpg, wpp, fn):
    T, D = h1.shape
    nt = T // ROW_TILE
    ntp = pp.shape[0] // ROW_TILE
    row = lambda n: pl.BlockSpec((ROW_TILE, n), lambda i: (i, 0))
    full = lambda a: pl.BlockSpec(a.shape, lambda i: (0, 0))
    idx = lambda f: pl.BlockSpec((ROW_TILE * TOP_K,), f, memory_space=pltpu.SMEM)
    return pl.pallas_call(
        functools.partial(_combine_kernel, ntp=ntp, n_tiles=nt),
        grid=(nt,),
        in_specs=[idx(lambda i: (0,)), idx(lambda i: (jnp.minimum(i + 1, nt - 1),)), row(D)]
        + _two_specs(ntp, pp.shape[1])
        + [row(LANES), pl.BlockSpec(memory_space=pl.ANY), full(npl), full(wpg), full(wpp), full(fn)],
        out_specs=_two_specs(ntp, D),
        out_shape=[jax.ShapeDtypeStruct((pp.shape[0], D), F32), jax.ShapeDtypeStruct((ps.shape[0], D), F32)],
        scratch_shapes=[pltpu.VMEM((2, TOP_K, ROW_TILE * SUB, LANES), F32), pltpu.VMEM((ROW_TILE, D), F32),
                        pltpu.SemaphoreType.DMA((2,))],
        compiler_params=_cparams(("arbitrary",)),
        name="moe_combine_ple",
    )(dest_flat, dest_flat, h1, pp, ps, gates, ys, npl, wpg, wpp, fn)


def _pad_lora_cols(x):
    W3 = 3 * B_WIDTH
    z = jnp.zeros(x.shape[:-1] + (LORA_PAD - 64,), x.dtype)
    return jnp.concatenate([x[..., :W3 + 64], z, x[..., W3 + 64:W3 + 128], z, x[..., W3 + 128:]], axis=-1)


def _unpad_lora_cols(x):
    W3 = 3 * B_WIDTH
    return jnp.concatenate([x[..., :W3 + 64], x[..., W3 + LORA_PAD:W3 + LORA_PAD + 64], x[..., W3 + 2 * LORA_PAD:]], axis=-1)


def _heads_to_pairs(S):
    n = S.shape[0]
    S = S.reshape(n, B_HEADS // 2, 2, B_HDIM, B_HDIM)
    z = jnp.zeros_like(S[:, :, 0])
    top = jnp.concatenate([S[:, :, 0], z], axis=-1)
    bot = jnp.concatenate([z, S[:, :, 1]], axis=-1)
    return jnp.concatenate([top, bot], axis=-2)


def _pairs_to_heads(Sb):
    n = Sb.shape[0]
    a = Sb[:, :, :B_HDIM, :B_HDIM]
    b = Sb[:, :, B_HDIM:, B_HDIM:]
    return jnp.stack([a, b], axis=2).reshape(n, B_HEADS, B_HDIM, B_HDIM)


def kernel(x_prompt, x_sample, state_mlstm_C, state_mlstm_n, state_mlstm_m, state_mlstm_conv, state_rwkv_S, state_rwkv_shift, p_prompt, p_sample, norm_mix, w_in, conv_w, conv_b, b_ig, b_fg, a_norm, shift_mu, w0, w_w2, a0, w_a2, w_g2, k_k, k_a, r_k, gn_w, gn_b, w_out, norm_ffn, w_router, b_router, w_gu, b_gu, w_down, b_down, norm_ple, w_ple_gate, w_ple_proj, final_norm):
    assert norm_mix.shape[0] == 1, "single-layer trunk"
    nbp, Lp, D = x_prompt.shape
    nbs, Ls, _ = x_sample.shape
    Tp, Ts = nbp * Lp, nbs * Ls
    T = Tp + Ts
    row = lambda a: a.reshape(1, -1).astype(F32)

    wi = w_in[0]
    AW2 = 2 * A_WIDTH
    wqk = wi[:, :AW2].astype(BF16)
    wv = wi[:, AW2:AW2 + A_WIDTH].astype(BF16)
    wo = wi[:, AW2 + A_WIDTH:AW2 + 2 * A_WIDTH].astype(BF16)
    a_cols = 4 * A_WIDTH + 2 * A_HEADS
    wif = jnp.pad(wi[:, 4 * A_WIDTH:a_cols], ((0, 0), (0, LANES - 2 * A_HEADS)))
    wb = _pad_lora_cols(wi[:, a_cols:]).astype(BF16)
    gate_bias = jnp.pad(jnp.concatenate([b_ig[0], b_fg[0]]), (0, LANES - 2 * A_HEADS)).reshape(1, LANES)
    padw = lambda w: jnp.pad(w, ((0, LORA_PAD - w.shape[0]), (0, 0)))
    hid = jnp.arange(B_WIDTH) // B_HDIM
    bd = (hid[:, None] == hid[None, :]).astype(BF16)
    E = w_router.shape[-1]
    wr = jnp.pad(w_router[0], ((0, 0), (0, LANES - E)))
    br = jnp.pad(b_router[0], (0, LANES - E), constant_values=NEG_BIG).reshape(1, LANES)

    assert Tp % ROW_TILE == 0 and Ts % ROW_TILE == 0
    xp, xsm = x_prompt.reshape(Tp, D), x_sample.reshape(Ts, D)
    qk, v, og, ifg, pb = _in_proj(xp, xsm, row(norm_mix), wqk, wv, wo, wif, wb)

    f0 = lambda *shp: jnp.zeros(shp, F32)
    ml_consts = (conv_w[0], row(conv_b), gate_bias, row(a_norm))
    m_s = jnp.pad(state_mlstm_m[0], ((0, 0), (0, LANES - A_HEADS))).reshape(nbs, 1, LANES)
    hap, Cp, np_, mp, convp = _mlstm(qk, v, og, ifg, f0(nbp, A_HEADS, A_HDIM, A_HDIM), f0(nbp, A_HEADS, A_HDIM),
                                     f0(nbp, 1, LANES), f0(nbp, CONV_W - 1, AW2), *ml_consts,
                                     n_seq=nbp, L=Lp, row_off=0)
    has, Cs, ns_, ms, convs = _mlstm(qk, v, og, ifg, state_mlstm_C[0], state_mlstm_n[0], m_s, state_mlstm_conv[0],
                                     *ml_consts, n_seq=nbs, L=Ls, row_off=Tp)

    rw_consts = (row(_pad_lora_cols(shift_mu[0])), row(w0), padw(w_w2[0]), row(a0), padw(w_a2[0]), w_g2[0].astype(BF16),
                 row(k_k), row(k_a), row(r_k), row(gn_w), row(gn_b), bd)
    npair = B_HEADS // 2
    hbp, Sp, shp_ = _rwkv(pb, f0(nbp, npair, PAIR, PAIR), f0(nbp, 1, B_COLS_PAD), *rw_consts,
                          n_seq=nbp, L=Lp, row_off=0)
    hbs, Ss, shs = _rwkv(pb, _heads_to_pairs(state_rwkv_S[0]), _pad_lora_cols(state_rwkv_shift[0]), *rw_consts,
                         n_seq=nbs, L=Ls, row_off=Tp)

    wout = w_out[0].astype(BF16)
    h1, u2, ids, gates, rank, counts = _out_router(hap, has, hbp, hbs, xp, xsm, wout[:A_WIDTH], wout[A_WIDTH:],
                                                   row(norm_ffn), wr, br)

    counts = counts[0, :E]
    pcounts = (counts + MOE_BLOCK - 1) // MOE_BLOCK * MOE_BLOCK
    pend = jnp.cumsum(pcounts)
    pstart = pend - pcounts
    n_blk = -(-T * TOP_K // MOE_BLOCK) + E
    dest = ((jnp.take(pstart, ids[:, :TOP_K]) + rank[:, :TOP_K]) * SUB).reshape(-1).astype(I32)
    blk_start = jnp.arange(n_blk, dtype=I32) * MOE_BLOCK
    blk_e = jnp.minimum(jnp.sum(pend[None, :] <= blk_start[:, None], axis=1), E - 1).astype(I32)
    n_used = (pend[E - 1:] // MOE_BLOCK).astype(I32)

    xs = _scatter_rows((pstart + counts).astype(I32), pend.astype(I32), n_used, dest, u2, n_blk * MOE_BLOCK)
    ys = _experts(blk_e, n_used, xs, w_gu[0], b_gu[0][:, None, :], w_down[0], b_down[0][:, None, :])
    yp, ysm = _combine(dest, h1, p_prompt[0].reshape(Tp, -1), p_sample[0].reshape(Ts, -1), gates, ys, row(norm_ple),
                       w_ple_gate[0].astype(BF16), w_ple_proj[0].astype(BF16), row(final_norm))

    lead = lambda a: a[None]
    return (yp.reshape(nbp, Lp, D), ysm.reshape(nbs, Ls, D),
            lead(Cp), lead(np_), lead(mp[:, 0, :A_HEADS]), lead(convp), lead(_pairs_to_heads(Sp)), lead(_unpad_lora_cols(shp_)),
            lead(Cs), lead(ns_), lead(ms[:, 0, :A_HEADS]), lead(convs), lead(_pairs_to_heads(Ss)), lead(_unpad_lora_cols(shs)))
```

```python
import functools

import jax
import jax.numpy as jnp
from jax import lax
from jax.experimental import pallas as pl
from jax.experimental.pallas import tpu as pltpu

F32 = jnp.float32
BF16 = jnp.bfloat16
I32 = jnp.int32

EPS = 1e-6
GN_EPS = 64e-5
A_HEADS, A_HDIM = 4, 128
B_HEADS, B_HDIM = 8, 64
A_WIDTH = A_HEADS * A_HDIM
B_WIDTH = B_HEADS * B_HDIM
CONV_W = 4
LORA_PAD = 128
N_EXPERTS, TOP_K = 32, 4
SWIGLU_LIMIT, SWIGLU_ALPHA = 7.0, 1.702
LANES = 128
ROW_TILE = 256
VMEM_LIMIT = 56 * 1024 * 1024


def _cparams(sem):
    return pltpu.CompilerParams(dimension_semantics=sem, vmem_limit_bytes=VMEM_LIMIT)


def _dot(a, b):
    return lax.dot_general(a, b, (((1,), (0,)), ((), ())), preferred_element_type=F32)


def _dot_nt(a, b):
    return lax.dot_general(a, b, (((1,), (1,)), ((), ())), preferred_element_type=F32)


def _dot_tn(a, b):
    return lax.dot_general(a, b, (((0,), (0,)), ((), ())), preferred_element_type=F32)


def _split2(x):
    hi = x.astype(BF16)
    lo = (x - hi.astype(F32)).astype(BF16)
    return hi, lo


def _split3(x):
    hi = x.astype(BF16)
    r1 = x - hi.astype(F32)
    mid = r1.astype(BF16)
    lo = (r1 - mid.astype(F32)).astype(BF16)
    return hi, mid, lo


def _dot3(a, b, f=_dot):
    ah, al = _split2(a)
    bh, bl = _split2(b)
    return f(ah, bh) + (f(ah, bl) + f(al, bh))


def _dotx2(a, b_exact, f=_dot):
    a1, a2 = _split2(a)
    return f(a1, b_exact) + f(a2, b_exact)


def _dotx_l(a_exact, b, f=_dot):
    b1, b2, b3 = _split3(b)
    return f(a_exact, b1) + (f(a_exact, b2) + f(a_exact, b3))


def _rms(x, g):
    return x * lax.rsqrt(jnp.mean(x * x, axis=-1, keepdims=True) + EPS) * g


def _sigmoid(x):
    return 1.0 / (1.0 + jnp.exp(-x))


def _two_specs(ntp, n):
    return [pl.BlockSpec((ROW_TILE, n), lambda i: (jnp.minimum(i, ntp - 1), 0)),
            pl.BlockSpec((ROW_TILE, n), lambda i: (jnp.maximum(i - ntp, 0), 0))]


def _pick(i, ntp, p_ref, s_ref):
    return jnp.where(i < ntp, p_ref[...], s_ref[...])


def _in_proj_kernel(xp_ref, xs_ref, g_ref, wqk_ref, wv_ref, wo_ref, wbg_ref,
                    qk_o, v_o, og_o, pbg_o, *, ntp):
    x = _pick(pl.program_id(0), ntp, xp_ref, xs_ref)
    ub = _rms(x, g_ref[...]).astype(BF16)
    qk_o[...] = _dot(ub, wqk_ref[...])
    v_o[...] = _dot(ub, wv_ref[...])
    og_o[...] = _dot(ub, wo_ref[...])
    pbg_o[...] = _dot(ub, wbg_ref[...])


def _in_proj(xp, xs, g, wqk, wv, wo, wbg):
    D = xp.shape[1]
    T = xp.shape[0] + xs.shape[0]
    ntp = xp.shape[0] // ROW_TILE
    row = lambda n: pl.BlockSpec((ROW_TILE, n), lambda i: (i, 0))
    full = lambda a: pl.BlockSpec(a.shape, lambda i: (0, 0))
    outs = [wqk.shape[1], wv.shape[1], wo.shape[1], wbg.shape[1]]
    return pl.pallas_call(
        functools.partial(_in_proj_kernel, ntp=ntp),
        grid=(T // ROW_TILE,),
        in_specs=_two_specs(ntp, D) + [full(g), full(wqk), full(wv), full(wo), full(wbg)],
        out_specs=[row(n) for n in outs],
        out_shape=[jax.ShapeDtypeStruct((T, n), F32) for n in outs],
        compiler_params=_cparams(("arbitrary",)),
        name="in_proj",
    )(xp, xs, g, wqk, wv, wo, wbg)


MLSTM_CHUNK = 128
NEG_BIG = -1e30


def _log_sigmoid(x):
    return jnp.minimum(x, 0.0) - jnp.log1p(jnp.exp(-jnp.abs(x)))


def _mlstm_kernel(qk_ref, v_ref, og_ref, if_ref, C0_ref, n0_ref, m0_ref, conv0_ref,
                  cw_ref, cb_ref, gb_ref, an_ref,
                  ha_o, C1_o, n1_o, m1_o, conv1_o, C_sc, n_sc, m_sc, cbuf, *, nv):
    CH = MLSTM_CHUNK
    c = pl.program_id(1)

    @pl.when(c == 0)
    def _init():
        C_sc[...] = C0_ref[0]
        n_sc[0:A_HEADS] = n0_ref[0]
        m_sc[0:1] = m0_ref[0]
        cbuf[5:8] = conv0_ref[0]

    cbuf[8:8 + nv] = qk_ref[...]
    if nv < CH:
        cbuf[8 + nv:8 + CH] = jnp.zeros((CH - nv, 2 * A_WIDTH), F32)
    y = cb_ref[...] + cbuf[5:5 + CH] * cw_ref[0:1]
    for j in range(1, CONV_W):
        y = y + cbuf[5 + j:5 + j + CH] * cw_ref[j:j + 1]
    conv_tail = cbuf[5 + nv:8 + nv]
    qk = y * _sigmoid(y)
    q = qk[:, :A_WIDTH]
    k = qk[:, A_WIDTH:] * (A_HDIM ** -0.5)
    if nv == CH:
        v = v_ref[...]
        og = og_ref[...]
        gp = if_ref[...] + gb_ref[...]
    else:
        zpad = lambda n: jnp.zeros((CH - nv, n), F32)
        v = jnp.concatenate([v_ref[...], zpad(A_WIDTH)], axis=0)
        og = jnp.concatenate([og_ref[...], zpad(A_WIDTH)], axis=0)
        gp = jnp.concatenate([if_ref[...] + gb_ref[...], zpad(LANES)], axis=0)

    row = lax.broadcasted_iota(I32, (CH, LANES), 0)
    li = gp
    lf = pltpu.roll(_log_sigmoid(gp), LANES - A_HEADS, 1)
    if nv < CH:
        li = jnp.where(row < nv, li, NEG_BIG)
        lf = jnp.where(row < nv, lf, 0.0)
    r2 = lax.broadcasted_iota(I32, (CH, CH), 0)
    c2 = lax.broadcasted_iota(I32, (CH, CH), 1)
    tri = r2 >= c2
    tril_b = jnp.where(tri, 1.0, 0.0).astype(BF16)
    b = _dotx_l(tril_b, lf)
    g = li - b
    sel = jnp.where(lax.broadcasted_iota(I32, (8, LANES), 0) == lax.broadcasted_iota(I32, (8, LANES), 1),
                    1.0, 0.0).astype(BF16)
    gT = _dotx_l(sel, g, f=_dot_nt)

    qb, kb, vb = q.astype(BF16), k.astype(BF16), v.astype(BF16)
    lane1 = lax.broadcasted_iota(I32, (1, LANES), 1)
    m_row = m_sc[0:1]
    H = range(A_HEADS)
    sls = [slice(h * A_HDIM, (h + 1) * A_HDIM) for h in H]
    bcol = [b[:, h:h + 1] for h in H]
    m_h = [m_row[:, h:h + 1] for h in H]
    C_h = [C_sc[h] for h in H]
    n_h = [n_sc[h:h + 1] for h in H]
    qk_t = [_dot_nt(qb[:, sl], kb[:, sl]) for sl in sls]
    qC = [_dot_nt(qb[:, sl], C_h[h].astype(BF16)) for h, sl in enumerate(sls)]
    dm = [jnp.where(tri, bcol[h] + gT[h:h + 1, :], -jnp.inf) for h in H]
    inter = [bcol[h] + m_h[h] for h in H]
    mt = [jnp.maximum(inter[h], jnp.max(dm[h], axis=1, keepdims=True)) for h in H]
    s = [qk_t[h] * jnp.exp(dm[h] - mt[h]) for h in H]
    wi = [jnp.exp(inter[h] - mt[h]) for h in H]
    sv = [_dot(s[h].astype(BF16), vb[:, sl]) for h, sl in enumerate(sls)]
    m_new = [mt[h][CH - 1:CH] for h in H]
    w_end = [jnp.exp(bcol[h][CH - 1:CH] + g[:, h:h + 1] - m_new[h]) for h in H]
    dec = [jnp.exp(bcol[h][CH - 1:CH] + m_h[h] - m_new[h]) for h in H]
    vk = [_dot_tn((v[:, sl] * w_end[h]).astype(BF16), kb[:, sl]) for h, sl in enumerate(sls)]
    outs = []
    m_next = m_row
    for h, sl in enumerate(sls):
        num = sv[h] + wi[h] * qC[h]
        den = jnp.sum(s[h], axis=1, keepdims=True) + wi[h] * jnp.sum(q[:, sl] * n_h[h], axis=1, keepdims=True)
        hh = num / jnp.maximum(jnp.abs(den), jnp.exp(-mt[h]))
        hn = hh * lax.rsqrt(jnp.mean(hh * hh, axis=1, keepdims=True) + EPS)
        outs.append(hn * an_ref[:, sl] * _sigmoid(og[:, sl]))
        C_sc[h] = dec[h] * C_h[h] + vk[h]
        n_sc[h:h + 1] = dec[h] * n_h[h] + jnp.sum(k[:, sl] * w_end[h], axis=0, keepdims=True)
        m_next = jnp.where(lane1 == h, m_new[h], m_next)
    m_sc[0:1] = m_next
    ha = jnp.concatenate(outs, axis=1)
    ha_o[...] = ha[:nv].astype(BF16)
    cbuf[5:8] = conv_tail

    @pl.when(c == pl.num_programs(1) - 1)
    def _fin():
        C1_o[0] = C_sc[...]
        n1_o[0] = n_sc[0:A_HEADS]
        m1_o[0] = m_sc[0:1]
        conv1_o[0] = conv_tail


def _mlstm(qk, v, og, pbg, C0, n0, m0, conv0, cw, cb, gb, an, *, n_seq, L, row_off):
    CH = MLSTM_CHUNK
    nv = min(L, CH)
    nch = L // nv
    off = row_off // nv
    gate_tile = pbg.shape[1] // LANES - 1
    tok = lambda n: pl.BlockSpec((nv, n), lambda s, c: (off + s * nch + c, 0))
    st = lambda shp: pl.BlockSpec((1,) + shp, lambda s, c: (s,) + (0,) * len(shp))
    full = lambda a: pl.BlockSpec(a.shape, lambda s, c: (0,) * a.ndim)
    in_specs = [tok(2 * A_WIDTH), tok(A_WIDTH), tok(A_WIDTH),
                pl.BlockSpec((nv, LANES), lambda s, c: (off + s * nch + c, gate_tile)),
                st((A_HEADS, A_HDIM, A_HDIM)), st((A_HEADS, A_HDIM)), st((1, LANES)),
                st((CONV_W - 1, 2 * A_WIDTH)), full(cw), full(cb), full(gb), full(an)]
    args = [qk, v, og, pbg, C0, n0, m0, conv0, cw, cb, gb, an]
    out_shape = [jax.ShapeDtypeStruct((n_seq * L, A_WIDTH), BF16),
                 jax.ShapeDtypeStruct((n_seq, A_HEADS, A_HDIM, A_HDIM), F32),
                 jax.ShapeDtypeStruct((n_seq, A_HEADS, A_HDIM), F32),
                 jax.ShapeDtypeStruct((n_seq, 1, LANES), F32),
                 jax.ShapeDtypeStruct((n_seq, CONV_W - 1, 2 * A_WIDTH), F32)]
    out_specs = [pl.BlockSpec((nv, A_WIDTH), lambda s, c: (s * nch + c, 0)),
                 st((A_HEADS, A_HDIM, A_HDIM)), st((A_HEADS, A_HDIM)),
                 st((1, LANES)), st((CONV_W - 1, 2 * A_WIDTH))]
    return pl.pallas_call(
        functools.partial(_mlstm_kernel, nv=nv),
        grid=(n_seq, nch),
        in_specs=in_specs, out_specs=out_specs, out_shape=out_shape,
        scratch_shapes=[pltpu.VMEM((A_HEADS, A_HDIM, A_HDIM), F32), pltpu.VMEM((8, A_HDIM), F32),
                        pltpu.VMEM((8, LANES), F32), pltpu.VMEM((CH + 8, 2 * A_WIDTH), F32)],
        compiler_params=_cparams(("arbitrary", "arbitrary")),
        name="mlstm",
    )(*args)


RWKV_CHUNK = 64
RWKV_NCHUNK = 2
B_COLS_PAD = 3 * B_WIDTH + 2 * LORA_PAD + 128
PAIR = 2 * B_HDIM


def _softplus(z):
    return jnp.maximum(z, 0.0) + jnp.log1p(jnp.exp(-jnp.abs(z)))


def _rwkv_kernel(pb_ref, S0_ref, sh0_ref, mu_ref, w0_ref, ww2_ref, a0_ref, wa2_ref, wg2_ref,
                 kk_ref, ka_ref, rk_ref, gw_ref, gb_ref, bd_ref,
                 hb_o, S1_o, sh1_o, S_sc, sbuf, *, nv):
    C = RWKV_CHUNK
    nc = -(-nv // C)
    TB = nc * C
    t = pl.program_id(1)

    @pl.when(t == 0)
    def _init():
        S_sc[...] = S0_ref[0]
        sbuf[7:8] = sh0_ref[0]

    sbuf[8:8 + nv] = pb_ref[...]
    if nv < TB:
        sbuf[8 + nv:8 + TB] = jnp.zeros((TB - nv, B_COLS_PAD), F32)
    pb = sbuf[8:8 + TB]
    prev = sbuf[7:7 + TB]
    last_row = sbuf[7 + nv:8 + nv]
    xm = pb + (prev - pb) * mu_ref[...]
    W = B_WIDTH
    r = xm[:, 0:W]
    kb = xm[:, W:2 * W]
    vb = xm[:, 2 * W:3 * W]
    wl = xm[:, 3 * W:3 * W + LORA_PAD]
    al = xm[:, 3 * W + LORA_PAD:3 * W + 2 * LORA_PAD]
    gl = xm[:, 3 * W + 2 * LORA_PAD:]
    w_log = -_softplus(-(w0_ref[...] + _dot3(jnp.tanh(wl), ww2_ref[...]))) - 0.5
    lw = -jnp.exp(w_log)
    a = _sigmoid(a0_ref[...] + _dot3(al, wa2_ref[...]))
    g = _dot(_sigmoid(gl).astype(BF16), wg2_ref[...])
    bd = bd_ref[...]
    kkr = kb * kk_ref[...]
    kk = kkr / jnp.maximum(jnp.sqrt(_dotx2(kkr * kkr, bd)), 1e-12)
    k = kb * (1.0 + (a - 1.0) * ka_ref[...])
    bonus = _dotx2(r * k * rk_ref[...], bd) * vb
    if nv < TB:
        valid = lax.broadcasted_iota(I32, (TB, W), 0) < nv
        zero = lambda z: jnp.where(valid, z, 0.0)
        lw, kk, k, vb = zero(lw), zero(kk), zero(k), zero(vb)

    lane = lax.broadcasted_iota(I32, (C, PAIR), 1)
    h0 = lane < B_HDIM
    sep = lambda z: jnp.concatenate([jnp.where(h0, z, 0.0), jnp.where(h0, 0.0, z)], axis=0)
    ri = lax.broadcasted_iota(I32, (C, C), 0)
    ci = lax.broadcasted_iota(I32, (C, C), 1)
    tril_c = jnp.where(ri >= ci, 1.0, 0.0).astype(BF16)
    r2 = lax.broadcasted_iota(I32, (PAIR, PAIR), 0)
    c2 = lax.broadcasted_iota(I32, (PAIR, PAIR), 1)
    low_s = r2 > c2
    low_i = r2 >= c2
    eye = jnp.where(r2 == c2, 1.0, 0.0)
    npair = B_HEADS // 2
    units = [(c, p) for c in range(nc) for p in range(npair)]
    cat0 = lambda x, y: jnp.concatenate([x, y], axis=0)
    cat1 = lambda x, y: jnp.concatenate([x, y], axis=1)
    bf = lambda z: z.astype(BF16)
    lhs, rhs_t, vs, wc = [], [], [], []
    for (c, p) in units:
        rs = slice(c * C, (c + 1) * C)
        ls = slice(p * PAIR, (p + 1) * PAIR)
        lwc = lw[rs, ls]
        b = _dotx_l(tril_c, lwc)
        emb = jnp.exp(-b)
        kkc = kk[rs, ls]
        lhs.append(bf(cat0(sep(kkc * jnp.exp(b - lwc)), sep(r[rs, ls] * jnp.exp(b)))))
        rhs_t.append(bf(cat0(sep(k[rs, ls] * emb), sep(kkc * a[rs, ls] * emb))))
        vs.append(bf(sep(vb[rs, ls])))
        wc.append(jnp.exp(b[C - 1:C]))
    A = [_dot_nt(l_, r_) for l_, r_ in zip(lhs, rhs_t)]
    Ak = [jnp.where(low_s, x[:PAIR, :PAIR], 0.0) for x in A]
    Ab = [jnp.where(low_s, x[:PAIR, PAIR:], 0.0) for x in A]
    Ark = [jnp.where(low_i, x[PAIR:, :PAIR], 0.0) for x in A]
    Arb = [jnp.where(low_i, x[PAIR:, PAIR:], 0.0) for x in A]
    X2 = [_dot(bf(cat0(x, y)), v_) for x, y, v_ in zip(Ak, Ark, vs)]
    VtK = [_dot_tn(v_, r_[:PAIR]) for v_, r_ in zip(vs, rhs_t)]
    Tm = [eye - x for x in Ab]
    P = [_dot(bf(x), bf(x)) for x in Ab]
    n = 2
    while 2 * n < C:
        X = [_dot(bf(cat0(t_, p_)), bf(p_)) for t_, p_ in zip(Tm, P)]
        Tm = [t_ + x[:PAIR] for t_, x in zip(Tm, X)]
        P = [x[PAIR:] for x in X]
        n *= 2
    Tm = [t_ + _dot(bf(t_), bf(p_)) for t_, p_ in zip(Tm, P)]
    Y = [_dot(bf(t_), cat1(l_[:PAIR], bf(x[:PAIR]))) for t_, l_, x in zip(Tm, lhs, X2)]
    Z = [_dot(bf(x), bf(y)) for x, y in zip(Arb, Y)]
    Rp = [bf(l_[PAIR:].astype(F32) - z[:, :PAIR]) for l_, z in zip(lhs, Z)]
    Op = [x[PAIR:] - z[:, PAIR:] for x, z in zip(X2, Z)]
    W = [_dot_tn(bf(y), r_[PAIR:]) for y, r_ in zip(Y, rhs_t)]
    Mc = [bf(w_[:PAIR] * c_) for w_, c_ in zip(W, wc)]
    Nn = [(v_ - w_[PAIR:]) * c_ for v_, w_, c_ in zip(VtK, W, wc)]
    o_rows = []
    for c in range(nc):
        S = [S_sc[p] for p in range(npair)]
        Sb = [bf(s_) for s_ in S]
        u0 = c * npair
        O = [_dot_nt(Rp[u0 + p], Sb[p]) + Op[u0 + p] for p in range(npair)]
        for p in range(npair):
            S_sc[p] = S[p] * wc[u0 + p] - _dot(Sb[p], Mc[u0 + p]) + Nn[u0 + p]
        o_rows.append(jnp.concatenate([x[:C] + x[C:] for x in O], axis=1))
    o = o_rows[0] if nc == 1 else jnp.concatenate(o_rows, axis=0)

    mean = _dotx2(o, bd) * (1.0 / B_HDIM)
    d = o - mean
    var = _dotx2(d * d, bd) * (1.0 / B_HDIM)
    hb = (d * lax.rsqrt(var + GN_EPS) * gw_ref[...] + gb_ref[...] + bonus) * g
    hb_o[...] = hb[:nv].astype(BF16)
    sbuf[7:8] = last_row

    @pl.when(t == pl.num_programs(1) - 1)
    def _fin():
        S1_o[0] = S_sc[...]
        sh1_o[0] = last_row


def _rwkv(pb, S0, sh0, mu, w0, ww2, a0, wa2, wg2, k_k, k_a, r_k, gn_w, gn_b, bd, *, n_seq, L, row_off):
    nv = min(L, RWKV_CHUNK * RWKV_NCHUNK)
    nst = L // nv
    off = row_off // nv
    TB = -(-nv // RWKV_CHUNK) * RWKV_CHUNK
    tok = lambda n: pl.BlockSpec((nv, n), lambda s, t: (off + s * nst + t, 0))
    st = lambda shp: pl.BlockSpec((1,) + shp, lambda s, t: (s,) + (0,) * len(shp))
    full = lambda a: pl.BlockSpec(a.shape, lambda s, t: (0,) * a.ndim)
    consts = [mu, w0, ww2, a0, wa2, wg2, k_k, k_a, r_k, gn_w, gn_b, bd]
    npair = B_HEADS // 2
    in_specs = [tok(B_COLS_PAD), st((npair, PAIR, PAIR)), st((1, B_COLS_PAD))] + [full(a) for a in consts]
    args = [pb, S0, sh0] + consts
    out_shape = [jax.ShapeDtypeStruct((n_seq * L, B_WIDTH), BF16),
                 jax.ShapeDtypeStruct((n_seq, npair, PAIR, PAIR), F32),
                 jax.ShapeDtypeStruct((n_seq, 1, B_COLS_PAD), F32)]
    out_specs = [pl.BlockSpec((nv, B_WIDTH), lambda s, t: (s * nst + t, 0)),
                 st((npair, PAIR, PAIR)), st((1, B_COLS_PAD))]
    return pl.pallas_call(
        functools.partial(_rwkv_kernel, nv=nv),
        grid=(n_seq, nst),
        in_specs=in_specs, out_specs=out_specs, out_shape=out_shape,
        scratch_shapes=[pltpu.VMEM((npair, PAIR, PAIR), F32), pltpu.VMEM((TB + 8, B_COLS_PAD), F32)],
        compiler_params=_cparams(("arbitrary", "arbitrary")),
        name="rwkv",
    )(*args)


SUB = 8


def _rows_to_tiles(ref, x):
    rows = x.shape[0]
    for j in range(SUB):
        ref[pl.ds(j, rows, stride=SUB), :] = x[:, j * LANES:(j + 1) * LANES]


def _tiles_to_rows(ref, rows):
    return jnp.concatenate([ref[pl.ds(j, rows, stride=SUB), :] for j in range(SUB)], axis=1)


def _row_tile(ref, r8):
    return ref.at[pl.ds(pl.multiple_of(r8, SUB), SUB)]


def _out_router_kernel(hap_ref, has_ref, hbp_ref, hbs_ref, xp_ref, xs_ref, woa_ref, wob_ref, nf_ref, wr_ref, br_ref,
                       h1_o, u2_o, ids_o, gates_o, rank_o, cnt_o, carry, *, ntp):
    i = pl.program_id(0)

    @pl.when(i == 0)
    def _init():
        carry[...] = jnp.zeros_like(carry)

    ha = _pick(i, ntp, hap_ref, has_ref)
    hb = _pick(i, ntp, hbp_ref, hbs_ref)
    h1 = _pick(i, ntp, xp_ref, xs_ref) + _dot(ha, woa_ref[...]) + _dot(hb, wob_ref[...])
    h1_o[...] = h1
    u2 = _rms(h1, nf_ref[...])
    _rows_to_tiles(u2_o, u2)
    logits = _dot3(u2, wr_ref[...]) + br_ref[...]
    TM = logits.shape[0]
    lane = lax.broadcasted_iota(I32, (TM, LANES), 1)
    lane_f = lane.astype(F32)
    work = logits
    vals, idxs = [], []
    for _ in range(TOP_K):
        m = jnp.max(work, axis=1, keepdims=True)
        idx = jnp.min(jnp.where(work == m, lane_f, float(LANES)), axis=1, keepdims=True)
        vals.append(m)
        idxs.append(idx)
        work = jnp.where(lane_f == idx, -jnp.inf, work)
    es = [jnp.exp(v - vals[0]) for v in vals]
    tot = es[0] + es[1] + es[2] + es[3]
    member = jnp.zeros((TM, LANES), F32)
    ids = jnp.zeros((TM, LANES), F32)
    gates = jnp.zeros((TM, LANES), F32)
    for kk in range(TOP_K):
        member = member + jnp.where(lane_f == idxs[kk], 1.0, 0.0)
        ids = jnp.where(lane == kk, idxs[kk], ids)
        gates = jnp.where(lane == kk, es[kk] / tot, gates)
    r2 = lax.broadcasted_iota(I32, (TM, TM), 0)
    c2 = lax.broadcasted_iota(I32, (TM, TM), 1)
    before = _dot(jnp.where(r2 > c2, 1.0, 0.0).astype(BF16), member.astype(BF16)) + carry[0:1]
    rank = jnp.zeros((TM, LANES), F32)
    for kk in range(TOP_K):
        rk = jnp.sum(jnp.where(lane_f == idxs[kk], before, 0.0), axis=1, keepdims=True)
        rank = jnp.where(lane == kk, rk, rank)
    ids_o[...] = ids.astype(I32)
    gates_o[...] = gates
    rank_o[...] = rank.astype(I32)
    carry[0:1] = carry[0:1] + jnp.sum(member, axis=0, keepdims=True)
    cnt_o[...] = carry[0:1].astype(I32)


def _out_router(hap, has, hbp, hbs, xp, xs, woa, wob, nf, wr, br):
    D = xp.shape[1]
    T = xp.shape[0] + xs.shape[0]
    ntp = xp.shape[0] // ROW_TILE
    nt = T // ROW_TILE
    row = lambda n: pl.BlockSpec((ROW_TILE, n), lambda i: (i, 0))
    full = lambda a: pl.BlockSpec(a.shape, lambda i: (0, 0))
    return pl.pallas_call(
        functools.partial(_out_router_kernel, ntp=ntp),
        grid=(nt,),
        in_specs=_two_specs(ntp, A_WIDTH) + _two_specs(ntp, B_WIDTH) + _two_specs(ntp, D)
        + [full(woa), full(wob), full(nf), full(wr), full(br)],
        out_specs=[row(D), pl.BlockSpec((ROW_TILE * SUB, LANES), lambda i: (i, 0)),
                   row(LANES), row(LANES), row(LANES), pl.BlockSpec((1, LANES), lambda i: (0, 0))],
        out_shape=[jax.ShapeDtypeStruct((T, D), F32), jax.ShapeDtypeStruct((T * SUB, LANES), F32),
                   jax.ShapeDtypeStruct((T, LANES), I32), jax.ShapeDtypeStruct((T, LANES), F32),
                   jax.ShapeDtypeStruct((T, LANES), I32), jax.ShapeDtypeStruct((1, LANES), I32)],
        scratch_shapes=[pltpu.VMEM((8, LANES), F32)],
        compiler_params=_cparams(("arbitrary",)),
        name="out_router",
    )(hap, has, hbp, hbs, xp, xs, woa, wob, nf, wr, br)


MOE_BLOCK = 512


def _row_copy_wait(buf, sem, n):
    for _ in range(n):
        pltpu.make_async_copy(buf, buf, sem).wait()


DMA_UNROLL = 4


def _scatter_kernel(plo_ref, phi_ref, nu_ref, dest_ref, u_ref, xs_o, zbuf, sem):
    TM = u_ref.shape[0] // SUB
    BLK = MOE_BLOCK * SUB

    @pl.when(pl.program_id(0) == 0)
    def _zero_unused_slots():
        zbuf[...] = jnp.zeros_like(zbuf)
        zrow = zbuf.at[pl.ds(0, SUB)]

        def per_expert(e, n):
            def per_row(rw, _):
                pltpu.make_async_copy(zrow, _row_tile(xs_o, rw * SUB), sem).start()
                return 0

            lax.fori_loop(plo_ref[e], phi_ref[e], per_row, 0)
            return n + (phi_ref[e] - plo_ref[e])

        n_rows = lax.fori_loop(0, plo_ref.shape[0], per_expert, 0)

        def wait_row(_, c):
            pltpu.make_async_copy(zrow, xs_o.at[pl.ds(0, SUB)], sem).wait()
            return c

        lax.fori_loop(0, n_rows, wait_row, 0)

        def per_block(j, _):
            cp = pltpu.make_async_copy(zbuf, xs_o.at[pl.ds(pl.multiple_of(j * BLK, BLK), BLK)], sem)
            cp.start()
            cp.wait()
            return 0

        lax.fori_loop(nu_ref[0], xs_o.shape[0] // BLK, per_block, 0)

    def body(i, _):
        src = _row_tile(u_ref, i * SUB)
        for kk in range(TOP_K):
            pltpu.make_async_copy(src, _row_tile(xs_o, dest_ref[i * TOP_K + kk]), sem).start(priority=kk % 2)
        return 0

    lax.fori_loop(0, TM, body, 0, unroll=DMA_UNROLL)
    _row_copy_wait(u_ref, sem, TOP_K)


def _scatter_rows(pad_lo, pad_hi, n_used, dest8_flat, u2t, n_slots):
    T = u2t.shape[0] // SUB
    grid_spec = pltpu.PrefetchScalarGridSpec(
        num_scalar_prefetch=3,
        grid=(T // ROW_TILE,),
        in_specs=[pl.BlockSpec((ROW_TILE * TOP_K,), lambda i, *_: (i,), memory_space=pltpu.SMEM),
                  pl.BlockSpec((ROW_TILE * SUB, LANES), lambda i, *_: (i, 0))],
        out_specs=pl.BlockSpec(memory_space=pl.ANY),
        scratch_shapes=[pltpu.VMEM((MOE_BLOCK * SUB, LANES), F32), pltpu.SemaphoreType.DMA(())],
    )
    return pl.pallas_call(
        _scatter_kernel,
        grid_spec=grid_spec,
        out_shape=jax.ShapeDtypeStruct((n_slots * SUB, LANES), F32),
        compiler_params=_cparams(("arbitrary",)),
        name="moe_scatter",
    )(pad_lo, pad_hi, n_used, dest8_flat, u2t)


def _expert_kernel(be_ref, nu_ref, xs_ref, wgu_ref, bgu_ref, wd_ref, bd_ref, ys_o, wgu_b, wd_b):
    j = pl.program_id(0)
    F = wd_ref.shape[1]
    used = j < nu_ref[0]

    @pl.when(used & ((j == 0) | (be_ref[j] != be_ref[jnp.maximum(j - 1, 0)])))
    def _new_expert():
        wgu_b[...] = wgu_ref[0].astype(BF16)
        wd_b[...] = wd_ref[0].astype(BF16)

    @pl.when(used)
    def _compute():
        x = _tiles_to_rows(xs_ref, MOE_BLOCK)
        gu = _dot(x.astype(BF16), wgu_b[...]) + bgu_ref[0]
        g_ = jnp.minimum(gu[:, :F], SWIGLU_LIMIT)
        up = jnp.clip(gu[:, F:], -SWIGLU_LIMIT, SWIGLU_LIMIT)
        act = (up + 1.0) * g_ * _sigmoid(SWIGLU_ALPHA * g_)
        _rows_to_tiles(ys_o, _dot(act.astype(BF16), wd_b[...]) + bd_ref[0])

    @pl.when(j >= nu_ref[0])
    def _unused():
        ys_o[...] = jnp.zeros_like(ys_o)


def _experts(blk_e, n_used, xs, wgu, bgu, wd, bd):
    nb = xs.shape[0] // (MOE_BLOCK * SUB)
    E, D, F2 = wgu.shape
    F = F2 // 2
    grid_spec = pltpu.PrefetchScalarGridSpec(
        num_scalar_prefetch=2,
        grid=(nb,),
        in_specs=[pl.BlockSpec((MOE_BLOCK * SUB, LANES), lambda j, be, nu: (jnp.minimum(j, nu[0] - 1), 0)),
                  pl.BlockSpec((1, D, F2), lambda j, be, nu: (be[j], 0, 0)),
                  pl.BlockSpec((1, 1, F2), lambda j, be, nu: (be[j], 0, 0)),
                  pl.BlockSpec((1, F, D), lambda j, be, nu: (be[j], 0, 0)),
                  pl.BlockSpec((1, 1, D), lambda j, be, nu: (be[j], 0, 0))],
        out_specs=pl.BlockSpec((MOE_BLOCK * SUB, LANES), lambda j, be, nu: (j, 0)),
        scratch_shapes=[pltpu.VMEM((D, F2), BF16), pltpu.VMEM((F, D), BF16)],
    )
    return pl.pallas_call(
        _expert_kernel,
        grid_spec=grid_spec,
        out_shape=jax.ShapeDtypeStruct(xs.shape, F32),
        compiler_params=_cparams(("arbitrary",)),
        name="moe_experts",
    )(blk_e, n_used, xs, wgu, bgu, wd, bd)


def _combine_kernel(dest0_ref, destn_ref, h1_ref, pp_ref, ps_ref, gates_ref, ys_ref, npl_ref, wpg_ref, wpp_ref, fn_ref,
                    yp_o, ysm_o, gbuf, ybuf, sems, *, ntp, n_tiles):
    TM = h1_ref.shape[0]
    t = pl.program_id(0)

    def fetch(dref, i, slot):
        for kk in range(TOP_K):
            src = _row_tile(ys_ref, dref[i * TOP_K + kk])
            pltpu.make_async_copy(src, _row_tile(gbuf.at[slot, kk], i * SUB), sems.at[slot]).start(priority=kk % 2)

    @pl.when(t == 0)
    def _first_fetch():
        def body(i, _):
            fetch(dest0_ref, i, 0)
            return 0

        lax.fori_loop(0, TM, body, 0, unroll=DMA_UNROLL)

    def tile(slot):
        _row_copy_wait(gbuf.at[slot, 0], sems.at[slot], TOP_K)
        for i in range(TM):
            fetch(destn_ref, i, 1 - slot)
        gates = gates_ref[...]
        h2 = h1_ref[...]
        for kk in range(TOP_K):
            h2 = h2 + _tiles_to_rows(gbuf.at[slot, kk], TM) * gates[:, kk:kk + 1]
        u3 = _rms(h2, npl_ref[...])
        gate = _sigmoid(_dot(u3.astype(BF16), wpg_ref[...]))
        p = _pick(t, ntp, pp_ref, ps_ref)
        h3 = h2 + gate * _dot(p.astype(BF16), wpp_ref[...])
        return _rms(h3, fn_ref[...])

    for slot in range(2):
        @pl.when(t % 2 == slot)
        def _tile():
            ybuf[...] = tile(slot)

    @pl.when(t < ntp)
    def _prompt_rows():
        yp_o[...] = ybuf[...]

    @pl.when(t >= ntp)
    def _sample_rows():
        ysm_o[...] = ybuf[...]

    @pl.when(t == n_tiles - 1)
    def _drain():
        other = n_tiles % 2
        _row_copy_wait(gbuf.at[other, 0], sems.at[other], TOP_K)


def _combine(dest_flat, h1, pp, ps, gates, ys, npl, wpg, wpp, fn):
    T, D = h1.shape
    nt = T // ROW_TILE
    ntp = pp.shape[0] // ROW_TILE
    row = lambda n: pl.BlockSpec((ROW_TILE, n), lambda i: (i, 0))
    full = lambda a: pl.BlockSpec(a.shape, lambda i: (0, 0))
    idx = lambda f: pl.BlockSpec((ROW_TILE * TOP_K,), f, memory_space=pltpu.SMEM)
    return pl.pallas_call(
        functools.partial(_combine_kernel, ntp=ntp, n_tiles=nt),
        grid=(nt,),
        in_specs=[idx(lambda i: (0,)), idx(lambda i: (jnp.minimum(i + 1, nt - 1),)), row(D)]
        + _two_specs(ntp, pp.shape[1])
        + [row(LANES), pl.BlockSpec(memory_space=pl.ANY), full(npl), full(wpg), full(wpp), full(fn)],
        out_specs=_two_specs(ntp, D),
        out_shape=[jax.ShapeDtypeStruct((pp.shape[0], D), F32), jax.ShapeDtypeStruct((ps.shape[0], D), F32)],
        scratch_shapes=[pltpu.VMEM((2, TOP_K, ROW_TILE * SUB, LANES), F32), pltpu.VMEM((ROW_TILE, D), F32),
                        pltpu.SemaphoreType.DMA((2,))],
        compiler_params=_cparams(("arbitrary",)),
        name="moe_combine_ple",
    )(dest_flat, dest_flat, h1, pp, ps, gates, ys, npl, wpg, wpp, fn)


def _pad_lora_cols(x):
    W3 = 3 * B_WIDTH
    z = jnp.zeros(x.shape[:-1] + (LORA_PAD - 64,), x.dtype)
    return jnp.concatenate([x[..., :W3 + 64], z, x[..., W3 + 64:W3 + 128], z, x[..., W3 + 128:]], axis=-1)


def _unpad_lora_cols(x):
    W3 = 3 * B_WIDTH
    return jnp.concatenate([x[..., :W3 + 64], x[..., W3 + LORA_PAD:W3 + LORA_PAD + 64], x[..., W3 + 2 * LORA_PAD:]], axis=-1)


def _heads_to_pairs(S):
    n = S.shape[0]
    S = S.reshape(n, B_HEADS // 2, 2, B_HDIM, B_HDIM)
    z = jnp.zeros_like(S[:, :, 0])
    top = jnp.concatenate([S[:, :, 0], z], axis=-1)
    bot = jnp.concatenate([z, S[:, :, 1]], axis=-1)
    return jnp.concatenate([top, bot], axis=-2)


def _pairs_to_heads(Sb):
    n = Sb.shape[0]
    a = Sb[:, :, :B_HDIM, :B_HDIM]
    b = Sb[:, :, B_HDIM:, B_HDIM:]
    return jnp.stack([a, b], axis=2).reshape(n, B_HEADS, B_HDIM, B_HDIM)


def kernel(x_prompt, x_sample, state_mlstm_C, state_mlstm_n, state_mlstm_m, state_mlstm_conv, state_rwkv_S, state_rwkv_shift, p_prompt, p_sample, norm_mix, w_in, conv_w, conv_b, b_ig, b_fg, a_norm, shift_mu, w0, w_w2, a0, w_a2, w_g2, k_k, k_a, r_k, gn_w, gn_b, w_out, norm_ffn, w_router, b_router, w_gu, b_gu, w_down, b_down, norm_ple, w_ple_gate, w_ple_proj, final_norm):
    assert norm_mix.shape[0] == 1, "single-layer trunk"
    nbp, Lp, D = x_prompt.shape
    nbs, Ls, _ = x_sample.shape
    Tp, Ts = nbp * Lp, nbs * Ls
    T = Tp + Ts
    row = lambda a: a.reshape(1, -1).astype(F32)

    wi = w_in[0]
    AW2 = 2 * A_WIDTH
    wqk = wi[:, :AW2].astype(BF16)
    wv = wi[:, AW2:AW2 + A_WIDTH].astype(BF16)
    wo = wi[:, AW2 + A_WIDTH:AW2 + 2 * A_WIDTH].astype(BF16)
    a_cols = 4 * A_WIDTH + 2 * A_HEADS
    wif = jnp.pad(wi[:, 4 * A_WIDTH:a_cols], ((0, 0), (0, LANES - 2 * A_HEADS)))
    wbg = jnp.concatenate([_pad_lora_cols(wi[:, a_cols:]), wif], axis=1).astype(BF16)
    gate_bias = jnp.pad(jnp.concatenate([b_ig[0], b_fg[0]]), (0, LANES - 2 * A_HEADS)).reshape(1, LANES)
    padw = lambda w: jnp.pad(w, ((0, LORA_PAD - w.shape[0]), (0, 0)))
    hid = jnp.arange(B_WIDTH) // B_HDIM
    bd = (hid[:, None] == hid[None, :]).astype(BF16)
    E = w_router.shape[-1]
    wr = jnp.pad(w_router[0], ((0, 0), (0, LANES - E)))
    br = jnp.pad(b_router[0], (0, LANES - E), constant_values=NEG_BIG).reshape(1, LANES)

    assert Tp % ROW_TILE == 0 and Ts % ROW_TILE == 0
    xp, xsm = x_prompt.reshape(Tp, D), x_sample.reshape(Ts, D)
    qk, v, og, pb = _in_proj(xp, xsm, row(norm_mix), wqk, wv, wo, wbg)
    ifg = pb

    f0 = lambda *shp: jnp.zeros(shp, F32)
    ml_consts = (conv_w[0], row(conv_b), gate_bias, row(a_norm))
    m_s = jnp.pad(state_mlstm_m[0], ((0, 0), (0, LANES - A_HEADS))).reshape(nbs, 1, LANES)
    hap, Cp, np_, mp, convp = _mlstm(qk, v, og, ifg, f0(nbp, A_HEADS, A_HDIM, A_HDIM), f0(nbp, A_HEADS, A_HDIM),
                                     f0(nbp, 1, LANES), f0(nbp, CONV_W - 1, AW2), *ml_consts,
                                     n_seq=nbp, L=Lp, row_off=0)
    has, Cs, ns_, ms, convs = _mlstm(qk, v, og, ifg, state_mlstm_C[0], state_mlstm_n[0], m_s, state_mlstm_conv[0],
                                     *ml_consts, n_seq=nbs, L=Ls, row_off=Tp)

    rw_consts = (row(_pad_lora_cols(shift_mu[0])), row(w0), padw(w_w2[0]), row(a0), padw(w_a2[0]), w_g2[0].astype(BF16),
                 row(k_k), row(k_a), row(r_k), row(gn_w), row(gn_b), bd)
    npair = B_HEADS // 2
    hbp, Sp, shp_ = _rwkv(pb, f0(nbp, npair, PAIR, PAIR), f0(nbp, 1, B_COLS_PAD), *rw_consts,
                          n_seq=nbp, L=Lp, row_off=0)
    hbs, Ss, shs = _rwkv(pb, _heads_to_pairs(state_rwkv_S[0]), _pad_lora_cols(state_rwkv_shift[0]), *rw_consts,
                         n_seq=nbs, L=Ls, row_off=Tp)

    wout = w_out[0].astype(BF16)
    h1, u2, ids, gates, rank, counts = _out_router(hap, has, hbp, hbs, xp, xsm, wout[:A_WIDTH], wout[A_WIDTH:],
                                                   row(norm_ffn), wr, br)

    counts = counts[0, :E]
    pcounts = (counts + MOE_BLOCK - 1) // MOE_BLOCK * MOE_BLOCK
    pend = jnp.cumsum(pcounts)
    pstart = pend - pcounts
    n_blk = -(-T * TOP_K // MOE_BLOCK) + E
    dest = ((jnp.take(pstart, ids[:, :TOP_K]) + rank[:, :TOP_K]) * SUB).reshape(-1).astype(I32)
    blk_start = jnp.arange(n_blk, dtype=I32) * MOE_BLOCK
    blk_e = jnp.minimum(jnp.sum(pend[None, :] <= blk_start[:, None], axis=1), E - 1).astype(I32)
    n_used = (pend[E - 1:] // MOE_BLOCK).astype(I32)

    xs = _scatter_rows((pstart + counts).astype(I32), pend.astype(I32), n_used, dest, u2, n_blk * MOE_BLOCK)
    ys = _experts(blk_e, n_used, xs, w_gu[0], b_gu[0][:, None, :], w_down[0], b_down[0][:, None, :])
    yp, ysm = _combine(dest, h1, p_prompt[0].reshape(Tp, -1), p_sample[0].reshape(Ts, -1), gates, ys, row(norm_ple),
                       w_ple_gate[0].astype(BF16), w_ple_proj[0].astype(BF16), row(final_norm))

    lead = lambda a: a[None]
    return (yp.reshape(nbp, Lp, D), ysm.reshape(nbs, Ls, D),
            lead(Cp), lead(np_), lead(mp[:, 0, :A_HEADS]), lead(convp), lead(_pairs_to_heads(Sp)), lead(_unpad_lora_cols(shp_)),
            lead(Cs), lead(ns_), lead(ms[:, 0, :A_HEADS]), lead(convs), lead(_pairs_to_heads(Ss)), lead(_unpad_lora_cols(shs)))
```

```python
import functools

import jax
import jax.numpy as jnp
from jax import lax
from jax.experimental import pallas as pl
from jax.experimental.pallas import tpu as pltpu

F32 = jnp.float32
BF16 = jnp.bfloat16
I32 = jnp.int32

EPS = 1e-6
GN_EPS = 64e-5
A_HEADS, A_HDIM = 4, 128
B_HEADS, B_HDIM = 8, 64
A_WIDTH = A_HEADS * A_HDIM
B_WIDTH = B_HEADS * B_HDIM
CONV_W = 4
LORA_PAD = 128
N_EXPERTS, TOP_K = 32, 4
SWIGLU_LIMIT, SWIGLU_ALPHA = 7.0, 1.702
LANES = 128
ROW_TILE = 256
VMEM_LIMIT = 56 * 1024 * 1024


def _cparams(sem):
    return pltpu.CompilerParams(dimension_semantics=sem, vmem_limit_bytes=VMEM_LIMIT)


def _dot(a, b):
    return lax.dot_general(a, b, (((1,), (0,)), ((), ())), preferred_element_type=F32)


def _dot_nt(a, b):
    return lax.dot_general(a, b, (((1,), (1,)), ((), ())), preferred_element_type=F32)


def _dot_tn(a, b):
    return lax.dot_general(a, b, (((0,), (0,)), ((), ())), preferred_element_type=F32)


def _split2(x):
    hi = x.astype(BF16)
    lo = (x - hi.astype(F32)).astype(BF16)
    return hi, lo


def _split3(x):
    hi = x.astype(BF16)
    r1 = x - hi.astype(F32)
    mid = r1.astype(BF16)
    lo = (r1 - mid.astype(F32)).astype(BF16)
    return hi, mid, lo


def _dot3(a, b, f=_dot):
    ah, al = _split2(a)
    bh, bl = _split2(b)
    return f(ah, bh) + (f(ah, bl) + f(al, bh))


def _dotx_l(a_exact, b, f=_dot):
    b1, b2, b3 = _split3(b)
    return f(a_exact, b1) + (f(a_exact, b2) + f(a_exact, b3))


def _rms(x, g):
    return x * lax.rsqrt(jnp.mean(x * x, axis=-1, keepdims=True) + EPS) * g


def _sigmoid(x):
    return 1.0 / (1.0 + jnp.exp(-x))


def _two_specs(ntp, n):
    return [pl.BlockSpec((ROW_TILE, n), lambda i: (jnp.minimum(i, ntp - 1), 0)),
            pl.BlockSpec((ROW_TILE, n), lambda i: (jnp.maximum(i - ntp, 0), 0))]


def _pick(i, ntp, p_ref, s_ref):
    return jnp.where(i < ntp, p_ref[...], s_ref[...])


def _in_proj_kernel(xp_ref, xs_ref, g_ref, wqk_ref, wv_ref, wo_ref, wbg_ref,
                    qk_o, v_o, og_o, pbg_o, *, ntp):
    x = _pick(pl.program_id(0), ntp, xp_ref, xs_ref)
    ub = _rms(x, g_ref[...]).astype(BF16)
    qk_o[...] = _dot(ub, wqk_ref[...])
    v_o[...] = _dot(ub, wv_ref[...])
    og_o[...] = _dot(ub, wo_ref[...])
    pbg_o[...] = _dot(ub, wbg_ref[...])


def _in_proj(xp, xs, g, wqk, wv, wo, wbg):
    D = xp.shape[1]
    T = xp.shape[0] + xs.shape[0]
    ntp = xp.shape[0] // ROW_TILE
    row = lambda n: pl.BlockSpec((ROW_TILE, n), lambda i: (i, 0))
    full = lambda a: pl.BlockSpec(a.shape, lambda i: (0, 0))
    outs = [wqk.shape[1], wv.shape[1], wo.shape[1], wbg.shape[1]]
    return pl.pallas_call(
        functools.partial(_in_proj_kernel, ntp=ntp),
        grid=(T // ROW_TILE,),
        in_specs=_two_specs(ntp, D) + [full(g), full(wqk), full(wv), full(wo), full(wbg)],
        out_specs=[row(n) for n in outs],
        out_shape=[jax.ShapeDtypeStruct((T, n), F32) for n in outs],
        compiler_params=_cparams(("arbitrary",)),
        name="in_proj",
    )(xp, xs, g, wqk, wv, wo, wbg)


MLSTM_CHUNK = 128
NEG_BIG = -1e30


def _log_sigmoid(x):
    return jnp.minimum(x, 0.0) - jnp.log1p(jnp.exp(-jnp.abs(x)))


def _mlstm_kernel(qk_ref, v_ref, og_ref, if_ref, C0_ref, n0_ref, m0_ref, conv0_ref,
                  cw_ref, cb_ref, gb_ref, an_ref,
                  ha_o, C1_o, n1_o, m1_o, conv1_o, C_sc, n_sc, m_sc, cbuf, *, nv):
    CH = MLSTM_CHUNK
    c = pl.program_id(1)

    @pl.when(c == 0)
    def _init():
        C_sc[...] = C0_ref[0]
        n_sc[0:A_HEADS] = n0_ref[0]
        m_sc[0:1] = m0_ref[0]
        cbuf[5:8] = conv0_ref[0]

    cbuf[8:8 + nv] = qk_ref[...]
    if nv < CH:
        cbuf[8 + nv:8 + CH] = jnp.zeros((CH - nv, 2 * A_WIDTH), F32)
    y = cb_ref[...] + cbuf[5:5 + CH] * cw_ref[0:1]
    for j in range(1, CONV_W):
        y = y + cbuf[5 + j:5 + j + CH] * cw_ref[j:j + 1]
    conv_tail = cbuf[5 + nv:8 + nv]
    qk = y * _sigmoid(y)
    q = qk[:, :A_WIDTH]
    k = qk[:, A_WIDTH:] * (A_HDIM ** -0.5)
    if nv == CH:
        v = v_ref[...]
        og = og_ref[...]
        gp = if_ref[...] + gb_ref[...]
    else:
        zpad = lambda n: jnp.zeros((CH - nv, n), F32)
        v = jnp.concatenate([v_ref[...], zpad(A_WIDTH)], axis=0)
        og = jnp.concatenate([og_ref[...], zpad(A_WIDTH)], axis=0)
        gp = jnp.concatenate([if_ref[...] + gb_ref[...], zpad(LANES)], axis=0)

    row = lax.broadcasted_iota(I32, (CH, LANES), 0)
    li = gp
    lf = pltpu.roll(_log_sigmoid(gp), LANES - A_HEADS, 1)
    if nv < CH:
        li = jnp.where(row < nv, li, NEG_BIG)
        lf = jnp.where(row < nv, lf, 0.0)
    r2 = lax.broadcasted_iota(I32, (CH, CH), 0)
    c2 = lax.broadcasted_iota(I32, (CH, CH), 1)
    tri = r2 >= c2
    tril_b = jnp.where(tri, 1.0, 0.0).astype(BF16)
    b = _dotx_l(tril_b, lf)
    g = li - b
    sel = jnp.where(lax.broadcasted_iota(I32, (8, LANES), 0) == lax.broadcasted_iota(I32, (8, LANES), 1),
                    1.0, 0.0).astype(BF16)
    gT = _dotx_l(sel, g, f=_dot_nt)

    qb, kb, vb = q.astype(BF16), k.astype(BF16), v.astype(BF16)
    lane1 = lax.broadcasted_iota(I32, (1, LANES), 1)
    m_row = m_sc[0:1]
    H = range(A_HEADS)
    sls = [slice(h * A_HDIM, (h + 1) * A_HDIM) for h in H]
    bcol = [b[:, h:h + 1] for h in H]
    m_h = [m_row[:, h:h + 1] for h in H]
    C_h = [C_sc[h] for h in H]
    n_h = [n_sc[h:h + 1] for h in H]
    qk_t = [_dot_nt(qb[:, sl], kb[:, sl]) for sl in sls]
    qC = [_dot_nt(qb[:, sl], C_h[h].astype(BF16)) for h, sl in enumerate(sls)]
    dm = [jnp.where(tri, bcol[h] + gT[h:h + 1, :], -jnp.inf) for h in H]
    inter = [bcol[h] + m_h[h] for h in H]
    mt = [jnp.maximum(inter[h], jnp.max(dm[h], axis=1, keepdims=True)) for h in H]
    s = [qk_t[h] * jnp.exp(dm[h] - mt[h]) for h in H]
    wi = [jnp.exp(inter[h] - mt[h]) for h in H]
    sv = [_dot(s[h].astype(BF16), vb[:, sl]) for h, sl in enumerate(sls)]
    m_new = [mt[h][CH - 1:CH] for h in H]
    w_end = [jnp.exp(bcol[h][CH - 1:CH] + g[:, h:h + 1] - m_new[h]) for h in H]
    dec = [jnp.exp(bcol[h][CH - 1:CH] + m_h[h] - m_new[h]) for h in H]
    vk = [_dot_tn((v[:, sl] * w_end[h]).astype(BF16), kb[:, sl]) for h, sl in enumerate(sls)]
    outs = []
    m_next = m_row
    for h, sl in enumerate(sls):
        num = sv[h] + wi[h] * qC[h]
        den = jnp.sum(s[h], axis=1, keepdims=True) + wi[h] * jnp.sum(q[:, sl] * n_h[h], axis=1, keepdims=True)
        hh = num / jnp.maximum(jnp.abs(den), jnp.exp(-mt[h]))
        hn = hh * lax.rsqrt(jnp.mean(hh * hh, axis=1, keepdims=True) + EPS)
        outs.append(hn * an_ref[:, sl] * _sigmoid(og[:, sl]))
        C_sc[h] = dec[h] * C_h[h] + vk[h]
        n_sc[h:h + 1] = dec[h] * n_h[h] + jnp.sum(k[:, sl] * w_end[h], axis=0, keepdims=True)
        m_next = jnp.where(lane1 == h, m_new[h], m_next)
    m_sc[0:1] = m_next
    ha = jnp.concatenate(outs, axis=1)
    ha_o[...] = ha[:nv].astype(BF16)
    cbuf[5:8] = conv_tail

    @pl.when(c == pl.num_programs(1) - 1)
    def _fin():
        C1_o[0] = C_sc[...]
        n1_o[0] = n_sc[0:A_HEADS]
        m1_o[0] = m_sc[0:1]
        conv1_o[0] = conv_tail


def _mlstm(qk, v, og, pbg, C0, n0, m0, conv0, cw, cb, gb, an, *, n_seq, L, row_off):
    CH = MLSTM_CHUNK
    nv = min(L, CH)
    nch = L // nv
    off = row_off // nv
    gate_tile = pbg.shape[1] // LANES - 1
    tok = lambda n: pl.BlockSpec((nv, n), lambda s, c: (off + s * nch + c, 0))
    st = lambda shp: pl.BlockSpec((1,) + shp, lambda s, c: (s,) + (0,) * len(shp))
    full = lambda a: pl.BlockSpec(a.shape, lambda s, c: (0,) * a.ndim)
    in_specs = [tok(2 * A_WIDTH), tok(A_WIDTH), tok(A_WIDTH),
                pl.BlockSpec((nv, LANES), lambda s, c: (off + s * nch + c, gate_tile)),
                st((A_HEADS, A_HDIM, A_HDIM)), st((A_HEADS, A_HDIM)), st((1, LANES)),
                st((CONV_W - 1, 2 * A_WIDTH)), full(cw), full(cb), full(gb), full(an)]
    args = [qk, v, og, pbg, C0, n0, m0, conv0, cw, cb, gb, an]
    out_shape = [jax.ShapeDtypeStruct((n_seq * L, A_WIDTH), BF16),
                 jax.ShapeDtypeStruct((n_seq, A_HEADS, A_HDIM, A_HDIM), F32),
                 jax.ShapeDtypeStruct((n_seq, A_HEADS, A_HDIM), F32),
                 jax.ShapeDtypeStruct((n_seq, 1, LANES), F32),
                 jax.ShapeDtypeStruct((n_seq, CONV_W - 1, 2 * A_WIDTH), F32)]
    out_specs = [pl.BlockSpec((nv, A_WIDTH), lambda s, c: (s * nch + c, 0)),
                 st((A_HEADS, A_HDIM, A_HDIM)), st((A_HEADS, A_HDIM)),
                 st((1, LANES)), st((CONV_W - 1, 2 * A_WIDTH))]
    return pl.pallas_call(
        functools.partial(_mlstm_kernel, nv=nv),
        grid=(n_seq, nch),
        in_specs=in_specs, out_specs=out_specs, out_shape=out_shape,
        scratch_shapes=[pltpu.VMEM((A_HEADS, A_HDIM, A_HDIM), F32), pltpu.VMEM((8, A_HDIM), F32),
                        pltpu.VMEM((8, LANES), F32), pltpu.VMEM((CH + 8, 2 * A_WIDTH), F32)],
        compiler_params=_cparams(("arbitrary", "arbitrary")),
        name="mlstm",
    )(*args)


RWKV_CHUNK = 64
RWKV_NCHUNK = 4
B_COLS_PAD = 3 * B_WIDTH + 2 * LORA_PAD + 128
PAIR = 2 * B_HDIM


def _softplus(z):
    return jnp.maximum(z, 0.0) + jnp.log1p(jnp.exp(-jnp.abs(z)))


def _rwkv_kernel(pb_ref, S0_ref, sh0_ref, mu_ref, w0_ref, ww2_ref, a0_ref, wa2_ref, wg2_ref,
                 kk_ref, ka_ref, rk_ref, gw_ref, gb_ref, bd_ref,
                 hb_o, S1_o, sh1_o, S_sc, sbuf, *, nv):
    C = RWKV_CHUNK
    nc = -(-nv // C)
    TB = nc * C
    t = pl.program_id(1)

    @pl.when(t == 0)
    def _init():
        S_sc[...] = S0_ref[0]
        sbuf[7:8] = sh0_ref[0]

    sbuf[8:8 + nv] = pb_ref[...]
    if nv < TB:
        sbuf[8 + nv:8 + TB] = jnp.zeros((TB - nv, B_COLS_PAD), F32)
    pb = sbuf[8:8 + TB]
    prev = sbuf[7:7 + TB]
    last_row = sbuf[7 + nv:8 + nv]
    xm = pb + (prev - pb) * mu_ref[...]
    W = B_WIDTH
    r = xm[:, 0:W]
    kb = xm[:, W:2 * W]
    vb = xm[:, 2 * W:3 * W]
    wl = xm[:, 3 * W:3 * W + LORA_PAD]
    al = xm[:, 3 * W + LORA_PAD:3 * W + 2 * LORA_PAD]
    gl = xm[:, 3 * W + 2 * LORA_PAD:]
    seg_sum = lambda z: _dot(z.astype(BF16), bd_ref[...])
    w_log = -_softplus(-(w0_ref[...] + _dot(jnp.tanh(wl).astype(BF16), ww2_ref[...]))) - 0.5
    lw = -jnp.exp(w_log)
    a = _sigmoid(a0_ref[...] + _dot(al.astype(BF16), wa2_ref[...]))
    g = _dot(_sigmoid(gl).astype(BF16), wg2_ref[...])
    kkr = kb * kk_ref[...]
    kk = kkr / jnp.maximum(jnp.sqrt(seg_sum(kkr * kkr)), 1e-12)
    k = kb * (1.0 + (a - 1.0) * ka_ref[...])
    bonus = seg_sum(r * k * rk_ref[...]) * vb
    if nv < TB:
        valid = lax.broadcasted_iota(I32, (TB, W), 0) < nv
        zero = lambda z: jnp.where(valid, z, 0.0)
        lw, kk, k, vb = zero(lw), zero(kk), zero(k), zero(vb)

    lane = lax.broadcasted_iota(I32, (C, PAIR), 1)
    h0 = lane < B_HDIM
    sep = lambda z: jnp.concatenate([jnp.where(h0, z, 0.0), jnp.where(h0, 0.0, z)], axis=0)
    ri = lax.broadcasted_iota(I32, (C, C), 0)
    ci = lax.broadcasted_iota(I32, (C, C), 1)
    tril_c = jnp.where(ri >= ci, 1.0, 0.0).astype(BF16)
    r2 = lax.broadcasted_iota(I32, (PAIR, PAIR), 0)
    c2 = lax.broadcasted_iota(I32, (PAIR, PAIR), 1)
    low_s = r2 > c2
    low_i = r2 >= c2
    eye = jnp.where(r2 == c2, 1.0, 0.0)
    npair = B_HEADS // 2
    units = [(c, p) for c in range(nc) for p in range(npair)]
    cat0 = lambda x, y: jnp.concatenate([x, y], axis=0)
    cat1 = lambda x, y: jnp.concatenate([x, y], axis=1)
    bf = lambda z: z.astype(BF16)
    lhs, rhs_t, vs, wc = [], [], [], []
    for (c, p) in units:
        rs = slice(c * C, (c + 1) * C)
        ls = slice(p * PAIR, (p + 1) * PAIR)
        lwc = lw[rs, ls]
        b = _dotx_l(tril_c, lwc)
        emb = jnp.exp(-b)
        kkc = kk[rs, ls]
        lhs.append(bf(cat0(sep(kkc * jnp.exp(b - lwc)), sep(r[rs, ls] * jnp.exp(b)))))
        rhs_t.append(bf(cat0(sep(k[rs, ls] * emb), sep(kkc * a[rs, ls] * emb))))
        vs.append(bf(sep(vb[rs, ls])))
        wc.append(jnp.exp(b[C - 1:C]))
    A = [_dot_nt(l_, r_) for l_, r_ in zip(lhs, rhs_t)]
    Ak = [jnp.where(low_s, x[:PAIR, :PAIR], 0.0) for x in A]
    Ab = [jnp.where(low_s, x[:PAIR, PAIR:], 0.0) for x in A]
    Ark = [jnp.where(low_i, x[PAIR:, :PAIR], 0.0) for x in A]
    Arb = [jnp.where(low_i, x[PAIR:, PAIR:], 0.0) for x in A]
    X2 = [_dot(bf(cat0(x, y)), v_) for x, y, v_ in zip(Ak, Ark, vs)]
    VtK = [_dot_tn(v_, r_[:PAIR]) for v_, r_ in zip(vs, rhs_t)]
    Tm = [eye - x for x in Ab]
    P = [_dot(bf(x), bf(x)) for x in Ab]
    n = 2
    while 2 * n < C:
        X = [_dot(bf(cat0(t_, p_)), bf(p_)) for t_, p_ in zip(Tm, P)]
        Tm = [t_ + x[:PAIR] for t_, x in zip(Tm, X)]
        P = [x[PAIR:] for x in X]
        n *= 2
    Tm = [t_ + _dot(bf(t_), bf(p_)) for t_, p_ in zip(Tm, P)]
    Y = [_dot(bf(t_), cat1(l_[:PAIR], bf(x[:PAIR]))) for t_, l_, x in zip(Tm, lhs, X2)]
    Z = [_dot(bf(x), bf(y)) for x, y in zip(Arb, Y)]
    Rp = [bf(l_[PAIR:].astype(F32) - z[:, :PAIR]) for l_, z in zip(lhs, Z)]
    Op = [x[PAIR:] - z[:, PAIR:] for x, z in zip(X2, Z)]
    W = [_dot_tn(bf(y), r_[PAIR:]) for y, r_ in zip(Y, rhs_t)]
    Mc = [bf(w_[:PAIR] * c_) for w_, c_ in zip(W, wc)]
    Nn = [(v_ - w_[PAIR:]) * c_ for v_, w_, c_ in zip(VtK, W, wc)]
    o_rows = []
    for c in range(nc):
        S = [S_sc[p] for p in range(npair)]
        Sb = [bf(s_) for s_ in S]
        u0 = c * npair
        O = [_dot_nt(Rp[u0 + p], Sb[p]) + Op[u0 + p] for p in range(npair)]
        for p in range(npair):
            S_sc[p] = S[p] * wc[u0 + p] - _dot(Sb[p], Mc[u0 + p]) + Nn[u0 + p]
        o_rows.append(jnp.concatenate([x[:C] + x[C:] for x in O], axis=1))
    o = o_rows[0] if nc == 1 else jnp.concatenate(o_rows, axis=0)

    mean = seg_sum(o) * (1.0 / B_HDIM)
    d = o - mean
    var = seg_sum(d * d) * (1.0 / B_HDIM)
    hb = (d * lax.rsqrt(var + GN_EPS) * gw_ref[...] + gb_ref[...] + bonus) * g
    hb_o[...] = hb[:nv].astype(BF16)
    sbuf[7:8] = last_row

    @pl.when(t == pl.num_programs(1) - 1)
    def _fin():
        S1_o[0] = S_sc[...]
        sh1_o[0] = last_row


def _rwkv(pb, S0, sh0, mu, w0, ww2, a0, wa2, wg2, k_k, k_a, r_k, gn_w, gn_b, bd, *, n_seq, L, row_off):
    nv = min(L, RWKV_CHUNK * RWKV_NCHUNK)
    nst = L // nv
    off = row_off // nv
    TB = -(-nv // RWKV_CHUNK) * RWKV_CHUNK
    tok = lambda n: pl.BlockSpec((nv, n), lambda s, t: (off + s * nst + t, 0))
    st = lambda shp: pl.BlockSpec((1,) + shp, lambda s, t: (s,) + (0,) * len(shp))
    full = lambda a: pl.BlockSpec(a.shape, lambda s, t: (0,) * a.ndim)
    consts = [mu, w0, ww2, a0, wa2, wg2, k_k, k_a, r_k, gn_w, gn_b, bd]
    npair = B_HEADS // 2
    in_specs = [tok(B_COLS_PAD), st((npair, PAIR, PAIR)), st((1, B_COLS_PAD))] + [full(a) for a in consts]
    args = [pb, S0, sh0] + consts
    out_shape = [jax.ShapeDtypeStruct((n_seq * L, B_WIDTH), BF16),
                 jax.ShapeDtypeStruct((n_seq, npair, PAIR, PAIR), F32),
                 jax.ShapeDtypeStruct((n_seq, 1, B_COLS_PAD), F32)]
    out_specs = [pl.BlockSpec((nv, B_WIDTH), lambda s, t: (s * nst + t, 0)),
                 st((npair, PAIR, PAIR)), st((1, B_COLS_PAD))]
    return pl.pallas_call(
        functools.partial(_rwkv_kernel, nv=nv),
        grid=(n_seq, nst),
        in_specs=in_specs, out_specs=out_specs, out_shape=out_shape,
        scratch_shapes=[pltpu.VMEM((npair, PAIR, PAIR), F32), pltpu.VMEM((TB + 8, B_COLS_PAD), F32)],
        compiler_params=_cparams(("arbitrary", "arbitrary")),
        name="rwkv",
    )(*args)


SUB = 8


def _rows_to_tiles(ref, x):
    rows = x.shape[0]
    for j in range(SUB):
        ref[pl.ds(j, rows, stride=SUB), :] = x[:, j * LANES:(j + 1) * LANES]


def _tiles_to_rows(ref, rows):
    return jnp.concatenate([ref[pl.ds(j, rows, stride=SUB), :] for j in range(SUB)], axis=1)


def _row_tile(ref, r8):
    return ref.at[pl.ds(pl.multiple_of(r8, SUB), SUB)]


def _out_router_kernel(hap_ref, has_ref, hbp_ref, hbs_ref, xp_ref, xs_ref, woa_ref, wob_ref, nf_ref, wr_ref, br_ref,
                       h1_o, u2_o, ids_o, gates_o, rank_o, cnt_o, carry, *, ntp):
    i = pl.program_id(0)

    @pl.when(i == 0)
    def _init():
        carry[...] = jnp.zeros_like(carry)

    ha = _pick(i, ntp, hap_ref, has_ref)
    hb = _pick(i, ntp, hbp_ref, hbs_ref)
    h1 = _pick(i, ntp, xp_ref, xs_ref) + _dot(ha, woa_ref[...]) + _dot(hb, wob_ref[...])
    h1_o[...] = h1
    u2 = _rms(h1, nf_ref[...])
    _rows_to_tiles(u2_o, u2)
    logits = _dot3(u2, wr_ref[...]) + br_ref[...]
    TM = logits.shape[0]
    lane = lax.broadcasted_iota(I32, (TM, LANES), 1)
    lane_f = lane.astype(F32)
    work = logits
    vals, idxs = [], []
    for _ in range(TOP_K):
        m = jnp.max(work, axis=1, keepdims=True)
        idx = jnp.min(jnp.where(work == m, lane_f, float(LANES)), axis=1, keepdims=True)
        vals.append(m)
        idxs.append(idx)
        work = jnp.where(lane_f == idx, -jnp.inf, work)
    es = [jnp.exp(v - vals[0]) for v in vals]
    tot = es[0] + es[1] + es[2] + es[3]
    member = jnp.zeros((TM, LANES), F32)
    ids = jnp.zeros((TM, LANES), F32)
    gates = jnp.zeros((TM, LANES), F32)
    for kk in range(TOP_K):
        member = member + jnp.where(lane_f == idxs[kk], 1.0, 0.0)
        ids = jnp.where(lane == kk, idxs[kk], ids)
        gates = jnp.where(lane == kk, es[kk] / tot, gates)
    r2 = lax.broadcasted_iota(I32, (TM, TM), 0)
    c2 = lax.broadcasted_iota(I32, (TM, TM), 1)
    before = _dot(jnp.where(r2 > c2, 1.0, 0.0).astype(BF16), member.astype(BF16)) + carry[0:1]
    rank = jnp.zeros((TM, LANES), F32)
    for kk in range(TOP_K):
        rk = jnp.sum(jnp.where(lane_f == idxs[kk], before, 0.0), axis=1, keepdims=True)
        rank = jnp.where(lane == kk, rk, rank)
    ids_o[...] = ids.astype(I32)
    gates_o[...] = gates
    rank_o[...] = rank.astype(I32)
    carry[0:1] = carry[0:1] + jnp.sum(member, axis=0, keepdims=True)
    cnt_o[...] = carry[0:1].astype(I32)


def _out_router(hap, has, hbp, hbs, xp, xs, woa, wob, nf, wr, br):
    D = xp.shape[1]
    T = xp.shape[0] + xs.shape[0]
    ntp = xp.shape[0] // ROW_TILE
    nt = T // ROW_TILE
    row = lambda n: pl.BlockSpec((ROW_TILE, n), lambda i: (i, 0))
    full = lambda a: pl.BlockSpec(a.shape, lambda i: (0, 0))
    return pl.pallas_call(
        functools.partial(_out_router_kernel, ntp=ntp),
        grid=(nt,),
        in_specs=_two_specs(ntp, A_WIDTH) + _two_specs(ntp, B_WIDTH) + _two_specs(ntp, D)
        + [full(woa), full(wob), full(nf), full(wr), full(br)],
        out_specs=[row(D), pl.BlockSpec((ROW_TILE * SUB, LANES), lambda i: (i, 0)),
                   row(LANES), row(LANES), row(LANES), pl.BlockSpec((1, LANES), lambda i: (0, 0))],
        out_shape=[jax.ShapeDtypeStruct((T, D), F32), jax.ShapeDtypeStruct((T * SUB, LANES), F32),
                   jax.ShapeDtypeStruct((T, LANES), I32), jax.ShapeDtypeStruct((T, LANES), F32),
                   jax.ShapeDtypeStruct((T, LANES), I32), jax.ShapeDtypeStruct((1, LANES), I32)],
        scratch_shapes=[pltpu.VMEM((8, LANES), F32)],
        compiler_params=_cparams(("arbitrary",)),
        name="out_router",
    )(hap, has, hbp, hbs, xp, xs, woa, wob, nf, wr, br)


MOE_BLOCK = 512


def _row_copy_wait(buf, sem, n):
    for _ in range(n):
        pltpu.make_async_copy(buf, buf, sem).wait()


DMA_UNROLL = 4


def _scatter_kernel(plo_ref, phi_ref, nu_ref, dest_ref, u_ref, xs_o, zbuf, sem):
    TM = u_ref.shape[0] // SUB
    BLK = MOE_BLOCK * SUB

    @pl.when(pl.program_id(0) == 0)
    def _zero_unused_slots():
        zbuf[...] = jnp.zeros_like(zbuf)
        zrow = zbuf.at[pl.ds(0, SUB)]

        def per_expert(e, n):
            def per_row(rw, _):
                pltpu.make_async_copy(zrow, _row_tile(xs_o, rw * SUB), sem).start()
                return 0

            lax.fori_loop(plo_ref[e], phi_ref[e], per_row, 0)
            return n + (phi_ref[e] - plo_ref[e])

        n_rows = lax.fori_loop(0, plo_ref.shape[0], per_expert, 0)

        def wait_row(_, c):
            pltpu.make_async_copy(zrow, xs_o.at[pl.ds(0, SUB)], sem).wait()
            return c

        lax.fori_loop(0, n_rows, wait_row, 0)

        def per_block(j, _):
            cp = pltpu.make_async_copy(zbuf, xs_o.at[pl.ds(pl.multiple_of(j * BLK, BLK), BLK)], sem)
            cp.start()
            cp.wait()
            return 0

        lax.fori_loop(nu_ref[0], xs_o.shape[0] // BLK, per_block, 0)

    def body(i, _):
        src = _row_tile(u_ref, i * SUB)
        for kk in range(TOP_K):
            pltpu.make_async_copy(src, _row_tile(xs_o, dest_ref[i * TOP_K + kk]), sem).start(priority=kk % 2)
        return 0

    lax.fori_loop(0, TM, body, 0, unroll=DMA_UNROLL)
    _row_copy_wait(u_ref, sem, TOP_K)


def _scatter_rows(pad_lo, pad_hi, n_used, dest8_flat, u2t, n_slots):
    T = u2t.shape[0] // SUB
    grid_spec = pltpu.PrefetchScalarGridSpec(
        num_scalar_prefetch=3,
        grid=(T // ROW_TILE,),
        in_specs=[pl.BlockSpec((ROW_TILE * TOP_K,), lambda i, *_: (i,), memory_space=pltpu.SMEM),
                  pl.BlockSpec((ROW_TILE * SUB, LANES), lambda i, *_: (i, 0))],
        out_specs=pl.BlockSpec(memory_space=pl.ANY),
        scratch_shapes=[pltpu.VMEM((MOE_BLOCK * SUB, LANES), F32), pltpu.SemaphoreType.DMA(())],
    )
    return pl.pallas_call(
        _scatter_kernel,
        grid_spec=grid_spec,
        out_shape=jax.ShapeDtypeStruct((n_slots * SUB, LANES), F32),
        compiler_params=_cparams(("arbitrary",)),
        name="moe_scatter",
    )(pad_lo, pad_hi, n_used, dest8_flat, u2t)


def _expert_kernel(be_ref, nu_ref, nxt_ref, xs_ref, wgu_hbm, bgu_ref, wd_hbm, bd_ref, ys_o,
                   wgu_f, wd_f, wgu_b, wd_b, slot_ref, sems):
    j = pl.program_id(0)
    F = wd_b.shape[0]
    used = j < nu_ref[0]
    e = be_ref[j]

    def w_copies(eid, slot):
        return (pltpu.make_async_copy(wgu_hbm.at[eid], wgu_f.at[slot], sems.at[0, slot]),
                pltpu.make_async_copy(wd_hbm.at[eid], wd_f.at[slot], sems.at[1, slot]))

    @pl.when(j == 0)
    def _first_fetch():
        slot_ref[0] = 0
        for cp in w_copies(e, 0):
            cp.start()

    @pl.when(used & ((j == 0) | (e != be_ref[jnp.maximum(j - 1, 0)])))
    def _new_expert():
        slot = slot_ref[0]
        for cp in w_copies(e, slot):
            cp.wait()
        nx = nxt_ref[e]

        @pl.when(nx != e)
        def _prefetch():
            for cp in w_copies(nx, 1 - slot):
                cp.start()

        wgu_b[...] = wgu_f[slot].astype(BF16)
        wd_b[...] = wd_f[slot].astype(BF16)
        slot_ref[0] = 1 - slot

    @pl.when(used)
    def _compute():
        x = _tiles_to_rows(xs_ref, MOE_BLOCK)
        gu = _dot(x.astype(BF16), wgu_b[...]) + bgu_ref[0]
        g_ = jnp.minimum(gu[:, :F], SWIGLU_LIMIT)
        up = jnp.clip(gu[:, F:], -SWIGLU_LIMIT, SWIGLU_LIMIT)
        act = (up + 1.0) * g_ * _sigmoid(SWIGLU_ALPHA * g_)
        _rows_to_tiles(ys_o, _dot(act.astype(BF16), wd_b[...]) + bd_ref[0])

    @pl.when(j >= nu_ref[0])
    def _unused():
        ys_o[...] = jnp.zeros_like(ys_o)


def _experts(blk_e, n_used, nxt_e, xs, wgu, bgu, wd, bd):
    nb = xs.shape[0] // (MOE_BLOCK * SUB)
    E, D, F2 = wgu.shape
    F = F2 // 2
    grid_spec = pltpu.PrefetchScalarGridSpec(
        num_scalar_prefetch=3,
        grid=(nb,),
        in_specs=[pl.BlockSpec((MOE_BLOCK * SUB, LANES), lambda j, be, nu, nx: (jnp.minimum(j, nu[0] - 1), 0)),
                  pl.BlockSpec(memory_space=pl.ANY),
                  pl.BlockSpec((1, 1, F2), lambda j, be, nu, nx: (be[j], 0, 0)),
                  pl.BlockSpec(memory_space=pl.ANY),
                  pl.BlockSpec((1, 1, D), lambda j, be, nu, nx: (be[j], 0, 0))],
        out_specs=pl.BlockSpec((MOE_BLOCK * SUB, LANES), lambda j, be, nu, nx: (j, 0)),
        scratch_shapes=[pltpu.VMEM((2, D, F2), F32), pltpu.VMEM((2, F, D), F32),
                        pltpu.VMEM((D, F2), BF16), pltpu.VMEM((F, D), BF16),
                        pltpu.SMEM((1,), I32), pltpu.SemaphoreType.DMA((2, 2))],
    )
    return pl.pallas_call(
        _expert_kernel,
        grid_spec=grid_spec,
        out_shape=jax.ShapeDtypeStruct(xs.shape, F32),
        compiler_params=_cparams(("arbitrary",)),
        name="moe_experts",
    )(blk_e, n_used, nxt_e, xs, wgu, bgu, wd, bd)


def _combine_kernel(dest0_ref, destn_ref, h1_ref, pp_ref, ps_ref, gates_ref, ys_ref, npl_ref, wpg_ref, wpp_ref, fn_ref,
                    yp_o, ysm_o, gbuf, ybuf, sems, *, ntp, n_tiles):
    TM = h1_ref.shape[0]
    t = pl.program_id(0)

    def fetch(dref, i, slot):
        for kk in range(TOP_K):
            src = _row_tile(ys_ref, dref[i * TOP_K + kk])
            pltpu.make_async_copy(src, _row_tile(gbuf.at[slot, kk], i * SUB), sems.at[slot]).start(priority=kk % 2)

    @pl.when(t == 0)
    def _first_fetch():
        def body(i, _):
            fetch(dest0_ref, i, 0)
            return 0

        lax.fori_loop(0, TM, body, 0, unroll=DMA_UNROLL)

    def tile(slot):
        _row_copy_wait(gbuf.at[slot, 0], sems.at[slot], TOP_K)
        for i in range(TM):
            fetch(destn_ref, i, 1 - slot)
        gates = gates_ref[...]
        h2 = h1_ref[...]
        for kk in range(TOP_K):
            h2 = h2 + _tiles_to_rows(gbuf.at[slot, kk], TM) * gates[:, kk:kk + 1]
        u3 = _rms(h2, npl_ref[...])
        gate = _sigmoid(_dot(u3.astype(BF16), wpg_ref[...]))
        p = _pick(t, ntp, pp_ref, ps_ref)
        h3 = h2 + gate * _dot(p.astype(BF16), wpp_ref[...])
        return _rms(h3, fn_ref[...])

    for slot in range(2):
        @pl.when(t % 2 == slot)
        def _tile():
            ybuf[...] = tile(slot)

    @pl.when(t < ntp)
    def _prompt_rows():
        yp_o[...] = ybuf[...]

    @pl.when(t >= ntp)
    def _sample_rows():
        ysm_o[...] = ybuf[...]

    @pl.when(t == n_tiles - 1)
    def _drain():
        other = n_tiles % 2
        _row_copy_wait(gbuf.at[other, 0], sems.at[other], TOP_K)


def _combine(dest_flat, h1, pp, ps, gates, ys, npl, wpg, wpp, fn):
    T, D = h1.shape
    nt = T // ROW_TILE
    ntp = pp.shape[0] // ROW_TILE
    row = lambda n: pl.BlockSpec((ROW_TILE, n), lambda i: (i, 0))
    full = lambda a: pl.BlockSpec(a.shape, lambda i: (0, 0))
    idx = lambda f: pl.BlockSpec((ROW_TILE * TOP_K,), f, memory_space=pltpu.SMEM)
    return pl.pallas_call(
        functools.partial(_combine_kernel, ntp=ntp, n_tiles=nt),
        grid=(nt,),
        in_specs=[idx(lambda i: (0,)), idx(lambda i: (jnp.minimum(i + 1, nt - 1),)), row(D)]
        + _two_specs(ntp, pp.shape[1])
        + [row(LANES), pl.BlockSpec(memory_space=pl.ANY), full(npl), full(wpg), full(wpp), full(fn)],
        out_specs=_two_specs(ntp, D),
        out_shape=[jax.ShapeDtypeStruct((pp.shape[0], D), F32), jax.ShapeDtypeStruct((ps.shape[0], D), F32)],
        scratch_shapes=[pltpu.VMEM((2, TOP_K, ROW_TILE * SUB, LANES), F32), pltpu.VMEM((ROW_TILE, D), F32),
                        pltpu.SemaphoreType.DMA((2,))],
        compiler_params=_cparams(("arbitrary",)),
        name="moe_combine_ple",
    )(dest_flat, dest_flat, h1, pp, ps, gates, ys, npl, wpg, wpp, fn)


def _pad_lora_cols(x):
    W3 = 3 * B_WIDTH
    z = jnp.zeros(x.shape[:-1] + (LORA_PAD - 64,), x.dtype)
    return jnp.concatenate([x[..., :W3 + 64], z, x[..., W3 + 64:W3 + 128], z, x[..., W3 + 128:]], axis=-1)


def _unpad_lora_cols(x):
    W3 = 3 * B_WIDTH
    return jnp.concatenate([x[..., :W3 + 64], x[..., W3 + LORA_PAD:W3 + LORA_PAD + 64], x[..., W3 + 2 * LORA_PAD:]], axis=-1)


def _heads_to_pairs(S):
    n = S.shape[0]
    S = S.reshape(n, B_HEADS // 2, 2, B_HDIM, B_HDIM)
    z = jnp.zeros_like(S[:, :, 0])
    top = jnp.concatenate([S[:, :, 0], z], axis=-1)
    bot = jnp.concatenate([z, S[:, :, 1]], axis=-1)
    return jnp.concatenate([top, bot], axis=-2)


def _pairs_to_heads(Sb):
    n = Sb.shape[0]
    a = Sb[:, :, :B_HDIM, :B_HDIM]
    b = Sb[:, :, B_HDIM:, B_HDIM:]
    return jnp.stack([a, b], axis=2).reshape(n, B_HEADS, B_HDIM, B_HDIM)


def kernel(x_prompt, x_sample, state_mlstm_C, state_mlstm_n, state_mlstm_m, state_mlstm_conv, state_rwkv_S, state_rwkv_shift, p_prompt, p_sample, norm_mix, w_in, conv_w, conv_b, b_ig, b_fg, a_norm, shift_mu, w0, w_w2, a0, w_a2, w_g2, k_k, k_a, r_k, gn_w, gn_b, w_out, norm_ffn, w_router, b_router, w_gu, b_gu, w_down, b_down, norm_ple, w_ple_gate, w_ple_proj, final_norm):
    assert norm_mix.shape[0] == 1, "single-layer trunk"
    nbp, Lp, D = x_prompt.shape
    nbs, Ls, _ = x_sample.shape
    Tp, Ts = nbp * Lp, nbs * Ls
    T = Tp + Ts
    row = lambda a: a.reshape(1, -1).astype(F32)

    wi = w_in[0]
    AW2 = 2 * A_WIDTH
    wqk = wi[:, :AW2].astype(BF16)
    wv = wi[:, AW2:AW2 + A_WIDTH].astype(BF16)
    wo = wi[:, AW2 + A_WIDTH:AW2 + 2 * A_WIDTH].astype(BF16)
    a_cols = 4 * A_WIDTH + 2 * A_HEADS
    wif = jnp.pad(wi[:, 4 * A_WIDTH:a_cols], ((0, 0), (0, LANES - 2 * A_HEADS)))
    wbg = jnp.concatenate([_pad_lora_cols(wi[:, a_cols:]), wif], axis=1).astype(BF16)
    gate_bias = jnp.pad(jnp.concatenate([b_ig[0], b_fg[0]]), (0, LANES - 2 * A_HEADS)).reshape(1, LANES)
    padw = lambda w: jnp.pad(w, ((0, LORA_PAD - w.shape[0]), (0, 0)))
    hid = jnp.arange(B_WIDTH) // B_HDIM
    bd = (hid[:, None] == hid[None, :]).astype(BF16)
    E = w_router.shape[-1]
    wr = jnp.pad(w_router[0], ((0, 0), (0, LANES - E)))
    br = jnp.pad(b_router[0], (0, LANES - E), constant_values=NEG_BIG).reshape(1, LANES)

    assert Tp % ROW_TILE == 0 and Ts % ROW_TILE == 0
    xp, xsm = x_prompt.reshape(Tp, D), x_sample.reshape(Ts, D)
    qk, v, og, pb = _in_proj(xp, xsm, row(norm_mix), wqk, wv, wo, wbg)
    ifg = pb

    f0 = lambda *shp: jnp.zeros(shp, F32)
    ml_consts = (conv_w[0], row(conv_b), gate_bias, row(a_norm))
    m_s = jnp.pad(state_mlstm_m[0], ((0, 0), (0, LANES - A_HEADS))).reshape(nbs, 1, LANES)
    hap, Cp, np_, mp, convp = _mlstm(qk, v, og, ifg, f0(nbp, A_HEADS, A_HDIM, A_HDIM), f0(nbp, A_HEADS, A_HDIM),
                                     f0(nbp, 1, LANES), f0(nbp, CONV_W - 1, AW2), *ml_consts,
                                     n_seq=nbp, L=Lp, row_off=0)
    has, Cs, ns_, ms, convs = _mlstm(qk, v, og, ifg, state_mlstm_C[0], state_mlstm_n[0], m_s, state_mlstm_conv[0],
                                     *ml_consts, n_seq=nbs, L=Ls, row_off=Tp)

    rw_consts = (row(_pad_lora_cols(shift_mu[0])), row(w0), padw(w_w2[0]).astype(BF16), row(a0), padw(w_a2[0]).astype(BF16),
                 w_g2[0].astype(BF16),
                 row(k_k), row(k_a), row(r_k), row(gn_w), row(gn_b), bd)
    npair = B_HEADS // 2
    hbp, Sp, shp_ = _rwkv(pb, f0(nbp, npair, PAIR, PAIR), f0(nbp, 1, B_COLS_PAD), *rw_consts,
                          n_seq=nbp, L=Lp, row_off=0)
    hbs, Ss, shs = _rwkv(pb, _heads_to_pairs(state_rwkv_S[0]), _pad_lora_cols(state_rwkv_shift[0]), *rw_consts,
                         n_seq=nbs, L=Ls, row_off=Tp)

    wout = w_out[0].astype(BF16)
    h1, u2, ids, gates, rank, counts = _out_router(hap, has, hbp, hbs, xp, xsm, wout[:A_WIDTH], wout[A_WIDTH:],
                                                   row(norm_ffn), wr, br)

    counts = counts[0, :E]
    pcounts = (counts + MOE_BLOCK - 1) // MOE_BLOCK * MOE_BLOCK
    pend = jnp.cumsum(pcounts)
    pstart = pend - pcounts
    n_blk = -(-T * TOP_K // MOE_BLOCK) + E
    dest = ((jnp.take(pstart, ids[:, :TOP_K]) + rank[:, :TOP_K]) * SUB).reshape(-1).astype(I32)
    blk_start = jnp.arange(n_blk, dtype=I32) * MOE_BLOCK
    blk_e = jnp.minimum(jnp.sum(pend[None, :] <= blk_start[:, None], axis=1), E - 1).astype(I32)
    n_used = (pend[E - 1:] // MOE_BLOCK).astype(I32)

    xs = _scatter_rows((pstart + counts).astype(I32), pend.astype(I32), n_used, dest, u2, n_blk * MOE_BLOCK)
    eid = jnp.arange(E, dtype=I32)
    later = jnp.where((eid[None, :] > eid[:, None]) & (pcounts[None, :] > 0), eid[None, :], E)
    nxt_e = jnp.min(later, axis=1)
    nxt_e = jnp.where(nxt_e < E, nxt_e, eid).astype(I32)
    ys = _experts(blk_e, n_used, nxt_e, xs, w_gu[0], b_gu[0][:, None, :], w_down[0], b_down[0][:, None, :])
    yp, ysm = _combine(dest, h1, p_prompt[0].reshape(Tp, -1), p_sample[0].reshape(Ts, -1), gates, ys, row(norm_ple),
                       w_ple_gate[0].astype(BF16), w_ple_proj[0].astype(BF16), row(final_norm))

    lead = lambda a: a[None]
    return (yp.reshape(nbp, Lp, D), ysm.reshape(nbs, Ls, D),
            lead(Cp), lead(np_), lead(mp[:, 0, :A_HEADS]), lead(convp), lead(_pairs_to_heads(Sp)), lead(_unpad_lora_cols(shp_)),
            lead(Cs), lead(ns_), lead(ms[:, 0, :A_HEADS]), lead(convs), lead(_pairs_to_heads(Ss)), lead(_unpad_lora_cols(shs)))
```

```python
import functools

import jax
import jax.numpy as jnp
from jax import lax
from jax.experimental import pallas as pl
from jax.experimental.pallas import tpu as pltpu

F32 = jnp.float32
BF16 = jnp.bfloat16
I32 = jnp.int32

EPS = 1e-6
GN_EPS = 64e-5
A_HEADS, A_HDIM = 4, 128
B_HEADS, B_HDIM = 8, 64
A_WIDTH = A_HEADS * A_HDIM
B_WIDTH = B_HEADS * B_HDIM
CONV_W = 4
LORA_PAD = 128
N_EXPERTS, TOP_K = 32, 4
SWIGLU_LIMIT, SWIGLU_ALPHA = 7.0, 1.702
LANES = 128
ROW_TILE = 256
VMEM_LIMIT = 56 * 1024 * 1024


def _cparams(sem):
    return pltpu.CompilerParams(dimension_semantics=sem, vmem_limit_bytes=VMEM_LIMIT)


def _dot(a, b):
    return lax.dot_general(a, b, (((1,), (0,)), ((), ())), preferred_element_type=F32)


def _dot_nt(a, b):
    return lax.dot_general(a, b, (((1,), (1,)), ((), ())), preferred_element_type=F32)


def _dot_tn(a, b):
    return lax.dot_general(a, b, (((0,), (0,)), ((), ())), preferred_element_type=F32)


def _split2(x):
    hi = x.astype(BF16)
    lo = (x - hi.astype(F32)).astype(BF16)
    return hi, lo


def _split3(x):
    hi = x.astype(BF16)
    r1 = x - hi.astype(F32)
    mid = r1.astype(BF16)
    lo = (r1 - mid.astype(F32)).astype(BF16)
    return hi, mid, lo


def _dot3(a, b, f=_dot):
    ah, al = _split2(a)
    bh, bl = _split2(b)
    return f(ah, bh) + (f(ah, bl) + f(al, bh))


def _dotx_l(a_exact, b, f=_dot):
    b1, b2, b3 = _split3(b)
    return f(a_exact, b1) + (f(a_exact, b2) + f(a_exact, b3))


def _rms(x, g):
    return x * lax.rsqrt(jnp.mean(x * x, axis=-1, keepdims=True) + EPS) * g


def _sigmoid(x):
    return 1.0 / (1.0 + jnp.exp(-x))


def _two_specs(ntp, n):
    return [pl.BlockSpec((ROW_TILE, n), lambda i: (jnp.minimum(i, ntp - 1), 0)),
            pl.BlockSpec((ROW_TILE, n), lambda i: (jnp.maximum(i - ntp, 0), 0))]


def _pick(i, ntp, p_ref, s_ref):
    return jnp.where(i < ntp, p_ref[...], s_ref[...])


def _in_proj_kernel(xp_ref, xs_ref, g_ref, wqk_ref, wv_ref, wo_ref, wbg_ref,
                    qk_o, v_o, og_o, pbg_o, *, ntp):
    x = _pick(pl.program_id(0), ntp, xp_ref, xs_ref)
    ub = _rms(x, g_ref[...]).astype(BF16)
    qk_o[...] = _dot(ub, wqk_ref[...])
    v_o[...] = _dot(ub, wv_ref[...])
    og_o[...] = _dot(ub, wo_ref[...])
    pbg_o[...] = _dot(ub, wbg_ref[...])


def _in_proj(xp, xs, g, wqk, wv, wo, wbg):
    D = xp.shape[1]
    T = xp.shape[0] + xs.shape[0]
    ntp = xp.shape[0] // ROW_TILE
    row = lambda n: pl.BlockSpec((ROW_TILE, n), lambda i: (i, 0))
    full = lambda a: pl.BlockSpec(a.shape, lambda i: (0, 0))
    outs = [wqk.shape[1], wv.shape[1], wo.shape[1], wbg.shape[1]]
    return pl.pallas_call(
        functools.partial(_in_proj_kernel, ntp=ntp),
        grid=(T // ROW_TILE,),
        in_specs=_two_specs(ntp, D) + [full(g), full(wqk), full(wv), full(wo), full(wbg)],
        out_specs=[row(n) for n in outs],
        out_shape=[jax.ShapeDtypeStruct((T, n), F32) for n in outs],
        compiler_params=_cparams(("arbitrary",)),
        name="in_proj",
    )(xp, xs, g, wqk, wv, wo, wbg)


MLSTM_CHUNK = 128
NEG_BIG = -1e30


def _log_sigmoid(x):
    return jnp.minimum(x, 0.0) - jnp.log1p(jnp.exp(-jnp.abs(x)))


def _mlstm_kernel(qk_ref, v_ref, og_ref, if_ref, C0_ref, n0_ref, m0_ref, conv0_ref,
                  cw_ref, cb_ref, gb_ref, an_ref,
                  ha_o, C1_o, n1_o, m1_o, conv1_o, C_sc, n_sc, m_sc, cbuf, *, nv):
    CH = MLSTM_CHUNK
    c = pl.program_id(1)

    @pl.when(c == 0)
    def _init():
        C_sc[...] = C0_ref[0]
        n_sc[0:A_HEADS] = n0_ref[0]
        m_sc[0:1] = m0_ref[0]
        cbuf[5:8] = conv0_ref[0]

    cbuf[8:8 + nv] = qk_ref[...]
    if nv < CH:
        cbuf[8 + nv:8 + CH] = jnp.zeros((CH - nv, 2 * A_WIDTH), F32)
    y = cb_ref[...] + cbuf[5:5 + CH] * cw_ref[0:1]
    for j in range(1, CONV_W):
        y = y + cbuf[5 + j:5 + j + CH] * cw_ref[j:j + 1]
    conv_tail = cbuf[5 + nv:8 + nv]
    qk = y * _sigmoid(y)
    q = qk[:, :A_WIDTH]
    k = qk[:, A_WIDTH:] * (A_HDIM ** -0.5)
    if nv == CH:
        v = v_ref[...]
        og = og_ref[...]
        gp = if_ref[...] + gb_ref[...]
    else:
        zpad = lambda n: jnp.zeros((CH - nv, n), F32)
        v = jnp.concatenate([v_ref[...], zpad(A_WIDTH)], axis=0)
        og = jnp.concatenate([og_ref[...], zpad(A_WIDTH)], axis=0)
        gp = jnp.concatenate([if_ref[...] + gb_ref[...], zpad(LANES)], axis=0)

    row = lax.broadcasted_iota(I32, (CH, LANES), 0)
    li = gp
    lf = pltpu.roll(_log_sigmoid(gp), LANES - A_HEADS, 1)
    if nv < CH:
        li = jnp.where(row < nv, li, NEG_BIG)
        lf = jnp.where(row < nv, lf, 0.0)
    r2 = lax.broadcasted_iota(I32, (CH, CH), 0)
    c2 = lax.broadcasted_iota(I32, (CH, CH), 1)
    tri = r2 >= c2
    tril_b = jnp.where(tri, 1.0, 0.0).astype(BF16)
    b = _dotx_l(tril_b, lf)
    g = li - b
    sel = jnp.where(lax.broadcasted_iota(I32, (8, LANES), 0) == lax.broadcasted_iota(I32, (8, LANES), 1),
                    1.0, 0.0).astype(BF16)
    gT = _dotx_l(sel, g, f=_dot_nt)

    qb, kb, vb = q.astype(BF16), k.astype(BF16), v.astype(BF16)
    lane1 = lax.broadcasted_iota(I32, (1, LANES), 1)
    m_row = m_sc[0:1]
    H = range(A_HEADS)
    sls = [slice(h * A_HDIM, (h + 1) * A_HDIM) for h in H]
    bcol = [b[:, h:h + 1] for h in H]
    m_h = [m_row[:, h:h + 1] for h in H]
    C_h = [C_sc[h] for h in H]
    n_h = [n_sc[h:h + 1] for h in H]
    qk_t = [_dot_nt(qb[:, sl], kb[:, sl]) for sl in sls]
    qC = [_dot_nt(qb[:, sl], C_h[h].astype(BF16)) for h, sl in enumerate(sls)]
    dm = [jnp.where(tri, bcol[h] + gT[h:h + 1, :], -jnp.inf) for h in H]
    inter = [bcol[h] + m_h[h] for h in H]
    mt = [jnp.maximum(inter[h], jnp.max(dm[h], axis=1, keepdims=True)) for h in H]
    s = [qk_t[h] * jnp.exp(dm[h] - mt[h]) for h in H]
    wi = [jnp.exp(inter[h] - mt[h]) for h in H]
    sv = [_dot(s[h].astype(BF16), vb[:, sl]) for h, sl in enumerate(sls)]
    m_new = [mt[h][CH - 1:CH] for h in H]
    w_end = [jnp.exp(bcol[h][CH - 1:CH] + g[:, h:h + 1] - m_new[h]) for h in H]
    dec = [jnp.exp(bcol[h][CH - 1:CH] + m_h[h] - m_new[h]) for h in H]
    vk = [_dot_tn((v[:, sl] * w_end[h]).astype(BF16), kb[:, sl]) for h, sl in enumerate(sls)]
    outs = []
    m_next = m_row
    for h, sl in enumerate(sls):
        num = sv[h] + wi[h] * qC[h]
        den = jnp.sum(s[h], axis=1, keepdims=True) + wi[h] * jnp.sum(q[:, sl] * n_h[h], axis=1, keepdims=True)
        hh = num / jnp.maximum(jnp.abs(den), jnp.exp(-mt[h]))
        hn = hh * lax.rsqrt(jnp.mean(hh * hh, axis=1, keepdims=True) + EPS)
        outs.append(hn * an_ref[:, sl] * _sigmoid(og[:, sl]))
        C_sc[h] = dec[h] * C_h[h] + vk[h]
        n_sc[h:h + 1] = dec[h] * n_h[h] + jnp.sum(k[:, sl] * w_end[h], axis=0, keepdims=True)
        m_next = jnp.where(lane1 == h, m_new[h], m_next)
    m_sc[0:1] = m_next
    ha = jnp.concatenate(outs, axis=1)
    ha_o[...] = ha[:nv].astype(BF16)
    cbuf[5:8] = conv_tail

    @pl.when(c == pl.num_programs(1) - 1)
    def _fin():
        C1_o[0] = C_sc[...]
        n1_o[0] = n_sc[0:A_HEADS]
        m1_o[0] = m_sc[0:1]
        conv1_o[0] = conv_tail


def _mlstm(qk, v, og, pbg, C0, n0, m0, conv0, cw, cb, gb, an, *, n_seq, L, row_off):
    CH = MLSTM_CHUNK
    nv = min(L, CH)
    nch = L // nv
    off = row_off // nv
    gate_tile = pbg.shape[1] // LANES - 1
    tok = lambda n: pl.BlockSpec((nv, n), lambda s, c: (off + s * nch + c, 0))
    st = lambda shp: pl.BlockSpec((1,) + shp, lambda s, c: (s,) + (0,) * len(shp))
    full = lambda a: pl.BlockSpec(a.shape, lambda s, c: (0,) * a.ndim)
    in_specs = [tok(2 * A_WIDTH), tok(A_WIDTH), tok(A_WIDTH),
                pl.BlockSpec((nv, LANES), lambda s, c: (off + s * nch + c, gate_tile)),
                st((A_HEADS, A_HDIM, A_HDIM)), st((A_HEADS, A_HDIM)), st((1, LANES)),
                st((CONV_W - 1, 2 * A_WIDTH)), full(cw), full(cb), full(gb), full(an)]
    args = [qk, v, og, pbg, C0, n0, m0, conv0, cw, cb, gb, an]
    out_shape = [jax.ShapeDtypeStruct((n_seq * L, A_WIDTH), BF16),
                 jax.ShapeDtypeStruct((n_seq, A_HEADS, A_HDIM, A_HDIM), F32),
                 jax.ShapeDtypeStruct((n_seq, A_HEADS, A_HDIM), F32),
                 jax.ShapeDtypeStruct((n_seq, 1, LANES), F32),
                 jax.ShapeDtypeStruct((n_seq, CONV_W - 1, 2 * A_WIDTH), F32)]
    out_specs = [pl.BlockSpec((nv, A_WIDTH), lambda s, c: (s * nch + c, 0)),
                 st((A_HEADS, A_HDIM, A_HDIM)), st((A_HEADS, A_HDIM)),
                 st((1, LANES)), st((CONV_W - 1, 2 * A_WIDTH))]
    return pl.pallas_call(
        functools.partial(_mlstm_kernel, nv=nv),
        grid=(n_seq, nch),
        in_specs=in_specs, out_specs=out_specs, out_shape=out_shape,
        scratch_shapes=[pltpu.VMEM((A_HEADS, A_HDIM, A_HDIM), F32), pltpu.VMEM((8, A_HDIM), F32),
                        pltpu.VMEM((8, LANES), F32), pltpu.VMEM((CH + 8, 2 * A_WIDTH), F32)],
        compiler_params=_cparams(("arbitrary", "arbitrary")),
        name="mlstm",
    )(*args)


RWKV_CHUNK = 64
RWKV_NCHUNK = 4
B_COLS_PAD = 3 * B_WIDTH + 2 * LORA_PAD + 128
PAIR = 2 * B_HDIM


def _softplus(z):
    return jnp.maximum(z, 0.0) + jnp.log1p(jnp.exp(-jnp.abs(z)))


def _rwkv_kernel(pb_ref, S0_ref, sh0_ref, mu_ref, w0_ref, ww2_ref, a0_ref, wa2_ref, wg2_ref,
                 kk_ref, ka_ref, rk_ref, gw_ref, gb_ref, bd_ref,
                 hb_o, S1_o, sh1_o, S_sc, sbuf, *, nv):
    C = RWKV_CHUNK
    nc = -(-nv // C)
    TB = nc * C
    t = pl.program_id(1)

    @pl.when(t == 0)
    def _init():
        S_sc[...] = S0_ref[0]
        sbuf[7:8] = sh0_ref[0]

    sbuf[8:8 + nv] = pb_ref[...]
    if nv < TB:
        sbuf[8 + nv:8 + TB] = jnp.zeros((TB - nv, B_COLS_PAD), F32)
    pb = sbuf[8:8 + TB]
    prev = sbuf[7:7 + TB]
    last_row = sbuf[7 + nv:8 + nv]
    xm = pb + (prev - pb) * mu_ref[...]
    W = B_WIDTH
    r = xm[:, 0:W]
    kb = xm[:, W:2 * W]
    vb = xm[:, 2 * W:3 * W]
    wl = xm[:, 3 * W:3 * W + LORA_PAD]
    al = xm[:, 3 * W + LORA_PAD:3 * W + 2 * LORA_PAD]
    gl = xm[:, 3 * W + 2 * LORA_PAD:]
    seg_sum = lambda z: _dot(z.astype(BF16), bd_ref[...])
    w_log = -_softplus(-(w0_ref[...] + _dot(jnp.tanh(wl).astype(BF16), ww2_ref[...]))) - 0.5
    lw = -jnp.exp(w_log)
    a = _sigmoid(a0_ref[...] + _dot(al.astype(BF16), wa2_ref[...]))
    g = _dot(_sigmoid(gl).astype(BF16), wg2_ref[...])
    kkr = kb * kk_ref[...]
    kk = kkr / jnp.maximum(jnp.sqrt(seg_sum(kkr * kkr)), 1e-12)
    k = kb * (1.0 + (a - 1.0) * ka_ref[...])
    bonus = seg_sum(r * k * rk_ref[...]) * vb
    if nv < TB:
        valid = lax.broadcasted_iota(I32, (TB, W), 0) < nv
        zero = lambda z: jnp.where(valid, z, 0.0)
        lw, kk, k, vb = zero(lw), zero(kk), zero(k), zero(vb)

    lane = lax.broadcasted_iota(I32, (C, PAIR), 1)
    h0 = lane < B_HDIM
    sep = lambda z: jnp.concatenate([jnp.where(h0, z, 0.0), jnp.where(h0, 0.0, z)], axis=0)
    ri = lax.broadcasted_iota(I32, (C, C), 0)
    ci = lax.broadcasted_iota(I32, (C, C), 1)
    tril_c = jnp.where(ri >= ci, 1.0, 0.0).astype(BF16)
    r2 = lax.broadcasted_iota(I32, (PAIR, PAIR), 0)
    c2 = lax.broadcasted_iota(I32, (PAIR, PAIR), 1)
    low_s = r2 > c2
    low_i = r2 >= c2
    eye = jnp.where(r2 == c2, 1.0, 0.0)
    npair = B_HEADS // 2
    units = [(c, p) for c in range(nc) for p in range(npair)]
    cat0 = lambda x, y: jnp.concatenate([x, y], axis=0)
    cat1 = lambda x, y: jnp.concatenate([x, y], axis=1)
    bf = lambda z: z.astype(BF16)
    lhs, rhs_t, vs, wc = [], [], [], []
    for (c, p) in units:
        rs = slice(c * C, (c + 1) * C)
        ls = slice(p * PAIR, (p + 1) * PAIR)
        lwc = lw[rs, ls]
        b = _dotx_l(tril_c, lwc)
        emb = jnp.exp(-b)
        kkc = kk[rs, ls]
        lhs.append(bf(cat0(sep(kkc * jnp.exp(b - lwc)), sep(r[rs, ls] * jnp.exp(b)))))
        rhs_t.append(bf(cat0(sep(k[rs, ls] * emb), sep(kkc * a[rs, ls] * emb))))
        vs.append(bf(sep(vb[rs, ls])))
        wc.append(jnp.exp(b[C - 1:C]))
    A = [_dot_nt(l_, r_) for l_, r_ in zip(lhs, rhs_t)]
    Ak = [jnp.where(low_s, x[:PAIR, :PAIR], 0.0) for x in A]
    Ab = [jnp.where(low_s, x[:PAIR, PAIR:], 0.0) for x in A]
    Ark = [jnp.where(low_i, x[PAIR:, :PAIR], 0.0) for x in A]
    Arb = [jnp.where(low_i, x[PAIR:, PAIR:], 0.0) for x in A]
    X2 = [_dot(bf(cat0(x, y)), v_) for x, y, v_ in zip(Ak, Ark, vs)]
    VtK = [_dot_tn(v_, r_[:PAIR]) for v_, r_ in zip(vs, rhs_t)]
    Tm = [eye - x for x in Ab]
    P = [_dot(bf(x), bf(x)) for x in Ab]
    n = 2
    while 2 * n < C:
        X = [_dot(bf(cat0(t_, p_)), bf(p_)) for t_, p_ in zip(Tm, P)]
        Tm = [t_ + x[:PAIR] for t_, x in zip(Tm, X)]
        P = [x[PAIR:] for x in X]
        n *= 2
    Tm = [t_ + _dot(bf(t_), bf(p_)) for t_, p_ in zip(Tm, P)]
    Y = [_dot(bf(t_), cat1(l_[:PAIR], bf(x[:PAIR]))) for t_, l_, x in zip(Tm, lhs, X2)]
    Z = [_dot(bf(x), bf(y)) for x, y in zip(Arb, Y)]
    Rp = [bf(l_[PAIR:].astype(F32) - z[:, :PAIR]) for l_, z in zip(lhs, Z)]
    Op = [x[PAIR:] - z[:, PAIR:] for x, z in zip(X2, Z)]
    W = [_dot_tn(bf(y), r_[PAIR:]) for y, r_ in zip(Y, rhs_t)]
    Mc = [bf(w_[:PAIR] * c_) for w_, c_ in zip(W, wc)]
    Nn = [(v_ - w_[PAIR:]) * c_ for v_, w_, c_ in zip(VtK, W, wc)]
    o_rows = []
    for c in range(nc):
        S = [S_sc[p] for p in range(npair)]
        Sb = [bf(s_) for s_ in S]
        u0 = c * npair
        O = [_dot_nt(Rp[u0 + p], Sb[p]) + Op[u0 + p] for p in range(npair)]
        for p in range(npair):
            S_sc[p] = S[p] * wc[u0 + p] - _dot(Sb[p], Mc[u0 + p]) + Nn[u0 + p]
        o_rows.append(jnp.concatenate([x[:C] + x[C:] for x in O], axis=1))
    o = o_rows[0] if nc == 1 else jnp.concatenate(o_rows, axis=0)

    mean = seg_sum(o) * (1.0 / B_HDIM)
    d = o - mean
    var = seg_sum(d * d) * (1.0 / B_HDIM)
    hb = (d * lax.rsqrt(var + GN_EPS) * gw_ref[...] + gb_ref[...] + bonus) * g
    hb_o[...] = hb[:nv].astype(BF16)
    sbuf[7:8] = last_row

    @pl.when(t == pl.num_programs(1) - 1)
    def _fin():
        S1_o[0] = S_sc[...]
        sh1_o[0] = last_row


def _rwkv(pb, S0, sh0, mu, w0, ww2, a0, wa2, wg2, k_k, k_a, r_k, gn_w, gn_b, bd, *, n_seq, L, row_off):
    nv = min(L, RWKV_CHUNK * RWKV_NCHUNK)
    nst = L // nv
    off = row_off // nv
    TB = -(-nv // RWKV_CHUNK) * RWKV_CHUNK
    tok = lambda n: pl.BlockSpec((nv, n), lambda s, t: (off + s * nst + t, 0))
    st = lambda shp: pl.BlockSpec((1,) + shp, lambda s, t: (s,) + (0,) * len(shp))
    full = lambda a: pl.BlockSpec(a.shape, lambda s, t: (0,) * a.ndim)
    consts = [mu, w0, ww2, a0, wa2, wg2, k_k, k_a, r_k, gn_w, gn_b, bd]
    npair = B_HEADS // 2
    in_specs = [tok(B_COLS_PAD), st((npair, PAIR, PAIR)), st((1, B_COLS_PAD))] + [full(a) for a in consts]
    args = [pb, S0, sh0] + consts
    out_shape = [jax.ShapeDtypeStruct((n_seq * L, B_WIDTH), BF16),
                 jax.ShapeDtypeStruct((n_seq, npair, PAIR, PAIR), F32),
                 jax.ShapeDtypeStruct((n_seq, 1, B_COLS_PAD), F32)]
    out_specs = [pl.BlockSpec((nv, B_WIDTH), lambda s, t: (s * nst + t, 0)),
                 st((npair, PAIR, PAIR)), st((1, B_COLS_PAD))]
    return pl.pallas_call(
        functools.partial(_rwkv_kernel, nv=nv),
        grid=(n_seq, nst),
        in_specs=in_specs, out_specs=out_specs, out_shape=out_shape,
        scratch_shapes=[pltpu.VMEM((npair, PAIR, PAIR), F32), pltpu.VMEM((TB + 8, B_COLS_PAD), F32)],
        compiler_params=_cparams(("arbitrary", "arbitrary")),
        name="rwkv",
    )(*args)


SUB = 8


def _rows_to_tiles(ref, x):
    rows = x.shape[0]
    for j in range(SUB):
        ref[pl.ds(j, rows, stride=SUB), :] = x[:, j * LANES:(j + 1) * LANES]


def _tiles_to_rows(ref, rows):
    return jnp.concatenate([ref[pl.ds(j, rows, stride=SUB), :] for j in range(SUB)], axis=1)


def _row_tile(ref, r8):
    return ref.at[pl.ds(pl.multiple_of(r8, SUB), SUB)]


def _out_router_kernel(hap_ref, has_ref, hbp_ref, hbs_ref, xp_ref, xs_ref, woa_ref, wob_ref, nf_ref, wr_ref, br_ref,
                       h1_o, u2_o, ids_o, gates_o, rank_o, cnt_o, carry, *, ntp):
    i = pl.program_id(0)

    @pl.when(i == 0)
    def _init():
        carry[...] = jnp.zeros_like(carry)

    ha = _pick(i, ntp, hap_ref, has_ref)
    hb = _pick(i, ntp, hbp_ref, hbs_ref)
    h1 = _pick(i, ntp, xp_ref, xs_ref) + _dot(ha, woa_ref[...]) + _dot(hb, wob_ref[...])
    h1_o[...] = h1
    u2 = _rms(h1, nf_ref[...])
    _rows_to_tiles(u2_o, u2)
    E = br_ref.shape[0]
    logits = _dot_nt(wr_ref[...], u2.astype(BF16)) + br_ref[...]
    TM = logits.shape[1]
    row_f = lax.broadcasted_iota(I32, (E, TM), 0).astype(F32)
    work = logits
    vals, idxs = [], []
    for _ in range(TOP_K):
        m = jnp.max(work, axis=0, keepdims=True)
        idx = jnp.min(jnp.where(work == m, row_f, float(E)), axis=0, keepdims=True)
        vals.append(m)
        idxs.append(idx)
        work = jnp.where(row_f == idx, -jnp.inf, work)
    es = [jnp.exp(v - vals[0]) for v in vals]
    tot = es[0] + es[1] + es[2] + es[3]
    gates = [e_ / tot for e_ in es]
    member = jnp.zeros((E, TM), F32)
    for kk in range(TOP_K):
        member = member + jnp.where(row_f == idxs[kk], 1.0, 0.0)
    r2 = lax.broadcasted_iota(I32, (TM, TM), 0)
    c2 = lax.broadcasted_iota(I32, (TM, TM), 1)
    before = _dot(member.astype(BF16), jnp.where(r2 < c2, 1.0, 0.0).astype(BF16)) + carry[:, 0:1]
    ranks = [jnp.sum(jnp.where(row_f == idxs[kk], before, 0.0), axis=0, keepdims=True) for kk in range(TOP_K)]
    pad = jnp.zeros((SUB - TOP_K, TM), F32)
    rows8 = lambda parts: jnp.concatenate(parts + [pad], axis=0)
    ids_o[...] = rows8(idxs).astype(I32)
    rank_o[...] = rows8(ranks).astype(I32)
    g8 = rows8(gates)
    sel = jnp.where(lax.broadcasted_iota(I32, (SUB, LANES), 0) == lax.broadcasted_iota(I32, (SUB, LANES), 1),
                    1.0, 0.0).astype(BF16)
    gates_o[...] = _dotx_l(sel, g8, f=lambda a, b: _dot_tn(b, a))
    carry[...] = carry[...] + jnp.sum(member, axis=1, keepdims=True)
    cnt_o[...] = carry[...].astype(I32)


def _out_router(hap, has, hbp, hbs, xp, xs, woa, wob, nf, wr, br):
    D = xp.shape[1]
    T = xp.shape[0] + xs.shape[0]
    ntp = xp.shape[0] // ROW_TILE
    nt = T // ROW_TILE
    row = lambda n: pl.BlockSpec((ROW_TILE, n), lambda i: (i, 0))
    full = lambda a: pl.BlockSpec(a.shape, lambda i: (0, 0))
    cnt = jax.ShapeDtypeStruct((br.shape[0], LANES), I32)
    return pl.pallas_call(
        functools.partial(_out_router_kernel, ntp=ntp),
        grid=(nt,),
        in_specs=_two_specs(ntp, A_WIDTH) + _two_specs(ntp, B_WIDTH) + _two_specs(ntp, D)
        + [full(woa), full(wob), full(nf), full(wr), full(br)],
        out_specs=[row(D), pl.BlockSpec((ROW_TILE * SUB, LANES), lambda i: (i, 0)),
                   pl.BlockSpec((SUB, ROW_TILE), lambda i: (0, i)), row(LANES),
                   pl.BlockSpec((SUB, ROW_TILE), lambda i: (0, i)), full(cnt)],
        out_shape=[jax.ShapeDtypeStruct((T, D), F32), jax.ShapeDtypeStruct((T * SUB, LANES), F32),
                   jax.ShapeDtypeStruct((SUB, T), I32), jax.ShapeDtypeStruct((T, LANES), F32),
                   jax.ShapeDtypeStruct((SUB, T), I32), cnt],
        scratch_shapes=[pltpu.VMEM(cnt.shape, F32)],
        compiler_params=_cparams(("arbitrary",)),
        name="out_router",
    )(hap, has, hbp, hbs, xp, xs, woa, wob, nf, wr, br)


MOE_BLOCK = 512


def _row_copy_wait(buf, sem, n):
    for _ in range(n):
        pltpu.make_async_copy(buf, buf, sem).wait()


DMA_UNROLL = 4


def _scatter_kernel(plo_ref, phi_ref, nu_ref, dest_ref, u_ref, xs_o, zbuf, sem):
    TM = u_ref.shape[0] // SUB
    BLK = MOE_BLOCK * SUB

    @pl.when(pl.program_id(0) == 0)
    def _zero_unused_slots():
        zbuf[...] = jnp.zeros_like(zbuf)
        zrow = zbuf.at[pl.ds(0, SUB)]

        def per_expert(e, n):
            def per_row(rw, _):
                pltpu.make_async_copy(zrow, _row_tile(xs_o, rw * SUB), sem).start()
                return 0

            lax.fori_loop(plo_ref[e], phi_ref[e], per_row, 0)
            return n + (phi_ref[e] - plo_ref[e])

        n_rows = lax.fori_loop(0, plo_ref.shape[0], per_expert, 0)

        def wait_row(_, c):
            pltpu.make_async_copy(zrow, xs_o.at[pl.ds(0, SUB)], sem).wait()
            return c

        lax.fori_loop(0, n_rows, wait_row, 0)

        def per_block(j, _):
            cp = pltpu.make_async_copy(zbuf, xs_o.at[pl.ds(pl.multiple_of(j * BLK, BLK), BLK)], sem)
            cp.start()
            cp.wait()
            return 0

        lax.fori_loop(nu_ref[0], xs_o.shape[0] // BLK, per_block, 0)

    def body(i, _):
        src = _row_tile(u_ref, i * SUB)
        for kk in range(TOP_K):
            pltpu.make_async_copy(src, _row_tile(xs_o, dest_ref[i * TOP_K + kk]), sem).start(priority=kk % 2)
        return 0

    lax.fori_loop(0, TM, body, 0, unroll=DMA_UNROLL)
    _row_copy_wait(u_ref, sem, TOP_K)


def _scatter_rows(pad_lo, pad_hi, n_used, dest8_flat, u2t, n_slots):
    T = u2t.shape[0] // SUB
    grid_spec = pltpu.PrefetchScalarGridSpec(
        num_scalar_prefetch=3,
        grid=(T // ROW_TILE,),
        in_specs=[pl.BlockSpec((ROW_TILE * TOP_K,), lambda i, *_: (i,), memory_space=pltpu.SMEM),
                  pl.BlockSpec((ROW_TILE * SUB, LANES), lambda i, *_: (i, 0))],
        out_specs=pl.BlockSpec(memory_space=pl.ANY),
        scratch_shapes=[pltpu.VMEM((MOE_BLOCK * SUB, LANES), F32), pltpu.SemaphoreType.DMA(())],
    )
    return pl.pallas_call(
        _scatter_kernel,
        grid_spec=grid_spec,
        out_shape=jax.ShapeDtypeStruct((n_slots * SUB, LANES), F32),
        compiler_params=_cparams(("arbitrary",)),
        name="moe_scatter",
    )(pad_lo, pad_hi, n_used, dest8_flat, u2t)


def _expert_kernel(be_ref, nu_ref, nxt_ref, xs_ref, wgu_hbm, bgu_ref, wd_hbm, bd_ref, ys_o,
                   wgu_f, wd_f, wgu_b, wd_b, slot_ref, sems):
    j = pl.program_id(0)
    F = wd_b.shape[0]
    used = j < nu_ref[0]
    e = be_ref[j]

    def w_copies(eid, slot):
        return (pltpu.make_async_copy(wgu_hbm.at[eid], wgu_f.at[slot], sems.at[0, slot]),
                pltpu.make_async_copy(wd_hbm.at[eid], wd_f.at[slot], sems.at[1, slot]))

    @pl.when(j == 0)
    def _first_fetch():
        slot_ref[0] = 0
        for cp in w_copies(e, 0):
            cp.start()

    @pl.when(used & ((j == 0) | (e != be_ref[jnp.maximum(j - 1, 0)])))
    def _new_expert():
        slot = slot_ref[0]
        for cp in w_copies(e, slot):
            cp.wait()
        nx = nxt_ref[e]

        @pl.when(nx != e)
        def _prefetch():
            for cp in w_copies(nx, 1 - slot):
                cp.start()

        wgu_b[...] = wgu_f[slot].astype(BF16)
        wd_b[...] = wd_f[slot].astype(BF16)
        slot_ref[0] = 1 - slot

    @pl.when(used)
    def _compute():
        x = _tiles_to_rows(xs_ref, MOE_BLOCK)
        gu = _dot(x.astype(BF16), wgu_b[...]) + bgu_ref[0]
        g_ = jnp.minimum(gu[:, :F], SWIGLU_LIMIT)
        up = jnp.clip(gu[:, F:], -SWIGLU_LIMIT, SWIGLU_LIMIT)
        act = (up + 1.0) * g_ * _sigmoid(SWIGLU_ALPHA * g_)
        _rows_to_tiles(ys_o, _dot(act.astype(BF16), wd_b[...]) + bd_ref[0])

    @pl.when(j >= nu_ref[0])
    def _unused():
        ys_o[...] = jnp.zeros_like(ys_o)


def _experts(blk_e, n_used, nxt_e, xs, wgu, bgu, wd, bd):
    nb = xs.shape[0] // (MOE_BLOCK * SUB)
    E, D, F2 = wgu.shape
    F = F2 // 2
    grid_spec = pltpu.PrefetchScalarGridSpec(
        num_scalar_prefetch=3,
        grid=(nb,),
        in_specs=[pl.BlockSpec((MOE_BLOCK * SUB, LANES), lambda j, be, nu, nx: (jnp.minimum(j, nu[0] - 1), 0)),
                  pl.BlockSpec(memory_space=pl.ANY),
                  pl.BlockSpec((1, 1, F2), lambda j, be, nu, nx: (be[j], 0, 0)),
                  pl.BlockSpec(memory_space=pl.ANY),
                  pl.BlockSpec((1, 1, D), lambda j, be, nu, nx: (be[j], 0, 0))],
        out_specs=pl.BlockSpec((MOE_BLOCK * SUB, LANES), lambda j, be, nu, nx: (j, 0)),
        scratch_shapes=[pltpu.VMEM((2, D, F2), F32), pltpu.VMEM((2, F, D), F32),
                        pltpu.VMEM((D, F2), BF16), pltpu.VMEM((F, D), BF16),
                        pltpu.SMEM((1,), I32), pltpu.SemaphoreType.DMA((2, 2))],
    )
    return pl.pallas_call(
        _expert_kernel,
        grid_spec=grid_spec,
        out_shape=jax.ShapeDtypeStruct(xs.shape, F32),
        compiler_params=_cparams(("arbitrary",)),
        name="moe_experts",
    )(blk_e, n_used, nxt_e, xs, wgu, bgu, wd, bd)


def _combine_kernel(dest0_ref, destn_ref, h1_ref, pp_ref, ps_ref, gates_ref, ys_ref, npl_ref, wpg_ref, wpp_ref, fn_ref,
                    yp_o, ysm_o, gbuf, ybuf, sems, *, ntp, n_tiles):
    TM = h1_ref.shape[0]
    t = pl.program_id(0)

    def fetch(dref, i, slot):
        for kk in range(TOP_K):
            src = _row_tile(ys_ref, dref[i * TOP_K + kk])
            pltpu.make_async_copy(src, _row_tile(gbuf.at[slot, kk], i * SUB), sems.at[slot]).start(priority=kk % 2)

    @pl.when(t == 0)
    def _first_fetch():
        def body(i, _):
            fetch(dest0_ref, i, 0)
            return 0

        lax.fori_loop(0, TM, body, 0, unroll=DMA_UNROLL)

    def tile(slot):
        _row_copy_wait(gbuf.at[slot, 0], sems.at[slot], TOP_K)
        for i in range(TM):
            fetch(destn_ref, i, 1 - slot)
        gates = gates_ref[...]
        h2 = h1_ref[...]
        for kk in range(TOP_K):
            h2 = h2 + _tiles_to_rows(gbuf.at[slot, kk], TM) * gates[:, kk:kk + 1]
        u3 = _rms(h2, npl_ref[...])
        gate = _sigmoid(_dot(u3.astype(BF16), wpg_ref[...]))
        p = _pick(t, ntp, pp_ref, ps_ref)
        h3 = h2 + gate * _dot(p.astype(BF16), wpp_ref[...])
        return _rms(h3, fn_ref[...])

    for slot in range(2):
        @pl.when(t % 2 == slot)
        def _tile():
            ybuf[...] = tile(slot)

    @pl.when(t < ntp)
    def _prompt_rows():
        yp_o[...] = ybuf[...]

    @pl.when(t >= ntp)
    def _sample_rows():
        ysm_o[...] = ybuf[...]

    @pl.when(t == n_tiles - 1)
    def _drain():
        other = n_tiles % 2
        _row_copy_wait(gbuf.at[other, 0], sems.at[other], TOP_K)


def _combine(dest_flat, h1, pp, ps, gates, ys, npl, wpg, wpp, fn):
    T, D = h1.shape
    nt = T // ROW_TILE
    ntp = pp.shape[0] // ROW_TILE
    row = lambda n: pl.BlockSpec((ROW_TILE, n), lambda i: (i, 0))
    full = lambda a: pl.BlockSpec(a.shape, lambda i: (0, 0))
    idx = lambda f: pl.BlockSpec((ROW_TILE * TOP_K,), f, memory_space=pltpu.SMEM)
    return pl.pallas_call(
        functools.partial(_combine_kernel, ntp=ntp, n_tiles=nt),
        grid=(nt,),
        in_specs=[idx(lambda i: (0,)), idx(lambda i: (jnp.minimum(i + 1, nt - 1),)), row(D)]
        + _two_specs(ntp, pp.shape[1])
        + [row(LANES), pl.BlockSpec(memory_space=pl.ANY), full(npl), full(wpg), full(wpp), full(fn)],
        out_specs=_two_specs(ntp, D),
        out_shape=[jax.ShapeDtypeStruct((pp.shape[0], D), F32), jax.ShapeDtypeStruct((ps.shape[0], D), F32)],
        scratch_shapes=[pltpu.VMEM((2, TOP_K, ROW_TILE * SUB, LANES), F32), pltpu.VMEM((ROW_TILE, D), F32),
                        pltpu.SemaphoreType.DMA((2,))],
        compiler_params=_cparams(("arbitrary",)),
        name="moe_combine_ple",
    )(dest_flat, dest_flat, h1, pp, ps, gates, ys, npl, wpg, wpp, fn)


def _pad_lora_cols(x):
    W3 = 3 * B_WIDTH
    z = jnp.zeros(x.shape[:-1] + (LORA_PAD - 64,), x.dtype)
    return jnp.concatenate([x[..., :W3 + 64], z, x[..., W3 + 64:W3 + 128], z, x[..., W3 + 128:]], axis=-1)


def _unpad_lora_cols(x):
    W3 = 3 * B_WIDTH
    return jnp.concatenate([x[..., :W3 + 64], x[..., W3 + LORA_PAD:W3 + LORA_PAD + 64], x[..., W3 + 2 * LORA_PAD:]], axis=-1)


def _heads_to_pairs(S):
    n = S.shape[0]
    S = S.reshape(n, B_HEADS // 2, 2, B_HDIM, B_HDIM)
    z = jnp.zeros_like(S[:, :, 0])
    top = jnp.concatenate([S[:, :, 0], z], axis=-1)
    bot = jnp.concatenate([z, S[:, :, 1]], axis=-1)
    return jnp.concatenate([top, bot], axis=-2)


def _pairs_to_heads(Sb):
    n = Sb.shape[0]
    a = Sb[:, :, :B_HDIM, :B_HDIM]
    b = Sb[:, :, B_HDIM:, B_HDIM:]
    return jnp.stack([a, b], axis=2).reshape(n, B_HEADS, B_HDIM, B_HDIM)


def kernel(x_prompt, x_sample, state_mlstm_C, state_mlstm_n, state_mlstm_m, state_mlstm_conv, state_rwkv_S, state_rwkv_shift, p_prompt, p_sample, norm_mix, w_in, conv_w, conv_b, b_ig, b_fg, a_norm, shift_mu, w0, w_w2, a0, w_a2, w_g2, k_k, k_a, r_k, gn_w, gn_b, w_out, norm_ffn, w_router, b_router, w_gu, b_gu, w_down, b_down, norm_ple, w_ple_gate, w_ple_proj, final_norm):
    assert norm_mix.shape[0] == 1, "single-layer trunk"
    nbp, Lp, D = x_prompt.shape
    nbs, Ls, _ = x_sample.shape
    Tp, Ts = nbp * Lp, nbs * Ls
    T = Tp + Ts
    row = lambda a: a.reshape(1, -1).astype(F32)

    wi = w_in[0]
    AW2 = 2 * A_WIDTH
    wqk = wi[:, :AW2].astype(BF16)
    wv = wi[:, AW2:AW2 + A_WIDTH].astype(BF16)
    wo = wi[:, AW2 + A_WIDTH:AW2 + 2 * A_WIDTH].astype(BF16)
    a_cols = 4 * A_WIDTH + 2 * A_HEADS
    wif = jnp.pad(wi[:, 4 * A_WIDTH:a_cols], ((0, 0), (0, LANES - 2 * A_HEADS)))
    wbg = jnp.concatenate([_pad_lora_cols(wi[:, a_cols:]), wif], axis=1).astype(BF16)
    gate_bias = jnp.pad(jnp.concatenate([b_ig[0], b_fg[0]]), (0, LANES - 2 * A_HEADS)).reshape(1, LANES)
    padw = lambda w: jnp.pad(w, ((0, LORA_PAD - w.shape[0]), (0, 0)))
    hid = jnp.arange(B_WIDTH) // B_HDIM
    bd = (hid[:, None] == hid[None, :]).astype(BF16)
    E = w_router.shape[-1]
    wr = w_router[0].T.astype(BF16)
    br = b_router[0].reshape(E, 1)

    assert Tp % ROW_TILE == 0 and Ts % ROW_TILE == 0
    xp, xsm = x_prompt.reshape(Tp, D), x_sample.reshape(Ts, D)
    qk, v, og, pb = _in_proj(xp, xsm, row(norm_mix), wqk, wv, wo, wbg)
    ifg = pb

    f0 = lambda *shp: jnp.zeros(shp, F32)
    ml_consts = (conv_w[0], row(conv_b), gate_bias, row(a_norm))
    m_s = jnp.pad(state_mlstm_m[0], ((0, 0), (0, LANES - A_HEADS))).reshape(nbs, 1, LANES)
    hap, Cp, np_, mp, convp = _mlstm(qk, v, og, ifg, f0(nbp, A_HEADS, A_HDIM, A_HDIM), f0(nbp, A_HEADS, A_HDIM),
                                     f0(nbp, 1, LANES), f0(nbp, CONV_W - 1, AW2), *ml_consts,
                                     n_seq=nbp, L=Lp, row_off=0)
    has, Cs, ns_, ms, convs = _mlstm(qk, v, og, ifg, state_mlstm_C[0], state_mlstm_n[0], m_s, state_mlstm_conv[0],
                                     *ml_consts, n_seq=nbs, L=Ls, row_off=Tp)

    rw_consts = (row(_pad_lora_cols(shift_mu[0])), row(w0), padw(w_w2[0]).astype(BF16), row(a0), padw(w_a2[0]).astype(BF16),
                 w_g2[0].astype(BF16),
                 row(k_k), row(k_a), row(r_k), row(gn_w), row(gn_b), bd)
    npair = B_HEADS // 2
    hbp, Sp, shp_ = _rwkv(pb, f0(nbp, npair, PAIR, PAIR), f0(nbp, 1, B_COLS_PAD), *rw_consts,
                          n_seq=nbp, L=Lp, row_off=0)
    hbs, Ss, shs = _rwkv(pb, _heads_to_pairs(state_rwkv_S[0]), _pad_lora_cols(state_rwkv_shift[0]), *rw_consts,
                         n_seq=nbs, L=Ls, row_off=Tp)

    wout = w_out[0].astype(BF16)
    h1, u2, ids, gates, rank, counts = _out_router(hap, has, hbp, hbs, xp, xsm, wout[:A_WIDTH], wout[A_WIDTH:],
                                                   row(norm_ffn), wr, br)

    counts = counts[:, 0]
    pcounts = (counts + MOE_BLOCK - 1) // MOE_BLOCK * MOE_BLOCK
    pend = jnp.cumsum(pcounts)
    pstart = pend - pcounts
    n_blk = -(-T * TOP_K // MOE_BLOCK) + E
    dest = ((jnp.take(pstart, ids[:TOP_K]) + rank[:TOP_K]) * SUB).T.reshape(-1).astype(I32)
    blk_start = jnp.arange(n_blk, dtype=I32) * MOE_BLOCK
    blk_e = jnp.minimum(jnp.sum(pend[None, :] <= blk_start[:, None], axis=1), E - 1).astype(I32)
    n_used = (pend[E - 1:] // MOE_BLOCK).astype(I32)

    xs = _scatter_rows((pstart + counts).astype(I32), pend.astype(I32), n_used, dest, u2, n_blk * MOE_BLOCK)
    eid = jnp.arange(E, dtype=I32)
    later = jnp.where((eid[None, :] > eid[:, None]) & (pcounts[None, :] > 0), eid[None, :], E)
    nxt_e = jnp.min(later, axis=1)
    nxt_e = jnp.where(nxt_e < E, nxt_e, eid).astype(I32)
    ys = _experts(blk_e, n_used, nxt_e, xs, w_gu[0], b_gu[0][:, None, :], w_down[0], b_down[0][:, None, :])
    yp, ysm = _combine(dest, h1, p_prompt[0].reshape(Tp, -1), p_sample[0].reshape(Ts, -1), gates, ys, row(norm_ple),
                       w_ple_gate[0].astype(BF16), w_ple_proj[0].astype(BF16), row(final_norm))

    lead = lambda a: a[None]
    return (yp.reshape(nbp, Lp, D), ysm.reshape(nbs, Ls, D),
            lead(Cp), lead(np_), lead(mp[:, 0, :A_HEADS]), lead(convp), lead(_pairs_to_heads(Sp)), lead(_unpad_lora_cols(shp_)),
            lead(Cs), lead(ns_), lead(ms[:, 0, :A_HEADS]), lead(convs), lead(_pairs_to_heads(Ss)), lead(_unpad_lora_cols(shs)))
```

```python
import functools

import jax
import jax.numpy as jnp
from jax import lax
from jax.experimental import pallas as pl
from jax.experimental.pallas import tpu as pltpu

F32 = jnp.float32
BF16 = jnp.bfloat16
I32 = jnp.int32

EPS = 1e-6
GN_EPS = 64e-5
A_HEADS, A_HDIM = 4, 128
B_HEADS, B_HDIM = 8, 64
A_WIDTH = A_HEADS * A_HDIM
B_WIDTH = B_HEADS * B_HDIM
CONV_W = 4
LORA_PAD = 128
N_EXPERTS, TOP_K = 32, 4
SWIGLU_LIMIT, SWIGLU_ALPHA = 7.0, 1.702
LANES = 128
ROW_TILE = 256
VMEM_LIMIT = 56 * 1024 * 1024


def _cparams(sem):
    return pltpu.CompilerParams(dimension_semantics=sem, vmem_limit_bytes=VMEM_LIMIT)


def _dot(a, b):
    return lax.dot_general(a, b, (((1,), (0,)), ((), ())), preferred_element_type=F32)


def _dot_nt(a, b):
    return lax.dot_general(a, b, (((1,), (1,)), ((), ())), preferred_element_type=F32)


def _dot_tn(a, b):
    return lax.dot_general(a, b, (((0,), (0,)), ((), ())), preferred_element_type=F32)


def _split2(x):
    hi = x.astype(BF16)
    lo = (x - hi.astype(F32)).astype(BF16)
    return hi, lo


def _split3(x):
    hi = x.astype(BF16)
    r1 = x - hi.astype(F32)
    mid = r1.astype(BF16)
    lo = (r1 - mid.astype(F32)).astype(BF16)
    return hi, mid, lo


def _dot3(a, b, f=_dot):
    ah, al = _split2(a)
    bh, bl = _split2(b)
    return f(ah, bh) + (f(ah, bl) + f(al, bh))


def _dotx_l(a_exact, b, f=_dot):
    b1, b2, b3 = _split3(b)
    return f(a_exact, b1) + (f(a_exact, b2) + f(a_exact, b3))


def _rms(x, g):
    return x * lax.rsqrt(jnp.mean(x * x, axis=-1, keepdims=True) + EPS) * g


def _sigmoid(x):
    return 1.0 / (1.0 + jnp.exp(-x))


def _two_specs(ntp, n):
    return [pl.BlockSpec((ROW_TILE, n), lambda i: (jnp.minimum(i, ntp - 1), 0)),
            pl.BlockSpec((ROW_TILE, n), lambda i: (jnp.maximum(i - ntp, 0), 0))]


def _pick(i, ntp, p_ref, s_ref):
    return jnp.where(i < ntp, p_ref[...], s_ref[...])


def _in_proj_kernel(xp_ref, xs_ref, g_ref, wqk_ref, wv_ref, wo_ref, wbg_ref,
                    qk_o, v_o, og_o, pbg_o, *, ntp):
    x = _pick(pl.program_id(0), ntp, xp_ref, xs_ref)
    ub = _rms(x, g_ref[...]).astype(BF16)
    qk_o[...] = _dot(ub, wqk_ref[...])
    v_o[...] = _dot(ub, wv_ref[...])
    og_o[...] = _dot(ub, wo_ref[...])
    pbg_o[...] = _dot(ub, wbg_ref[...])


def _in_proj(xp, xs, g, wqk, wv, wo, wbg):
    D = xp.shape[1]
    T = xp.shape[0] + xs.shape[0]
    ntp = xp.shape[0] // ROW_TILE
    row = lambda n: pl.BlockSpec((ROW_TILE, n), lambda i: (i, 0))
    full = lambda a: pl.BlockSpec(a.shape, lambda i: (0, 0))
    outs = [wqk.shape[1], wv.shape[1], wo.shape[1], wbg.shape[1]]
    return pl.pallas_call(
        functools.partial(_in_proj_kernel, ntp=ntp),
        grid=(T // ROW_TILE,),
        in_specs=_two_specs(ntp, D) + [full(g), full(wqk), full(wv), full(wo), full(wbg)],
        out_specs=[row(n) for n in outs],
        out_shape=[jax.ShapeDtypeStruct((T, n), F32) for n in outs],
        compiler_params=_cparams(("arbitrary",)),
        name="in_proj",
    )(xp, xs, g, wqk, wv, wo, wbg)


MLSTM_CHUNK = 128
NEG_BIG = -1e30


def _log_sigmoid(x):
    return jnp.minimum(x, 0.0) - jnp.log1p(jnp.exp(-jnp.abs(x)))


def _mlstm_kernel(qk_ref, v_ref, og_ref, if_ref, C0_ref, n0_ref, m0_ref, conv0_ref,
                  cw_ref, cb_ref, gb_ref, an_ref,
                  ha_o, C1_o, n1_o, m1_o, conv1_o, C_sc, n_sc, m_sc, cbuf, *, nv):
    CH = MLSTM_CHUNK
    c = pl.program_id(1)

    @pl.when(c == 0)
    def _init():
        C_sc[...] = C0_ref[0]
        n_sc[0:A_HEADS] = n0_ref[0]
        m_sc[0:1] = m0_ref[0]
        cbuf[5:8] = conv0_ref[0]

    cbuf[8:8 + nv] = qk_ref[...]
    if nv < CH:
        cbuf[8 + nv:8 + CH] = jnp.zeros((CH - nv, 2 * A_WIDTH), F32)
    y = cb_ref[...] + cbuf[5:5 + CH] * cw_ref[0:1]
    for j in range(1, CONV_W):
        y = y + cbuf[5 + j:5 + j + CH] * cw_ref[j:j + 1]
    conv_tail = cbuf[5 + nv:8 + nv]
    qk = y * _sigmoid(y)
    q = qk[:, :A_WIDTH]
    k = qk[:, A_WIDTH:] * (A_HDIM ** -0.5)
    if nv == CH:
        v = v_ref[...]
        og = og_ref[...]
        gp = if_ref[...] + gb_ref[...]
    else:
        zpad = lambda n: jnp.zeros((CH - nv, n), F32)
        v = jnp.concatenate([v_ref[...], zpad(A_WIDTH)], axis=0)
        og = jnp.concatenate([og_ref[...], zpad(A_WIDTH)], axis=0)
        gp = jnp.concatenate([if_ref[...] + gb_ref[...], zpad(LANES)], axis=0)

    row = lax.broadcasted_iota(I32, (CH, LANES), 0)
    li = gp
    lf = pltpu.roll(_log_sigmoid(gp), LANES - A_HEADS, 1)
    if nv < CH:
        li = jnp.where(row < nv, li, NEG_BIG)
        lf = jnp.where(row < nv, lf, 0.0)
    r2 = lax.broadcasted_iota(I32, (CH, CH), 0)
    c2 = lax.broadcasted_iota(I32, (CH, CH), 1)
    tri = r2 >= c2
    tril_b = jnp.where(tri, 1.0, 0.0).astype(BF16)
    b = _dotx_l(tril_b, lf)
    g = li - b
    sel = jnp.where(lax.broadcasted_iota(I32, (8, LANES), 0) == lax.broadcasted_iota(I32, (8, LANES), 1),
                    1.0, 0.0).astype(BF16)
    gT = _dotx_l(sel, g, f=_dot_nt)

    qb, kb, vb = q.astype(BF16), k.astype(BF16), v.astype(BF16)
    lane1 = lax.broadcasted_iota(I32, (1, LANES), 1)
    m_row = m_sc[0:1]
    H = range(A_HEADS)
    sls = [slice(h * A_HDIM, (h + 1) * A_HDIM) for h in H]
    bcol = [b[:, h:h + 1] for h in H]
    m_h = [m_row[:, h:h + 1] for h in H]
    C_h = [C_sc[h] for h in H]
    n_h = [n_sc[h:h + 1] for h in H]
    qk_t = [_dot_nt(qb[:, sl], kb[:, sl]) for sl in sls]
    qC = [_dot_nt(qb[:, sl], C_h[h].astype(BF16)) for h, sl in enumerate(sls)]
    dm = [jnp.where(tri, bcol[h] + gT[h:h + 1, :], -jnp.inf) for h in H]
    inter = [bcol[h] + m_h[h] for h in H]
    mt = [jnp.maximum(inter[h], jnp.max(dm[h], axis=1, keepdims=True)) for h in H]
    s = [qk_t[h] * jnp.exp(dm[h] - mt[h]) for h in H]
    wi = [jnp.exp(inter[h] - mt[h]) for h in H]
    sv = [_dot(s[h].astype(BF16), vb[:, sl]) for h, sl in enumerate(sls)]
    m_new = [mt[h][CH - 1:CH] for h in H]
    w_end = [jnp.exp(bcol[h][CH - 1:CH] + g[:, h:h + 1] - m_new[h]) for h in H]
    dec = [jnp.exp(bcol[h][CH - 1:CH] + m_h[h] - m_new[h]) for h in H]
    vk = [_dot_tn((v[:, sl] * w_end[h]).astype(BF16), kb[:, sl]) for h, sl in enumerate(sls)]
    outs = []
    m_next = m_row
    for h, sl in enumerate(sls):
        num = sv[h] + wi[h] * qC[h]
        den = jnp.sum(s[h], axis=1, keepdims=True) + wi[h] * jnp.sum(q[:, sl] * n_h[h], axis=1, keepdims=True)
        hh = num / jnp.maximum(jnp.abs(den), jnp.exp(-mt[h]))
        hn = hh * lax.rsqrt(jnp.mean(hh * hh, axis=1, keepdims=True) + EPS)
        outs.append(hn * an_ref[:, sl] * _sigmoid(og[:, sl]))
        C_sc[h] = dec[h] * C_h[h] + vk[h]
        n_sc[h:h + 1] = dec[h] * n_h[h] + jnp.sum(k[:, sl] * w_end[h], axis=0, keepdims=True)
        m_next = jnp.where(lane1 == h, m_new[h], m_next)
    m_sc[0:1] = m_next
    ha = jnp.concatenate(outs, axis=1)
    ha_o[...] = ha[:nv].astype(BF16)
    cbuf[5:8] = conv_tail

    @pl.when(c == pl.num_programs(1) - 1)
    def _fin():
        C1_o[0] = C_sc[...]
        n1_o[0] = n_sc[0:A_HEADS]
        m1_o[0] = m_sc[0:1]
        conv1_o[0] = conv_tail


def _mlstm(qk, v, og, pbg, C0, n0, m0, conv0, cw, cb, gb, an, *, n_seq, L, row_off):
    CH = MLSTM_CHUNK
    nv = min(L, CH)
    nch = L // nv
    off = row_off // nv
    gate_tile = pbg.shape[1] // LANES - 1
    tok = lambda n: pl.BlockSpec((nv, n), lambda s, c: (off + s * nch + c, 0))
    st = lambda shp: pl.BlockSpec((1,) + shp, lambda s, c: (s,) + (0,) * len(shp))
    full = lambda a: pl.BlockSpec(a.shape, lambda s, c: (0,) * a.ndim)
    in_specs = [tok(2 * A_WIDTH), tok(A_WIDTH), tok(A_WIDTH),
                pl.BlockSpec((nv, LANES), lambda s, c: (off + s * nch + c, gate_tile)),
                st((A_HEADS, A_HDIM, A_HDIM)), st((A_HEADS, A_HDIM)), st((1, LANES)),
                st((CONV_W - 1, 2 * A_WIDTH)), full(cw), full(cb), full(gb), full(an)]
    args = [qk, v, og, pbg, C0, n0, m0, conv0, cw, cb, gb, an]
    out_shape = [jax.ShapeDtypeStruct((n_seq * L, A_WIDTH), BF16),
                 jax.ShapeDtypeStruct((n_seq, A_HEADS, A_HDIM, A_HDIM), F32),
                 jax.ShapeDtypeStruct((n_seq, A_HEADS, A_HDIM), F32),
                 jax.ShapeDtypeStruct((n_seq, 1, LANES), F32),
                 jax.ShapeDtypeStruct((n_seq, CONV_W - 1, 2 * A_WIDTH), F32)]
    out_specs = [pl.BlockSpec((nv, A_WIDTH), lambda s, c: (s * nch + c, 0)),
                 st((A_HEADS, A_HDIM, A_HDIM)), st((A_HEADS, A_HDIM)),
                 st((1, LANES)), st((CONV_W - 1, 2 * A_WIDTH))]
    return pl.pallas_call(
        functools.partial(_mlstm_kernel, nv=nv),
        grid=(n_seq, nch),
        in_specs=in_specs, out_specs=out_specs, out_shape=out_shape,
        scratch_shapes=[pltpu.VMEM((A_HEADS, A_HDIM, A_HDIM), F32), pltpu.VMEM((8, A_HDIM), F32),
                        pltpu.VMEM((8, LANES), F32), pltpu.VMEM((CH + 8, 2 * A_WIDTH), F32)],
        compiler_params=_cparams(("arbitrary", "arbitrary")),
        name="mlstm",
    )(*args)


RWKV_CHUNK = 64
RWKV_NCHUNK = 4
B_COLS_PAD = 3 * B_WIDTH + 2 * LORA_PAD + 128
PAIR = 2 * B_HDIM


def _softplus(z):
    return jnp.maximum(z, 0.0) + jnp.log1p(jnp.exp(-jnp.abs(z)))


def _rwkv_kernel(pb_ref, S0_ref, sh0_ref, mu_ref, w0_ref, ww2_ref, a0_ref, wa2_ref, wg2_ref,
                 kk_ref, ka_ref, rk_ref, gw_ref, gb_ref, bd_ref,
                 hb_o, S1_o, sh1_o, S_sc, sbuf, *, nv):
    C = RWKV_CHUNK
    nc = -(-nv // C)
    TB = nc * C
    t = pl.program_id(1)

    @pl.when(t == 0)
    def _init():
        S_sc[...] = S0_ref[0]
        sbuf[7:8] = sh0_ref[0]

    sbuf[8:8 + nv] = pb_ref[...]
    if nv < TB:
        sbuf[8 + nv:8 + TB] = jnp.zeros((TB - nv, B_COLS_PAD), F32)
    pb = sbuf[8:8 + TB]
    prev = sbuf[7:7 + TB]
    last_row = sbuf[7 + nv:8 + nv]
    xm = pb + (prev - pb) * mu_ref[...]
    W = B_WIDTH
    r = xm[:, 0:W]
    kb = xm[:, W:2 * W]
    vb = xm[:, 2 * W:3 * W]
    wl = xm[:, 3 * W:3 * W + LORA_PAD]
    al = xm[:, 3 * W + LORA_PAD:3 * W + 2 * LORA_PAD]
    gl = xm[:, 3 * W + 2 * LORA_PAD:]
    seg_sum = lambda z: _dot(z.astype(BF16), bd_ref[...])
    w_log = -_softplus(-(w0_ref[...] + _dot(jnp.tanh(wl).astype(BF16), ww2_ref[...]))) - 0.5
    lw = -jnp.exp(w_log)
    a = _sigmoid(a0_ref[...] + _dot(al.astype(BF16), wa2_ref[...]))
    g = _dot(_sigmoid(gl).astype(BF16), wg2_ref[...])
    kkr = kb * kk_ref[...]
    kk = kkr / jnp.maximum(jnp.sqrt(seg_sum(kkr * kkr)), 1e-12)
    k = kb * (1.0 + (a - 1.0) * ka_ref[...])
    bonus = seg_sum(r * k * rk_ref[...]) * vb
    if nv < TB:
        valid = lax.broadcasted_iota(I32, (TB, W), 0) < nv
        zero = lambda z: jnp.where(valid, z, 0.0)
        lw, kk, k, vb = zero(lw), zero(kk), zero(k), zero(vb)

    lane = lax.broadcasted_iota(I32, (C, PAIR), 1)
    h0 = lane < B_HDIM
    sep = lambda z: jnp.concatenate([jnp.where(h0, z, 0.0), jnp.where(h0, 0.0, z)], axis=0)
    ri = lax.broadcasted_iota(I32, (C, C), 0)
    ci = lax.broadcasted_iota(I32, (C, C), 1)
    tril_c = jnp.where(ri >= ci, 1.0, 0.0).astype(BF16)
    r2 = lax.broadcasted_iota(I32, (PAIR, PAIR), 0)
    c2 = lax.broadcasted_iota(I32, (PAIR, PAIR), 1)
    low_s = r2 > c2
    low_i = r2 >= c2
    eye = jnp.where(r2 == c2, 1.0, 0.0)
    npair = B_HEADS // 2
    units = [(c, p) for c in range(nc) for p in range(npair)]
    cat0 = lambda x, y: jnp.concatenate([x, y], axis=0)
    cat1 = lambda x, y: jnp.concatenate([x, y], axis=1)
    bf = lambda z: z.astype(BF16)
    lhs, rhs_t, vs, wc = [], [], [], []
    for (c, p) in units:
        rs = slice(c * C, (c + 1) * C)
        ls = slice(p * PAIR, (p + 1) * PAIR)
        lwc = lw[rs, ls]
        b = _dotx_l(tril_c, lwc)
        emb = jnp.exp(-b)
        kkc = kk[rs, ls]
        lhs.append(bf(cat0(sep(kkc * jnp.exp(b - lwc)), sep(r[rs, ls] * jnp.exp(b)))))
        rhs_t.append(bf(cat0(sep(k[rs, ls] * emb), sep(kkc * a[rs, ls] * emb))))
        vs.append(bf(sep(vb[rs, ls])))
        wc.append(jnp.exp(b[C - 1:C]))
    A = [_dot_nt(l_, r_) for l_, r_ in zip(lhs, rhs_t)]
    Ak = [jnp.where(low_s, x[:PAIR, :PAIR], 0.0) for x in A]
    Ab = [jnp.where(low_s, x[:PAIR, PAIR:], 0.0) for x in A]
    Ark = [jnp.where(low_i, x[PAIR:, :PAIR], 0.0) for x in A]
    Arb = [jnp.where(low_i, x[PAIR:, PAIR:], 0.0) for x in A]
    X2 = [_dot(bf(cat0(x, y)), v_) for x, y, v_ in zip(Ak, Ark, vs)]
    VtK = [_dot_tn(v_, r_[:PAIR]) for v_, r_ in zip(vs, rhs_t)]
    Tm = [eye - x for x in Ab]
    P = [_dot(bf(x), bf(x)) for x in Ab]
    n = 2
    while 2 * n < C:
        X = [_dot(bf(cat0(t_, p_)), bf(p_)) for t_, p_ in zip(Tm, P)]
        Tm = [t_ + x[:PAIR] for t_, x in zip(Tm, X)]
        P = [x[PAIR:] for x in X]
        n *= 2
    Tm = [t_ + _dot(bf(t_), bf(p_)) for t_, p_ in zip(Tm, P)]
    Y = [_dot(bf(t_), cat1(l_[:PAIR], bf(x[:PAIR]))) for t_, l_, x in zip(Tm, lhs, X2)]
    Z = [_dot(bf(x), bf(y)) for x, y in zip(Arb, Y)]
    Rp = [bf(l_[PAIR:].astype(F32) - z[:, :PAIR]) for l_, z in zip(lhs, Z)]
    Op = [x[PAIR:] - z[:, PAIR:] for x, z in zip(X2, Z)]
    W = [_dot_tn(bf(y), r_[PAIR:]) for y, r_ in zip(Y, rhs_t)]
    Mc = [bf(w_[:PAIR] * c_) for w_, c_ in zip(W, wc)]
    Nn = [(v_ - w_[PAIR:]) * c_ for v_, w_, c_ in zip(VtK, W, wc)]
    o_rows = []
    for c in range(nc):
        S = [S_sc[p] for p in range(npair)]
        Sb = [bf(s_) for s_ in S]
        u0 = c * npair
        O = [_dot_nt(Rp[u0 + p], Sb[p]) + Op[u0 + p] for p in range(npair)]
        for p in range(npair):
            S_sc[p] = S[p] * wc[u0 + p] - _dot(Sb[p], Mc[u0 + p]) + Nn[u0 + p]
        o_rows.append(jnp.concatenate([x[:C] + x[C:] for x in O], axis=1))
    o = o_rows[0] if nc == 1 else jnp.concatenate(o_rows, axis=0)

    mean = seg_sum(o) * (1.0 / B_HDIM)
    d = o - mean
    var = seg_sum(d * d) * (1.0 / B_HDIM)
    hb = (d * lax.rsqrt(var + GN_EPS) * gw_ref[...] + gb_ref[...] + bonus) * g
    hb_o[...] = hb[:nv].astype(BF16)
    sbuf[7:8] = last_row

    @pl.when(t == pl.num_programs(1) - 1)
    def _fin():
        S1_o[0] = S_sc[...]
        sh1_o[0] = last_row


def _rwkv(pb, S0, sh0, mu, w0, ww2, a0, wa2, wg2, k_k, k_a, r_k, gn_w, gn_b, bd, *, n_seq, L, row_off):
    nv = min(L, RWKV_CHUNK * RWKV_NCHUNK)
    nst = L // nv
    off = row_off // nv
    TB = -(-nv // RWKV_CHUNK) * RWKV_CHUNK
    tok = lambda n: pl.BlockSpec((nv, n), lambda s, t: (off + s * nst + t, 0))
    st = lambda shp: pl.BlockSpec((1,) + shp, lambda s, t: (s,) + (0,) * len(shp))
    full = lambda a: pl.BlockSpec(a.shape, lambda s, t: (0,) * a.ndim)
    consts = [mu, w0, ww2, a0, wa2, wg2, k_k, k_a, r_k, gn_w, gn_b, bd]
    npair = B_HEADS // 2
    in_specs = [tok(B_COLS_PAD), st((npair, PAIR, PAIR)), st((1, B_COLS_PAD))] + [full(a) for a in consts]
    args = [pb, S0, sh0] + consts
    out_shape = [jax.ShapeDtypeStruct((n_seq * L, B_WIDTH), BF16),
                 jax.ShapeDtypeStruct((n_seq, npair, PAIR, PAIR), F32),
                 jax.ShapeDtypeStruct((n_seq, 1, B_COLS_PAD), F32)]
    out_specs = [pl.BlockSpec((nv, B_WIDTH), lambda s, t: (s * nst + t, 0)),
                 st((npair, PAIR, PAIR)), st((1, B_COLS_PAD))]
    return pl.pallas_call(
        functools.partial(_rwkv_kernel, nv=nv),
        grid=(n_seq, nst),
        in_specs=in_specs, out_specs=out_specs, out_shape=out_shape,
        scratch_shapes=[pltpu.VMEM((npair, PAIR, PAIR), F32), pltpu.VMEM((TB + 8, B_COLS_PAD), F32)],
        compiler_params=_cparams(("arbitrary", "arbitrary")),
        name="rwkv",
    )(*args)


SUB = 8


def _rows_to_tiles(ref, x):
    rows = x.shape[0]
    for j in range(SUB):
        ref[pl.ds(j, rows, stride=SUB), :] = x[:, j * LANES:(j + 1) * LANES]


def _tiles_to_rows(ref, rows):
    return jnp.concatenate([ref[pl.ds(j, rows, stride=SUB), :] for j in range(SUB)], axis=1)


def _row_tile(ref, r8):
    return ref.at[pl.ds(pl.multiple_of(r8, SUB), SUB)]


def _out_router_kernel(hap_ref, has_ref, hbp_ref, hbs_ref, xp_ref, xs_ref, woa_ref, wob_ref, nf_ref, wr_ref, br_ref,
                       h1_o, u2_o, ids_o, gates_o, rank_o, cnt_o, carry, *, ntp):
    i = pl.program_id(0)

    @pl.when(i == 0)
    def _init():
        carry[...] = jnp.zeros_like(carry)

    ha = _pick(i, ntp, hap_ref, has_ref)
    hb = _pick(i, ntp, hbp_ref, hbs_ref)
    h1 = _pick(i, ntp, xp_ref, xs_ref) + _dot(ha, woa_ref[...]) + _dot(hb, wob_ref[...])
    h1_o[...] = h1
    u2 = _rms(h1, nf_ref[...])
    _rows_to_tiles(u2_o, u2)
    E = br_ref.shape[0]
    logits = _dot_nt(wr_ref[...], u2.astype(BF16)) + br_ref[...]
    TM = logits.shape[1]
    row_f = lax.broadcasted_iota(I32, (E, TM), 0).astype(F32)
    work = logits
    vals, idxs = [], []
    for _ in range(TOP_K):
        m = jnp.max(work, axis=0, keepdims=True)
        idx = jnp.min(jnp.where(work == m, row_f, float(E)), axis=0, keepdims=True)
        vals.append(m)
        idxs.append(idx)
        work = jnp.where(row_f == idx, -jnp.inf, work)
    es = [jnp.exp(v - vals[0]) for v in vals]
    tot = es[0] + es[1] + es[2] + es[3]
    gates = [e_ / tot for e_ in es]
    member = jnp.zeros((E, TM), F32)
    for kk in range(TOP_K):
        member = member + jnp.where(row_f == idxs[kk], 1.0, 0.0)
    r2 = lax.broadcasted_iota(I32, (TM, TM), 0)
    c2 = lax.broadcasted_iota(I32, (TM, TM), 1)
    before = _dot(member.astype(BF16), jnp.where(r2 < c2, 1.0, 0.0).astype(BF16)) + carry[:, 0:1]
    ranks = [jnp.sum(jnp.where(row_f == idxs[kk], before, 0.0), axis=0, keepdims=True) for kk in range(TOP_K)]
    pad = jnp.zeros((SUB - TOP_K, TM), F32)
    rows8 = lambda parts: jnp.concatenate(parts + [pad], axis=0)
    ids_o[...] = rows8(idxs).astype(I32)
    rank_o[...] = rows8(ranks).astype(I32)
    g8 = rows8(gates)
    sel = jnp.where(lax.broadcasted_iota(I32, (SUB, LANES), 0) == lax.broadcasted_iota(I32, (SUB, LANES), 1),
                    1.0, 0.0).astype(BF16)
    gates_o[...] = _dotx_l(sel, g8, f=lambda a, b: _dot_tn(b, a))
    carry[...] = carry[...] + jnp.sum(member, axis=1, keepdims=True)
    cnt_o[...] = carry[...].astype(I32)


def _out_router(hap, has, hbp, hbs, xp, xs, woa, wob, nf, wr, br):
    D = xp.shape[1]
    T = xp.shape[0] + xs.shape[0]
    ntp = xp.shape[0] // ROW_TILE
    nt = T // ROW_TILE
    row = lambda n: pl.BlockSpec((ROW_TILE, n), lambda i: (i, 0))
    full = lambda a: pl.BlockSpec(a.shape, lambda i: (0, 0))
    cnt = jax.ShapeDtypeStruct((br.shape[0], LANES), I32)
    return pl.pallas_call(
        functools.partial(_out_router_kernel, ntp=ntp),
        grid=(nt,),
        in_specs=_two_specs(ntp, A_WIDTH) + _two_specs(ntp, B_WIDTH) + _two_specs(ntp, D)
        + [full(woa), full(wob), full(nf), full(wr), full(br)],
        out_specs=[row(D), pl.BlockSpec((ROW_TILE * SUB, LANES), lambda i: (i, 0)),
                   pl.BlockSpec((SUB, ROW_TILE), lambda i: (0, i)), row(LANES),
                   pl.BlockSpec((SUB, ROW_TILE), lambda i: (0, i)), full(cnt)],
        out_shape=[jax.ShapeDtypeStruct((T, D), F32), jax.ShapeDtypeStruct((T * SUB, LANES), F32),
                   jax.ShapeDtypeStruct((SUB, T), I32), jax.ShapeDtypeStruct((T, LANES), F32),
                   jax.ShapeDtypeStruct((SUB, T), I32), cnt],
        scratch_shapes=[pltpu.VMEM(cnt.shape, F32)],
        compiler_params=_cparams(("arbitrary",)),
        name="out_router",
    )(hap, has, hbp, hbs, xp, xs, woa, wob, nf, wr, br)


MOE_BLOCK = 512


def _row_copy_wait(buf, sem, n):
    for _ in range(n):
        pltpu.make_async_copy(buf, buf, sem).wait()


DMA_UNROLL = 4


def _scatter_kernel(plo_ref, phi_ref, nu_ref, dest_ref, u_ref, xs_o, zbuf, sem):
    TM = u_ref.shape[0] // SUB
    BLK = MOE_BLOCK * SUB

    @pl.when(pl.program_id(0) == 0)
    def _zero_unused_slots():
        zbuf[...] = jnp.zeros_like(zbuf)
        zrow = zbuf.at[pl.ds(0, SUB)]

        def per_expert(e, n):
            def per_row(rw, _):
                pltpu.make_async_copy(zrow, _row_tile(xs_o, rw * SUB), sem).start()
                return 0

            lax.fori_loop(plo_ref[e], phi_ref[e], per_row, 0)
            return n + (phi_ref[e] - plo_ref[e])

        n_rows = lax.fori_loop(0, plo_ref.shape[0], per_expert, 0)

        def wait_row(_, c):
            pltpu.make_async_copy(zrow, xs_o.at[pl.ds(0, SUB)], sem).wait()
            return c

        lax.fori_loop(0, n_rows, wait_row, 0)

        def per_block(j, _):
            cp = pltpu.make_async_copy(zbuf, xs_o.at[pl.ds(pl.multiple_of(j * BLK, BLK), BLK)], sem)
            cp.start()
            cp.wait()
            return 0

        lax.fori_loop(nu_ref[0], xs_o.shape[0] // BLK, per_block, 0)

    def body(i, _):
        src = _row_tile(u_ref, i * SUB)
        for kk in range(TOP_K):
            pltpu.make_async_copy(src, _row_tile(xs_o, dest_ref[i * TOP_K + kk]), sem).start(priority=kk % 2)
        return 0

    lax.fori_loop(0, TM, body, 0, unroll=DMA_UNROLL)
    _row_copy_wait(u_ref, sem, TOP_K)


def _scatter_rows(pad_lo, pad_hi, n_used, dest8_flat, u2t, n_slots):
    T = u2t.shape[0] // SUB
    grid_spec = pltpu.PrefetchScalarGridSpec(
        num_scalar_prefetch=3,
        grid=(T // ROW_TILE,),
        in_specs=[pl.BlockSpec((ROW_TILE * TOP_K,), lambda i, *_: (i,), memory_space=pltpu.SMEM),
                  pl.BlockSpec((ROW_TILE * SUB, LANES), lambda i, *_: (i, 0))],
        out_specs=pl.BlockSpec(memory_space=pl.ANY),
        scratch_shapes=[pltpu.VMEM((MOE_BLOCK * SUB, LANES), F32), pltpu.SemaphoreType.DMA(())],
    )
    return pl.pallas_call(
        _scatter_kernel,
        grid_spec=grid_spec,
        out_shape=jax.ShapeDtypeStruct((n_slots * SUB, LANES), F32),
        compiler_params=_cparams(("arbitrary",)),
        name="moe_scatter",
    )(pad_lo, pad_hi, n_used, dest8_flat, u2t)


def _expert_kernel(be_ref, nu_ref, nxt_ref, xs_ref, wgu_hbm, bgu_ref, wd_hbm, bd_ref, ys_o,
                   wgu_f, wd_f, wgu_b, wd_b, slot_ref, sems):
    j = pl.program_id(0)
    F = wd_b.shape[0]
    used = j < nu_ref[0]
    e = be_ref[j]

    def w_copies(eid, slot):
        return (pltpu.make_async_copy(wgu_hbm.at[eid], wgu_f.at[slot], sems.at[0, slot]),
                pltpu.make_async_copy(wd_hbm.at[eid], wd_f.at[slot], sems.at[1, slot]))

    @pl.when(j == 0)
    def _first_fetch():
        slot_ref[0] = 0
        for cp in w_copies(e, 0):
            cp.start()

    @pl.when(used & ((j == 0) | (e != be_ref[jnp.maximum(j - 1, 0)])))
    def _new_expert():
        slot = slot_ref[0]
        for cp in w_copies(e, slot):
            cp.wait()
        nx = nxt_ref[e]

        @pl.when(nx != e)
        def _prefetch():
            for cp in w_copies(nx, 1 - slot):
                cp.start()

        wgu_b[...] = wgu_f[slot].astype(BF16)
        wd_b[...] = wd_f[slot].astype(BF16)
        slot_ref[0] = 1 - slot

    @pl.when(used)
    def _compute():
        x = _tiles_to_rows(xs_ref, MOE_BLOCK)
        gu = _dot(x.astype(BF16), wgu_b[...]) + bgu_ref[0]
        g_ = jnp.minimum(gu[:, :F], SWIGLU_LIMIT)
        up = jnp.clip(gu[:, F:], -SWIGLU_LIMIT, SWIGLU_LIMIT)
        act = (up + 1.0) * g_ * _sigmoid(SWIGLU_ALPHA * g_)
        _rows_to_tiles(ys_o, _dot(act.astype(BF16), wd_b[...]) + bd_ref[0])

    @pl.when(j >= nu_ref[0])
    def _unused():
        ys_o[...] = jnp.zeros_like(ys_o)


def _experts(blk_e, n_used, nxt_e, xs, wgu, bgu, wd, bd):
    nb = xs.shape[0] // (MOE_BLOCK * SUB)
    E, D, F2 = wgu.shape
    F = F2 // 2
    grid_spec = pltpu.PrefetchScalarGridSpec(
        num_scalar_prefetch=3,
        grid=(nb,),
        in_specs=[pl.BlockSpec((MOE_BLOCK * SUB, LANES), lambda j, be, nu, nx: (jnp.minimum(j, nu[0] - 1), 0)),
                  pl.BlockSpec(memory_space=pl.ANY),
                  pl.BlockSpec((1, 1, F2), lambda j, be, nu, nx: (be[j], 0, 0)),
                  pl.BlockSpec(memory_space=pl.ANY),
                  pl.BlockSpec((1, 1, D), lambda j, be, nu, nx: (be[j], 0, 0))],
        out_specs=pl.BlockSpec((MOE_BLOCK * SUB, LANES), lambda j, be, nu, nx: (j, 0)),
        scratch_shapes=[pltpu.VMEM((2, D, F2), F32), pltpu.VMEM((2, F, D), F32),
                        pltpu.VMEM((D, F2), BF16), pltpu.VMEM((F, D), BF16),
                        pltpu.SMEM((1,), I32), pltpu.SemaphoreType.DMA((2, 2))],
    )
    return pl.pallas_call(
        _expert_kernel,
        grid_spec=grid_spec,
        out_shape=jax.ShapeDtypeStruct(xs.shape, F32),
        compiler_params=_cparams(("arbitrary",)),
        name="moe_experts",
    )(blk_e, n_used, nxt_e, xs, wgu, bgu, wd, bd)


def _combine_kernel(dest0_ref, destn_ref, h1_ref, pp_ref, ps_ref, gates_ref, ys_ref, npl_ref, wpg_ref, wpp_ref, fn_ref,
                    yp_o, ysm_o, gbuf, ybuf, sems, *, ntp, n_tiles):
    TM = h1_ref.shape[0]
    t = pl.program_id(0)

    def fetch(dref, i, slot):
        for kk in range(TOP_K):
            src = _row_tile(ys_ref, dref[i * TOP_K + kk])
            pltpu.make_async_copy(src, _row_tile(gbuf.at[slot, kk], i * SUB), sems.at[slot]).start(priority=kk % 2)

    @pl.when(t == 0)
    def _first_fetch():
        def body(i, _):
            fetch(dest0_ref, i, 0)
            return 0

        lax.fori_loop(0, TM, body, 0, unroll=DMA_UNROLL)

    def tile(slot):
        _row_copy_wait(gbuf.at[slot, 0], sems.at[slot], TOP_K)
        for i in range(TM):
            fetch(destn_ref, i, 1 - slot)
        gates = gates_ref[...]
        h2 = h1_ref[...]
        for kk in range(TOP_K):
            h2 = h2 + _tiles_to_rows(gbuf.at[slot, kk], TM) * gates[:, kk:kk + 1]
        u3 = _rms(h2, npl_ref[...])
        gate = _sigmoid(_dot(u3.astype(BF16), wpg_ref[...]))
        p = _pick(t, ntp, pp_ref, ps_ref)
        h3 = h2 + gate * _dot(p.astype(BF16), wpp_ref[...])
        return _rms(h3, fn_ref[...])

    for slot in range(2):
        @pl.when(t % 2 == slot)
        def _tile():
            ybuf[...] = tile(slot)

    @pl.when(t < ntp)
    def _prompt_rows():
        yp_o[...] = ybuf[...]

    @pl.when(t >= ntp)
    def _sample_rows():
        ysm_o[...] = ybuf[...]

    @pl.when(t == n_tiles - 1)
    def _drain():
        other = n_tiles % 2
        _row_copy_wait(gbuf.at[other, 0], sems.at[other], TOP_K)


def _combine(dest_flat, h1, pp, ps, gates, ys, npl, wpg, wpp, fn):
    T, D = h1.shape
    nt = T // ROW_TILE
    ntp = pp.shape[0] // ROW_TILE
    row = lambda n: pl.BlockSpec((ROW_TILE, n), lambda i: (i, 0))
    full = lambda a: pl.BlockSpec(a.shape, lambda i: (0, 0))
    idx = lambda f: pl.BlockSpec((ROW_TILE * TOP_K,), f, memory_space=pltpu.SMEM)
    return pl.pallas_call(
        functools.partial(_combine_kernel, ntp=ntp, n_tiles=nt),
        grid=(nt,),
        in_specs=[idx(lambda i: (0,)), idx(lambda i: (jnp.minimum(i + 1, nt - 1),)), row(D)]
        + _two_specs(ntp, pp.shape[1])
        + [row(LANES), pl.BlockSpec(memory_space=pl.ANY), full(npl), full(wpg), full(wpp), full(fn)],
        out_specs=_two_specs(ntp, D),
        out_shape=[jax.ShapeDtypeStruct((pp.shape[0], D), F32), jax.ShapeDtypeStruct((ps.shape[0], D), F32)],
        scratch_shapes=[pltpu.VMEM((2, TOP_K, ROW_TILE * SUB, LANES), F32), pltpu.VMEM((ROW_TILE, D), F32),
                        pltpu.SemaphoreType.DMA((2,))],
        compiler_params=_cparams(("arbitrary",)),
        name="moe_combine_ple",
    )(dest_flat, dest_flat, h1, pp, ps, gates, ys, npl, wpg, wpp, fn)


def _pad_lora_cols(x):
    W3 = 3 * B_WIDTH
    z = jnp.zeros(x.shape[:-1] + (LORA_PAD - 64,), x.dtype)
    return jnp.concatenate([x[..., :W3 + 64], z, x[..., W3 + 64:W3 + 128], z, x[..., W3 + 128:]], axis=-1)


def _unpad_lora_cols(x):
    W3 = 3 * B_WIDTH
    return jnp.concatenate([x[..., :W3 + 64], x[..., W3 + LORA_PAD:W3 + LORA_PAD + 64], x[..., W3 + 2 * LORA_PAD:]], axis=-1)


def _heads_to_pairs(S):
    n = S.shape[0]
    S = S.reshape(n, B_HEADS // 2, 2, B_HDIM, B_HDIM)
    z = jnp.zeros_like(S[:, :, 0])
    top = jnp.concatenate([S[:, :, 0], z], axis=-1)
    bot = jnp.concatenate([z, S[:, :, 1]], axis=-1)
    return jnp.concatenate([top, bot], axis=-2)


def _pairs_to_heads(Sb):
    n = Sb.shape[0]
    a = Sb[:, :, :B_HDIM, :B_HDIM]
    b = Sb[:, :, B_HDIM:, B_HDIM:]
    return jnp.stack([a, b], axis=2).reshape(n, B_HEADS, B_HDIM, B_HDIM)


def kernel(x_prompt, x_sample, state_mlstm_C, state_mlstm_n, state_mlstm_m, state_mlstm_conv, state_rwkv_S, state_rwkv_shift, p_prompt, p_sample, norm_mix, w_in, conv_w, conv_b, b_ig, b_fg, a_norm, shift_mu, w0, w_w2, a0, w_a2, w_g2, k_k, k_a, r_k, gn_w, gn_b, w_out, norm_ffn, w_router, b_router, w_gu, b_gu, w_down, b_down, norm_ple, w_ple_gate, w_ple_proj, final_norm):
    assert norm_mix.shape[0] == 1, "single-layer trunk"
    nbp, Lp, D = x_prompt.shape
    nbs, Ls, _ = x_sample.shape
    Tp, Ts = nbp * Lp, nbs * Ls
    T = Tp + Ts
    row = lambda a: a.reshape(1, -1).astype(F32)

    wi = w_in[0]
    AW2 = 2 * A_WIDTH
    wqk = wi[:, :AW2].astype(BF16)
    wv = wi[:, AW2:AW2 + A_WIDTH].astype(BF16)
    wo = wi[:, AW2 + A_WIDTH:AW2 + 2 * A_WIDTH].astype(BF16)
    a_cols = 4 * A_WIDTH + 2 * A_HEADS
    wif = jnp.pad(wi[:, 4 * A_WIDTH:a_cols], ((0, 0), (0, LANES - 2 * A_HEADS)))
    wbg = jnp.concatenate([_pad_lora_cols(wi[:, a_cols:]), wif], axis=1).astype(BF16)
    gate_bias = jnp.pad(jnp.concatenate([b_ig[0], b_fg[0]]), (0, LANES - 2 * A_HEADS)).reshape(1, LANES)
    padw = lambda w: jnp.pad(w, ((0, LORA_PAD - w.shape[0]), (0, 0)))
    hid = jnp.arange(B_WIDTH) // B_HDIM
    bd = (hid[:, None] == hid[None, :]).astype(BF16)
    E = w_router.shape[-1]
    wr = w_router[0].T.astype(BF16)
    br = b_router[0].reshape(E, 1)

    assert Tp % ROW_TILE == 0 and Ts % ROW_TILE == 0
    xp, xsm = x_prompt.reshape(Tp, D), x_sample.reshape(Ts, D)
    qk, v, og, pb = _in_proj(xp, xsm, row(norm_mix), wqk, wv, wo, wbg)
    ifg = pb

    f0 = lambda *shp: jnp.zeros(shp, F32)
    ml_consts = (conv_w[0], row(conv_b), gate_bias, row(a_norm))
    m_s = jnp.pad(state_mlstm_m[0], ((0, 0), (0, LANES - A_HEADS))).reshape(nbs, 1, LANES)
    hap, Cp, np_, mp, convp = _mlstm(qk, v, og, ifg, f0(nbp, A_HEADS, A_HDIM, A_HDIM), f0(nbp, A_HEADS, A_HDIM),
                                     f0(nbp, 1, LANES), f0(nbp, CONV_W - 1, AW2), *ml_consts,
                                     n_seq=nbp, L=Lp, row_off=0)
    has, Cs, ns_, ms, convs = _mlstm(qk, v, og, ifg, state_mlstm_C[0], state_mlstm_n[0], m_s, state_mlstm_conv[0],
                                     *ml_consts, n_seq=nbs, L=Ls, row_off=Tp)

    rw_consts = (row(_pad_lora_cols(shift_mu[0])), row(w0), padw(w_w2[0]).astype(BF16), row(a0), padw(w_a2[0]).astype(BF16),
                 w_g2[0].astype(BF16),
                 row(k_k), row(k_a), row(r_k), row(gn_w), row(gn_b), bd)
    npair = B_HEADS // 2
    hbp, Sp, shp_ = _rwkv(pb, f0(nbp, npair, PAIR, PAIR), f0(nbp, 1, B_COLS_PAD), *rw_consts,
                          n_seq=nbp, L=Lp, row_off=0)
    hbs, Ss, shs = _rwkv(pb, _heads_to_pairs(state_rwkv_S[0]), _pad_lora_cols(state_rwkv_shift[0]), *rw_consts,
                         n_seq=nbs, L=Ls, row_off=Tp)

    wout = w_out[0].astype(BF16)
    h1, u2, ids, gates, rank, counts = _out_router(hap, has, hbp, hbs, xp, xsm, wout[:A_WIDTH], wout[A_WIDTH:],
                                                   row(norm_ffn), wr, br)

    counts = counts[:, 0]
    pcounts = (counts + MOE_BLOCK - 1) // MOE_BLOCK * MOE_BLOCK
    pend = jnp.cumsum(pcounts)
    pstart = pend - pcounts
    n_blk = -(-T * TOP_K // MOE_BLOCK) + E
    eid = jnp.arange(E, dtype=I32)
    base = jnp.sum(jnp.where(ids[None, :TOP_K] == eid[:, None, None], pstart[:, None, None], 0), axis=0)
    dest = ((base + rank[:TOP_K]) * SUB).T.reshape(-1).astype(I32)
    blk_start = jnp.arange(n_blk, dtype=I32) * MOE_BLOCK
    blk_e = jnp.minimum(jnp.sum(pend[None, :] <= blk_start[:, None], axis=1), E - 1).astype(I32)
    n_used = (pend[E - 1:] // MOE_BLOCK).astype(I32)

    xs = _scatter_rows((pstart + counts).astype(I32), pend.astype(I32), n_used, dest, u2, n_blk * MOE_BLOCK)
    later = jnp.where((eid[None, :] > eid[:, None]) & (pcounts[None, :] > 0), eid[None, :], E)
    nxt_e = jnp.min(later, axis=1)
    nxt_e = jnp.where(nxt_e < E, nxt_e, eid).astype(I32)
    ys = _experts(blk_e, n_used, nxt_e, xs, w_gu[0], b_gu[0][:, None, :], w_down[0], b_down[0][:, None, :])
    yp, ysm = _combine(dest, h1, p_prompt[0].reshape(Tp, -1), p_sample[0].reshape(Ts, -1), gates, ys, row(norm_ple),
                       w_ple_gate[0].astype(BF16), w_ple_proj[0].astype(BF16), row(final_norm))

    lead = lambda a: a[None]
    return (yp.reshape(nbp, Lp, D), ysm.reshape(nbs, Ls, D),
            lead(Cp), lead(np_), lead(mp[:, 0, :A_HEADS]), lead(convp), lead(_pairs_to_heads(Sp)), lead(_unpad_lora_cols(shp_)),
            lead(Cs), lead(ns_), lead(ms[:, 0, :A_HEADS]), lead(convs), lead(_pairs_to_heads(Ss)), lead(_unpad_lora_cols(shs)))
```

```python
import functools

import jax
import jax.numpy as jnp
from jax import lax
from jax.experimental import pallas as pl
from jax.experimental.pallas import tpu as pltpu

F32 = jnp.float32
BF16 = jnp.bfloat16
I32 = jnp.int32

EPS = 1e-6
GN_EPS = 64e-5
A_HEADS, A_HDIM = 4, 128
B_HEADS, B_HDIM = 8, 64
A_WIDTH = A_HEADS * A_HDIM
B_WIDTH = B_HEADS * B_HDIM
CONV_W = 4
LORA_PAD = 128
N_EXPERTS, TOP_K = 32, 4
SWIGLU_LIMIT, SWIGLU_ALPHA = 7.0, 1.702
LANES = 128
ROW_TILE = 256
VMEM_LIMIT = 56 * 1024 * 1024


def _cparams(sem):
    return pltpu.CompilerParams(dimension_semantics=sem, vmem_limit_bytes=VMEM_LIMIT)


def _dot(a, b):
    return lax.dot_general(a, b, (((1,), (0,)), ((), ())), preferred_element_type=F32)


def _dot_nt(a, b):
    return lax.dot_general(a, b, (((1,), (1,)), ((), ())), preferred_element_type=F32)


def _dot_tn(a, b):
    return lax.dot_general(a, b, (((0,), (0,)), ((), ())), preferred_element_type=F32)


def _split2(x):
    hi = x.astype(BF16)
    lo = (x - hi.astype(F32)).astype(BF16)
    return hi, lo


def _split3(x):
    hi = x.astype(BF16)
    r1 = x - hi.astype(F32)
    mid = r1.astype(BF16)
    lo = (r1 - mid.astype(F32)).astype(BF16)
    return hi, mid, lo


def _dot3(a, b, f=_dot):
    ah, al = _split2(a)
    bh, bl = _split2(b)
    return f(ah, bh) + (f(ah, bl) + f(al, bh))


def _dotx_l(a_exact, b, f=_dot):
    b1, b2, b3 = _split3(b)
    return f(a_exact, b1) + (f(a_exact, b2) + f(a_exact, b3))


def _rms(x, g):
    return x * lax.rsqrt(jnp.mean(x * x, axis=-1, keepdims=True) + EPS) * g


def _sigmoid(x):
    return 1.0 / (1.0 + jnp.exp(-x))


def _two_specs(ntp, n):
    return [pl.BlockSpec((ROW_TILE, n), lambda i: (jnp.minimum(i, ntp - 1), 0)),
            pl.BlockSpec((ROW_TILE, n), lambda i: (jnp.maximum(i - ntp, 0), 0))]


def _pick(i, ntp, p_ref, s_ref):
    return jnp.where(i < ntp, p_ref[...], s_ref[...])


def _in_proj_kernel(xp_ref, xs_ref, g_ref, wqk_ref, wv_ref, wo_ref, wbg_ref,
                    qk_o, v_o, og_o, pbg_o, *, ntp):
    x = _pick(pl.program_id(0), ntp, xp_ref, xs_ref)
    ub = _rms(x, g_ref[...]).astype(BF16)
    qk_o[...] = _dot(ub, wqk_ref[...])
    v_o[...] = _dot(ub, wv_ref[...])
    og_o[...] = _dot(ub, wo_ref[...])
    pbg_o[...] = _dot(ub, wbg_ref[...])


def _in_proj(xp, xs, g, wqk, wv, wo, wbg):
    D = xp.shape[1]
    T = xp.shape[0] + xs.shape[0]
    ntp = xp.shape[0] // ROW_TILE
    row = lambda n: pl.BlockSpec((ROW_TILE, n), lambda i: (i, 0))
    full = lambda a: pl.BlockSpec(a.shape, lambda i: (0, 0))
    outs = [wqk.shape[1], wv.shape[1], wo.shape[1], wbg.shape[1]]
    return pl.pallas_call(
        functools.partial(_in_proj_kernel, ntp=ntp),
        grid=(T // ROW_TILE,),
        in_specs=_two_specs(ntp, D) + [full(g), full(wqk), full(wv), full(wo), full(wbg)],
        out_specs=[row(n) for n in outs],
        out_shape=[jax.ShapeDtypeStruct((T, n), F32) for n in outs],
        compiler_params=_cparams(("arbitrary",)),
        name="in_proj",
    )(xp, xs, g, wqk, wv, wo, wbg)


MLSTM_CHUNK = 128
NEG_BIG = -1e30


def _log_sigmoid(x):
    return jnp.minimum(x, 0.0) - jnp.log1p(jnp.exp(-jnp.abs(x)))


def _mlstm_kernel(qk_ref, v_ref, og_ref, if_ref, C0_ref, n0_ref, m0_ref, conv0_ref,
                  cw_ref, cb_ref, gb_ref, an_ref,
                  ha_o, C1_o, n1_o, m1_o, conv1_o, C_sc, n_sc, m_sc, cbuf, *, nv):
    CH = MLSTM_CHUNK
    c = pl.program_id(1)

    @pl.when(c == 0)
    def _init():
        C_sc[...] = C0_ref[0]
        n_sc[0:A_HEADS] = n0_ref[0]
        m_sc[0:1] = m0_ref[0]
        cbuf[5:8] = conv0_ref[0]

    cbuf[8:8 + nv] = qk_ref[...]
    if nv < CH:
        cbuf[8 + nv:8 + CH] = jnp.zeros((CH - nv, 2 * A_WIDTH), F32)
    y = cb_ref[...] + cbuf[5:5 + CH] * cw_ref[0:1]
    for j in range(1, CONV_W):
        y = y + cbuf[5 + j:5 + j + CH] * cw_ref[j:j + 1]
    conv_tail = cbuf[5 + nv:8 + nv]
    qk = y * _sigmoid(y)
    q = qk[:, :A_WIDTH]
    k = qk[:, A_WIDTH:] * (A_HDIM ** -0.5)
    if nv == CH:
        v = v_ref[...]
        og = og_ref[...]
        gp = if_ref[...] + gb_ref[...]
    else:
        zpad = lambda n: jnp.zeros((CH - nv, n), F32)
        v = jnp.concatenate([v_ref[...], zpad(A_WIDTH)], axis=0)
        og = jnp.concatenate([og_ref[...], zpad(A_WIDTH)], axis=0)
        gp = jnp.concatenate([if_ref[...] + gb_ref[...], zpad(LANES)], axis=0)

    row = lax.broadcasted_iota(I32, (CH, LANES), 0)
    li = gp
    lf = pltpu.roll(_log_sigmoid(gp), LANES - A_HEADS, 1)
    if nv < CH:
        li = jnp.where(row < nv, li, NEG_BIG)
        lf = jnp.where(row < nv, lf, 0.0)
    r2 = lax.broadcasted_iota(I32, (CH, CH), 0)
    c2 = lax.broadcasted_iota(I32, (CH, CH), 1)
    tri = r2 >= c2
    tril_b = jnp.where(tri, 1.0, 0.0).astype(BF16)
    b = _dotx_l(tril_b, lf)
    g = li - b
    sel = jnp.where(lax.broadcasted_iota(I32, (8, LANES), 0) == lax.broadcasted_iota(I32, (8, LANES), 1),
                    1.0, 0.0).astype(BF16)
    gT = _dotx_l(sel, g, f=_dot_nt)

    qb, kb, vb = q.astype(BF16), k.astype(BF16), v.astype(BF16)
    lane1 = lax.broadcasted_iota(I32, (1, LANES), 1)
    m_row = m_sc[0:1]
    H = range(A_HEADS)
    sls = [slice(h * A_HDIM, (h + 1) * A_HDIM) for h in H]
    bcol = [b[:, h:h + 1] for h in H]
    m_h = [m_row[:, h:h + 1] for h in H]
    C_h = [C_sc[h] for h in H]
    n_h = [n_sc[h:h + 1] for h in H]
    qk_t = [_dot_nt(qb[:, sl], kb[:, sl]) for sl in sls]
    qC = [_dot_nt(qb[:, sl], C_h[h].astype(BF16)) for h, sl in enumerate(sls)]
    dm = [jnp.where(tri, bcol[h] + gT[h:h + 1, :], -jnp.inf) for h in H]
    inter = [bcol[h] + m_h[h] for h in H]
    mt = [jnp.maximum(inter[h], jnp.max(dm[h], axis=1, keepdims=True)) for h in H]
    s = [qk_t[h] * jnp.exp(dm[h] - mt[h]) for h in H]
    wi = [jnp.exp(inter[h] - mt[h]) for h in H]
    sv = [_dot(s[h].astype(BF16), vb[:, sl]) for h, sl in enumerate(sls)]
    m_new = [mt[h][CH - 1:CH] for h in H]
    w_end = [jnp.exp(bcol[h][CH - 1:CH] + g[:, h:h + 1] - m_new[h]) for h in H]
    dec = [jnp.exp(bcol[h][CH - 1:CH] + m_h[h] - m_new[h]) for h in H]
    vk = [_dot_tn((v[:, sl] * w_end[h]).astype(BF16), kb[:, sl]) for h, sl in enumerate(sls)]
    outs = []
    m_next = m_row
    for h, sl in enumerate(sls):
        num = sv[h] + wi[h] * qC[h]
        den = jnp.sum(s[h], axis=1, keepdims=True) + wi[h] * jnp.sum(q[:, sl] * n_h[h], axis=1, keepdims=True)
        hh = num / jnp.maximum(jnp.abs(den), jnp.exp(-mt[h]))
        hn = hh * lax.rsqrt(jnp.mean(hh * hh, axis=1, keepdims=True) + EPS)
        outs.append(hn * an_ref[:, sl] * _sigmoid(og[:, sl]))
        C_sc[h] = dec[h] * C_h[h] + vk[h]
        n_sc[h:h + 1] = dec[h] * n_h[h] + jnp.sum(k[:, sl] * w_end[h], axis=0, keepdims=True)
        m_next = jnp.where(lane1 == h, m_new[h], m_next)
    m_sc[0:1] = m_next
    ha = jnp.concatenate(outs, axis=1)
    ha_o[...] = ha[:nv].astype(BF16)
    cbuf[5:8] = conv_tail

    @pl.when(c == pl.num_programs(1) - 1)
    def _fin():
        C1_o[0] = C_sc[...]
        n1_o[0] = n_sc[0:A_HEADS]
        m1_o[0] = m_sc[0:1]
        conv1_o[0] = conv_tail


def _mlstm(qk, v, og, pbg, C0, n0, m0, conv0, cw, cb, gb, an, *, n_seq, L, row_off):
    CH = MLSTM_CHUNK
    nv = min(L, CH)
    nch = L // nv
    off = row_off // nv
    gate_tile = pbg.shape[1] // LANES - 1
    tok = lambda n: pl.BlockSpec((nv, n), lambda s, c: (off + s * nch + c, 0))
    st = lambda shp: pl.BlockSpec((1,) + shp, lambda s, c: (s,) + (0,) * len(shp))
    full = lambda a: pl.BlockSpec(a.shape, lambda s, c: (0,) * a.ndim)
    in_specs = [tok(2 * A_WIDTH), tok(A_WIDTH), tok(A_WIDTH),
                pl.BlockSpec((nv, LANES), lambda s, c: (off + s * nch + c, gate_tile)),
                st((A_HEADS, A_HDIM, A_HDIM)), st((A_HEADS, A_HDIM)), st((1, LANES)),
                st((CONV_W - 1, 2 * A_WIDTH)), full(cw), full(cb), full(gb), full(an)]
    args = [qk, v, og, pbg, C0, n0, m0, conv0, cw, cb, gb, an]
    out_shape = [jax.ShapeDtypeStruct((n_seq * L, A_WIDTH), BF16),
                 jax.ShapeDtypeStruct((n_seq, A_HEADS, A_HDIM, A_HDIM), F32),
                 jax.ShapeDtypeStruct((n_seq, A_HEADS, A_HDIM), F32),
                 jax.ShapeDtypeStruct((n_seq, 1, LANES), F32),
                 jax.ShapeDtypeStruct((n_seq, CONV_W - 1, 2 * A_WIDTH), F32)]
    out_specs = [pl.BlockSpec((nv, A_WIDTH), lambda s, c: (s * nch + c, 0)),
                 st((A_HEADS, A_HDIM, A_HDIM)), st((A_HEADS, A_HDIM)),
                 st((1, LANES)), st((CONV_W - 1, 2 * A_WIDTH))]
    return pl.pallas_call(
        functools.partial(_mlstm_kernel, nv=nv),
        grid=(n_seq, nch),
        in_specs=in_specs, out_specs=out_specs, out_shape=out_shape,
        scratch_shapes=[pltpu.VMEM((A_HEADS, A_HDIM, A_HDIM), F32), pltpu.VMEM((8, A_HDIM), F32),
                        pltpu.VMEM((8, LANES), F32), pltpu.VMEM((CH + 8, 2 * A_WIDTH), F32)],
        compiler_params=_cparams(("arbitrary", "arbitrary")),
        name="mlstm",
    )(*args)


RWKV_CHUNK = 64
RWKV_NCHUNK = 4
B_COLS_PAD = 3 * B_WIDTH + 2 * LORA_PAD + 128
PAIR = 2 * B_HDIM


def _softplus(z):
    return jnp.maximum(z, 0.0) + jnp.log1p(jnp.exp(-jnp.abs(z)))


def _rwkv_kernel(pb_ref, S0_ref, sh0_ref, mu_ref, w0_ref, ww2_ref, a0_ref, wa2_ref, wg2_ref,
                 kk_ref, ka_ref, rk_ref, gw_ref, gb_ref, bd_ref,
                 hb_o, S1_o, sh1_o, S_sc, sbuf, *, nv):
    C = RWKV_CHUNK
    nc = -(-nv // C)
    TB = nc * C
    t = pl.program_id(1)

    @pl.when(t == 0)
    def _init():
        S_sc[...] = S0_ref[0]
        sbuf[7:8] = sh0_ref[0]

    sbuf[8:8 + nv] = pb_ref[...]
    if nv < TB:
        sbuf[8 + nv:8 + TB] = jnp.zeros((TB - nv, B_COLS_PAD), F32)
    pb = sbuf[8:8 + TB]
    prev = sbuf[7:7 + TB]
    last_row = sbuf[7 + nv:8 + nv]
    xm = pb + (prev - pb) * mu_ref[...]
    W = B_WIDTH
    r = xm[:, 0:W]
    kb = xm[:, W:2 * W]
    vb = xm[:, 2 * W:3 * W]
    wl = xm[:, 3 * W:3 * W + LORA_PAD]
    al = xm[:, 3 * W + LORA_PAD:3 * W + 2 * LORA_PAD]
    gl = xm[:, 3 * W + 2 * LORA_PAD:]
    seg_sum = lambda z: _dot(z.astype(BF16), bd_ref[...])
    w_log = -_softplus(-(w0_ref[...] + _dot(jnp.tanh(wl).astype(BF16), ww2_ref[...]))) - 0.5
    lw = -jnp.exp(w_log)
    a = _sigmoid(a0_ref[...] + _dot(al.astype(BF16), wa2_ref[...]))
    g = _dot(_sigmoid(gl).astype(BF16), wg2_ref[...])
    kkr = kb * kk_ref[...]
    kk = kkr / jnp.maximum(jnp.sqrt(seg_sum(kkr * kkr)), 1e-12)
    k = kb * (1.0 + (a - 1.0) * ka_ref[...])
    bonus = seg_sum(r * k * rk_ref[...]) * vb
    if nv < TB:
        valid = lax.broadcasted_iota(I32, (TB, W), 0) < nv
        zero = lambda z: jnp.where(valid, z, 0.0)
        lw, kk, k, vb = zero(lw), zero(kk), zero(k), zero(vb)

    lane = lax.broadcasted_iota(I32, (C, PAIR), 1)
    h0 = lane < B_HDIM
    sep = lambda z: jnp.concatenate([jnp.where(h0, z, 0.0), jnp.where(h0, 0.0, z)], axis=0)
    ri = lax.broadcasted_iota(I32, (C, C), 0)
    ci = lax.broadcasted_iota(I32, (C, C), 1)
    tril_c = jnp.where(ri >= ci, 1.0, 0.0).astype(BF16)
    r2 = lax.broadcasted_iota(I32, (PAIR, PAIR), 0)
    c2 = lax.broadcasted_iota(I32, (PAIR, PAIR), 1)
    low_s = r2 > c2
    low_i = r2 >= c2
    eye = jnp.where(r2 == c2, 1.0, 0.0)
    npair = B_HEADS // 2
    units = [(c, p) for c in range(nc) for p in range(npair)]
    cat0 = lambda x, y: jnp.concatenate([x, y], axis=0)
    cat1 = lambda x, y: jnp.concatenate([x, y], axis=1)
    bf = lambda z: z.astype(BF16)
    lhs, rhs_t, vs, wc = [], [], [], []
    for (c, p) in units:
        rs = slice(c * C, (c + 1) * C)
        ls = slice(p * PAIR, (p + 1) * PAIR)
        lwc = lw[rs, ls]
        b = _dotx_l(tril_c, lwc)
        emb = jnp.exp(-b)
        kkc = kk[rs, ls]
        lhs.append(bf(cat0(sep(kkc * jnp.exp(b - lwc)), sep(r[rs, ls] * jnp.exp(b)))))
        rhs_t.append(bf(cat0(sep(k[rs, ls] * emb), sep(kkc * a[rs, ls] * emb))))
        vs.append(bf(sep(vb[rs, ls])))
        wc.append(jnp.exp(b[C - 1:C]))
    A = [_dot_nt(l_, r_) for l_, r_ in zip(lhs, rhs_t)]
    Ak = [jnp.where(low_s, x[:PAIR, :PAIR], 0.0) for x in A]
    Ab = [jnp.where(low_s, x[:PAIR, PAIR:], 0.0) for x in A]
    Ark = [jnp.where(low_i, x[PAIR:, :PAIR], 0.0) for x in A]
    Arb = [jnp.where(low_i, x[PAIR:, PAIR:], 0.0) for x in A]
    X2 = [_dot(bf(cat0(x, y)), v_) for x, y, v_ in zip(Ak, Ark, vs)]
    VtK = [_dot_tn(v_, r_[:PAIR]) for v_, r_ in zip(vs, rhs_t)]
    Tm = [eye - x for x in Ab]
    P = [_dot(bf(x), bf(x)) for x in Ab]
    n = 2
    while 2 * n < C:
        X = [_dot(bf(cat0(t_, p_)), bf(p_)) for t_, p_ in zip(Tm, P)]
        Tm = [t_ + x[:PAIR] for t_, x in zip(Tm, X)]
        P = [x[PAIR:] for x in X]
        n *= 2
    Tm = [t_ + _dot(bf(t_), bf(p_)) for t_, p_ in zip(Tm, P)]
    Y = [_dot(bf(t_), cat1(l_[:PAIR], bf(x[:PAIR]))) for t_, l_, x in zip(Tm, lhs, X2)]
    Z = [_dot(bf(x), bf(y)) for x, y in zip(Arb, Y)]
    Rp = [bf(l_[PAIR:].astype(F32) - z[:, :PAIR]) for l_, z in zip(lhs, Z)]
    Op = [x[PAIR:] - z[:, PAIR:] for x, z in zip(X2, Z)]
    W = [_dot_tn(bf(y), r_[PAIR:]) for y, r_ in zip(Y, rhs_t)]
    Mc = [bf(w_[:PAIR] * c_) for w_, c_ in zip(W, wc)]
    Nn = [(v_ - w_[PAIR:]) * c_ for v_, w_, c_ in zip(VtK, W, wc)]
    o_rows = []
    for c in range(nc):
        S = [S_sc[p] for p in range(npair)]
        Sb = [bf(s_) for s_ in S]
        u0 = c * npair
        O = [_dot_nt(Rp[u0 + p], Sb[p]) + Op[u0 + p] for p in range(npair)]
        for p in range(npair):
            S_sc[p] = S[p] * wc[u0 + p] - _dot(Sb[p], Mc[u0 + p]) + Nn[u0 + p]
        o_rows.append(jnp.concatenate([x[:C] + x[C:] for x in O], axis=1))
    o = o_rows[0] if nc == 1 else jnp.concatenate(o_rows, axis=0)

    mean = seg_sum(o) * (1.0 / B_HDIM)
    d = o - mean
    var = seg_sum(d * d) * (1.0 / B_HDIM)
    hb = (d * lax.rsqrt(var + GN_EPS) * gw_ref[...] + gb_ref[...] + bonus) * g
    hb_o[...] = hb[:nv].astype(BF16)
    sbuf[7:8] = last_row

    @pl.when(t == pl.num_programs(1) - 1)
    def _fin():
        S1_o[0] = S_sc[...]
        sh1_o[0] = last_row


def _rwkv(pb, S0, sh0, mu, w0, ww2, a0, wa2, wg2, k_k, k_a, r_k, gn_w, gn_b, bd, *, n_seq, L, row_off):
    nv = min(L, RWKV_CHUNK * RWKV_NCHUNK)
    nst = L // nv
    off = row_off // nv
    TB = -(-nv // RWKV_CHUNK) * RWKV_CHUNK
    tok = lambda n: pl.BlockSpec((nv, n), lambda s, t: (off + s * nst + t, 0))
    st = lambda shp: pl.BlockSpec((1,) + shp, lambda s, t: (s,) + (0,) * len(shp))
    full = lambda a: pl.BlockSpec(a.shape, lambda s, t: (0,) * a.ndim)
    consts = [mu, w0, ww2, a0, wa2, wg2, k_k, k_a, r_k, gn_w, gn_b, bd]
    npair = B_HEADS // 2
    in_specs = [tok(B_COLS_PAD), st((npair, PAIR, PAIR)), st((1, B_COLS_PAD))] + [full(a) for a in consts]
    args = [pb, S0, sh0] + consts
    out_shape = [jax.ShapeDtypeStruct((n_seq * L, B_WIDTH), BF16),
                 jax.ShapeDtypeStruct((n_seq, npair, PAIR, PAIR), F32),
                 jax.ShapeDtypeStruct((n_seq, 1, B_COLS_PAD), F32)]
    out_specs = [pl.BlockSpec((nv, B_WIDTH), lambda s, t: (s * nst + t, 0)),
                 st((npair, PAIR, PAIR)), st((1, B_COLS_PAD))]
    return pl.pallas_call(
        functools.partial(_rwkv_kernel, nv=nv),
        grid=(n_seq, nst),
        in_specs=in_specs, out_specs=out_specs, out_shape=out_shape,
        scratch_shapes=[pltpu.VMEM((npair, PAIR, PAIR), F32), pltpu.VMEM((TB + 8, B_COLS_PAD), F32)],
        compiler_params=_cparams(("arbitrary", "arbitrary")),
        name="rwkv",
    )(*args)


SUB = 8


def _rows_to_tiles(ref, x):
    rows = x.shape[0]
    for j in range(SUB):
        ref[pl.ds(j, rows, stride=SUB), :] = x[:, j * LANES:(j + 1) * LANES]


def _tiles_to_rows(ref, rows):
    return jnp.concatenate([ref[pl.ds(j, rows, stride=SUB), :] for j in range(SUB)], axis=1)


def _row_tile(ref, r8):
    return ref.at[pl.ds(pl.multiple_of(r8, SUB), SUB)]


def _out_router_kernel(hap_ref, has_ref, hbp_ref, hbs_ref, xp_ref, xs_ref, woa_ref, wob_ref, nf_ref, wr_ref, br_ref,
                       h1_o, u2_o, ids_o, gates_o, rank_o, cnt_o, carry, *, ntp):
    i = pl.program_id(0)

    @pl.when(i == 0)
    def _init():
        carry[...] = jnp.zeros_like(carry)

    ha = _pick(i, ntp, hap_ref, has_ref)
    hb = _pick(i, ntp, hbp_ref, hbs_ref)
    h1 = _pick(i, ntp, xp_ref, xs_ref) + _dot(ha, woa_ref[...]) + _dot(hb, wob_ref[...])
    h1_o[...] = h1
    u2 = _rms(h1, nf_ref[...])
    _rows_to_tiles(u2_o, u2)
    E = br_ref.shape[0]
    logits = _dot_nt(wr_ref[...], u2.astype(BF16)) + br_ref[...]
    TM = logits.shape[1]
    row_f = lax.broadcasted_iota(I32, (E, TM), 0).astype(F32)
    work = logits
    vals, idxs = [], []
    for _ in range(TOP_K):
        m = jnp.max(work, axis=0, keepdims=True)
        idx = jnp.min(jnp.where(work == m, row_f, float(E)), axis=0, keepdims=True)
        vals.append(m)
        idxs.append(idx)
        work = jnp.where(row_f == idx, -jnp.inf, work)
    es = [jnp.exp(v - vals[0]) for v in vals]
    tot = es[0] + es[1] + es[2] + es[3]
    gates = [e_ / tot for e_ in es]
    member = jnp.zeros((E, TM), F32)
    for kk in range(TOP_K):
        member = member + jnp.where(row_f == idxs[kk], 1.0, 0.0)
    r2 = lax.broadcasted_iota(I32, (TM, TM), 0)
    c2 = lax.broadcasted_iota(I32, (TM, TM), 1)
    before = _dot(member.astype(BF16), jnp.where(r2 < c2, 1.0, 0.0).astype(BF16)) + carry[:, 0:1]
    ranks = [jnp.sum(jnp.where(row_f == idxs[kk], before, 0.0), axis=0, keepdims=True) for kk in range(TOP_K)]
    pad = jnp.zeros((SUB - TOP_K, TM), F32)
    rows8 = lambda parts: jnp.concatenate(parts + [pad], axis=0)
    ids_o[...] = rows8(idxs).astype(I32)
    rank_o[...] = rows8(ranks).astype(I32)
    g8 = rows8(gates)
    sel = jnp.where(lax.broadcasted_iota(I32, (SUB, LANES), 0) == lax.broadcasted_iota(I32, (SUB, LANES), 1),
                    1.0, 0.0).astype(BF16)
    gates_o[...] = _dotx_l(sel, g8, f=lambda a, b: _dot_tn(b, a))
    carry[...] = carry[...] + jnp.sum(member, axis=1, keepdims=True)
    cnt_o[...] = carry[...].astype(I32)


def _out_router(hap, has, hbp, hbs, xp, xs, woa, wob, nf, wr, br):
    D = xp.shape[1]
    T = xp.shape[0] + xs.shape[0]
    ntp = xp.shape[0] // ROW_TILE
    nt = T // ROW_TILE
    row = lambda n: pl.BlockSpec((ROW_TILE, n), lambda i: (i, 0))
    full = lambda a: pl.BlockSpec(a.shape, lambda i: (0, 0))
    cnt = jax.ShapeDtypeStruct((br.shape[0], LANES), I32)
    return pl.pallas_call(
        functools.partial(_out_router_kernel, ntp=ntp),
        grid=(nt,),
        in_specs=_two_specs(ntp, A_WIDTH) + _two_specs(ntp, B_WIDTH) + _two_specs(ntp, D)
        + [full(woa), full(wob), full(nf), full(wr), full(br)],
        out_specs=[row(D), pl.BlockSpec((ROW_TILE * SUB, LANES), lambda i: (i, 0)),
                   pl.BlockSpec((SUB, ROW_TILE), lambda i: (0, i)), row(LANES),
                   pl.BlockSpec((SUB, ROW_TILE), lambda i: (0, i)), full(cnt)],
        out_shape=[jax.ShapeDtypeStruct((T, D), F32), jax.ShapeDtypeStruct((T * SUB, LANES), F32),
                   jax.ShapeDtypeStruct((SUB, T), I32), jax.ShapeDtypeStruct((T, LANES), F32),
                   jax.ShapeDtypeStruct((SUB, T), I32), cnt],
        scratch_shapes=[pltpu.VMEM(cnt.shape, F32)],
        compiler_params=_cparams(("arbitrary",)),
        name="out_router",
    )(hap, has, hbp, hbs, xp, xs, woa, wob, nf, wr, br)


MOE_BLOCK = 512


def _row_copy_wait(buf, sem, n):
    for _ in range(n):
        pltpu.make_async_copy(buf, buf, sem).wait()


DMA_UNROLL = 4


SCATTER_BUFS = 3


def _scatter_kernel(plo_ref, phi_ref, nu_ref, dest_ref, u_hbm, xs_o, zbuf, ubuf, sem, in_sems, out_sems):
    TM = ROW_TILE
    BLK = MOE_BLOCK * SUB
    NB = SCATTER_BUFS
    t = pl.program_id(0)
    nt = pl.num_programs(0)

    def fetch(i, slot):
        src = u_hbm.at[pl.ds(pl.multiple_of(i * (TM * SUB), TM * SUB), TM * SUB)]
        return pltpu.make_async_copy(src, ubuf.at[slot], in_sems.at[slot])

    @pl.when(t == 0)
    def _first_fetch():
        fetch(0, 0).start()

    @pl.when(t == 0)
    def _zero_unused_slots():
        zbuf[...] = jnp.zeros_like(zbuf)
        zrow = zbuf.at[pl.ds(0, SUB)]

        def per_expert(e, n):
            def per_row(rw, _):
                pltpu.make_async_copy(zrow, _row_tile(xs_o, rw * SUB), sem).start()
                return 0

            lax.fori_loop(plo_ref[e], phi_ref[e], per_row, 0)
            return n + (phi_ref[e] - plo_ref[e])

        n_rows = lax.fori_loop(0, plo_ref.shape[0], per_expert, 0)

        def wait_row(_, c):
            pltpu.make_async_copy(zrow, xs_o.at[pl.ds(0, SUB)], sem).wait()
            return c

        lax.fori_loop(0, n_rows, wait_row, 0)

        def per_block(j, _):
            cp = pltpu.make_async_copy(zbuf, xs_o.at[pl.ds(pl.multiple_of(j * BLK, BLK), BLK)], sem)
            cp.start()
            cp.wait()
            return 0

        lax.fori_loop(nu_ref[0], xs_o.shape[0] // BLK, per_block, 0)

    slot = lax.rem(t, NB)
    nslot = lax.rem(t + 1, NB)

    @pl.when(t + 1 < nt)
    def _next_fetch():
        @pl.when(t + 1 >= NB)
        def _buffer_free():
            _row_copy_wait(ubuf.at[nslot], out_sems.at[nslot], TOP_K)

        fetch(t + 1, nslot).start()

    fetch(t, slot).wait()
    cur = ubuf.at[slot]

    def body(i, _):
        src = _row_tile(cur, i * SUB)
        for kk in range(TOP_K):
            pltpu.make_async_copy(src, _row_tile(xs_o, dest_ref[i * TOP_K + kk]), out_sems.at[slot]).start(priority=kk % 2)
        return 0

    lax.fori_loop(0, TM, body, 0, unroll=DMA_UNROLL)

    @pl.when(t == nt - 1)
    def _drain():
        for s in range(NB):
            _row_copy_wait(ubuf.at[s], out_sems.at[s], TOP_K)


def _scatter_rows(pad_lo, pad_hi, n_used, dest8_flat, u2t, n_slots):
    T = u2t.shape[0] // SUB
    assert T // ROW_TILE >= SCATTER_BUFS
    grid_spec = pltpu.PrefetchScalarGridSpec(
        num_scalar_prefetch=3,
        grid=(T // ROW_TILE,),
        in_specs=[pl.BlockSpec((ROW_TILE * TOP_K,), lambda i, *_: (i,), memory_space=pltpu.SMEM),
                  pl.BlockSpec(memory_space=pl.ANY)],
        out_specs=pl.BlockSpec(memory_space=pl.ANY),
        scratch_shapes=[pltpu.VMEM((MOE_BLOCK * SUB, LANES), F32), pltpu.VMEM((SCATTER_BUFS, ROW_TILE * SUB, LANES), F32),
                        pltpu.SemaphoreType.DMA(()), pltpu.SemaphoreType.DMA((SCATTER_BUFS,)),
                        pltpu.SemaphoreType.DMA((SCATTER_BUFS,))],
    )
    return pl.pallas_call(
        _scatter_kernel,
        grid_spec=grid_spec,
        out_shape=jax.ShapeDtypeStruct((n_slots * SUB, LANES), F32),
        compiler_params=_cparams(("arbitrary",)),
        name="moe_scatter",
    )(pad_lo, pad_hi, n_used, dest8_flat, u2t)


def _expert_kernel(be_ref, nu_ref, nxt_ref, xs_ref, wgu_hbm, bgu_ref, wd_hbm, bd_ref, ys_o,
                   wgu_f, wd_f, wgu_b, wd_b, slot_ref, sems):
    j = pl.program_id(0)
    F = wd_b.shape[0]
    used = j < nu_ref[0]
    e = be_ref[j]

    def w_copies(eid, slot):
        return (pltpu.make_async_copy(wgu_hbm.at[eid], wgu_f.at[slot], sems.at[0, slot]),
                pltpu.make_async_copy(wd_hbm.at[eid], wd_f.at[slot], sems.at[1, slot]))

    @pl.when(j == 0)
    def _first_fetch():
        slot_ref[0] = 0
        for cp in w_copies(e, 0):
            cp.start()

    @pl.when(used & ((j == 0) | (e != be_ref[jnp.maximum(j - 1, 0)])))
    def _new_expert():
        slot = slot_ref[0]
        for cp in w_copies(e, slot):
            cp.wait()
        nx = nxt_ref[e]

        @pl.when(nx != e)
        def _prefetch():
            for cp in w_copies(nx, 1 - slot):
                cp.start()

        wgu_b[...] = wgu_f[slot].astype(BF16)
        wd_b[...] = wd_f[slot].astype(BF16)
        slot_ref[0] = 1 - slot

    @pl.when(used)
    def _compute():
        x = _tiles_to_rows(xs_ref, MOE_BLOCK)
        gu = _dot(x.astype(BF16), wgu_b[...]) + bgu_ref[0]
        g_ = jnp.minimum(gu[:, :F], SWIGLU_LIMIT)
        up = jnp.clip(gu[:, F:], -SWIGLU_LIMIT, SWIGLU_LIMIT)
        act = (up + 1.0) * g_ * _sigmoid(SWIGLU_ALPHA * g_)
        _rows_to_tiles(ys_o, _dot(act.astype(BF16), wd_b[...]) + bd_ref[0])

    @pl.when(j >= nu_ref[0])
    def _unused():
        ys_o[...] = jnp.zeros_like(ys_o)


def _experts(blk_e, n_used, nxt_e, xs, wgu, bgu, wd, bd):
    nb = xs.shape[0] // (MOE_BLOCK * SUB)
    E, D, F2 = wgu.shape
    F = F2 // 2
    grid_spec = pltpu.PrefetchScalarGridSpec(
        num_scalar_prefetch=3,
        grid=(nb,),
        in_specs=[pl.BlockSpec((MOE_BLOCK * SUB, LANES), lambda j, be, nu, nx: (jnp.minimum(j, nu[0] - 1), 0)),
                  pl.BlockSpec(memory_space=pl.ANY),
                  pl.BlockSpec((1, 1, F2), lambda j, be, nu, nx: (be[j], 0, 0)),
                  pl.BlockSpec(memory_space=pl.ANY),
                  pl.BlockSpec((1, 1, D), lambda j, be, nu, nx: (be[j], 0, 0))],
        out_specs=pl.BlockSpec((MOE_BLOCK * SUB, LANES), lambda j, be, nu, nx: (j, 0)),
        scratch_shapes=[pltpu.VMEM((2, D, F2), F32), pltpu.VMEM((2, F, D), F32),
                        pltpu.VMEM((D, F2), BF16), pltpu.VMEM((F, D), BF16),
                        pltpu.SMEM((1,), I32), pltpu.SemaphoreType.DMA((2, 2))],
    )
    return pl.pallas_call(
        _expert_kernel,
        grid_spec=grid_spec,
        out_shape=jax.ShapeDtypeStruct(xs.shape, F32),
        compiler_params=_cparams(("arbitrary",)),
        name="moe_experts",
    )(blk_e, n_used, nxt_e, xs, wgu, bgu, wd, bd)


def _combine_kernel(dest0_ref, destn_ref, h1_ref, pp_ref, ps_ref, gates_ref, ys_ref, npl_ref, wpg_ref, wpp_ref, fn_ref,
                    yp_o, ysm_o, gbuf, ybuf, sems, *, ntp, n_tiles):
    TM = h1_ref.shape[0]
    t = pl.program_id(0)

    def fetch(dref, i, slot):
        for kk in range(TOP_K):
            src = _row_tile(ys_ref, dref[i * TOP_K + kk])
            pltpu.make_async_copy(src, _row_tile(gbuf.at[slot, kk], i * SUB), sems.at[slot]).start(priority=kk % 2)

    @pl.when(t == 0)
    def _first_fetch():
        def body(i, _):
            fetch(dest0_ref, i, 0)
            return 0

        lax.fori_loop(0, TM, body, 0, unroll=DMA_UNROLL)

    def tile(slot):
        _row_copy_wait(gbuf.at[slot, 0], sems.at[slot], TOP_K)
        for i in range(TM):
            fetch(destn_ref, i, 1 - slot)
        gates = gates_ref[...]
        h2 = h1_ref[...]
        for kk in range(TOP_K):
            h2 = h2 + _tiles_to_rows(gbuf.at[slot, kk], TM) * gates[:, kk:kk + 1]
        u3 = _rms(h2, npl_ref[...])
        gate = _sigmoid(_dot(u3.astype(BF16), wpg_ref[...]))
        p = _pick(t, ntp, pp_ref, ps_ref)
        h3 = h2 + gate * _dot(p.astype(BF16), wpp_ref[...])
        return _rms(h3, fn_ref[...])

    for slot in range(2):
        @pl.when(t % 2 == slot)
        def _tile():
            ybuf[...] = tile(slot)

    @pl.when(t < ntp)
    def _prompt_rows():
        yp_o[...] = ybuf[...]

    @pl.when(t >= ntp)
    def _sample_rows():
        ysm_o[...] = ybuf[...]

    @pl.when(t == n_tiles - 1)
    def _drain():
        other = n_tiles % 2
        _row_copy_wait(gbuf.at[other, 0], sems.at[other], TOP_K)


def _combine(dest_flat, h1, pp, ps, gates, ys, npl, wpg, wpp, fn):
    T, D = h1.shape
    nt = T // ROW_TILE
    ntp = pp.shape[0] // ROW_TILE
    row = lambda n: pl.BlockSpec((ROW_TILE, n), lambda i: (i, 0))
    full = lambda a: pl.BlockSpec(a.shape, lambda i: (0, 0))
    idx = lambda f: pl.BlockSpec((ROW_TILE * TOP_K,), f, memory_space=pltpu.SMEM)
    return pl.pallas_call(
        functools.partial(_combine_kernel, ntp=ntp, n_tiles=nt),
        grid=(nt,),
        in_specs=[idx(lambda i: (0,)), idx(lambda i: (jnp.minimum(i + 1, nt - 1),)), row(D)]
        + _two_specs(ntp, pp.shape[1])
        + [row(LANES), pl.BlockSpec(memory_space=pl.ANY), full(npl), full(wpg), full(wpp), full(fn)],
        out_specs=_two_specs(ntp, D),
        out_shape=[jax.ShapeDtypeStruct((pp.shape[0], D), F32), jax.ShapeDtypeStruct((ps.shape[0], D), F32)],
        scratch_shapes=[pltpu.VMEM((2, TOP_K, ROW_TILE * SUB, LANES), F32), pltpu.VMEM((ROW_TILE, D), F32),
                        pltpu.SemaphoreType.DMA((2,))],
        compiler_params=_cparams(("arbitrary",)),
        name="moe_combine_ple",
    )(dest_flat, dest_flat, h1, pp, ps, gates, ys, npl, wpg, wpp, fn)


def _pad_lora_cols(x):
    W3 = 3 * B_WIDTH
    z = jnp.zeros(x.shape[:-1] + (LORA_PAD - 64,), x.dtype)
    return jnp.concatenate([x[..., :W3 + 64], z, x[..., W3 + 64:W3 + 128], z, x[..., W3 + 128:]], axis=-1)


def _unpad_lora_cols(x):
    W3 = 3 * B_WIDTH
    return jnp.concatenate([x[..., :W3 + 64], x[..., W3 + LORA_PAD:W3 + LORA_PAD + 64], x[..., W3 + 2 * LORA_PAD:]], axis=-1)


def _heads_to_pairs(S):
    n = S.shape[0]
    S = S.reshape(n, B_HEADS // 2, 2, B_HDIM, B_HDIM)
    z = jnp.zeros_like(S[:, :, 0])
    top = jnp.concatenate([S[:, :, 0], z], axis=-1)
    bot = jnp.concatenate([z, S[:, :, 1]], axis=-1)
    return jnp.concatenate([top, bot], axis=-2)


def _pairs_to_heads(Sb):
    n = Sb.shape[0]
    a = Sb[:, :, :B_HDIM, :B_HDIM]
    b = Sb[:, :, B_HDIM:, B_HDIM:]
    return jnp.stack([a, b], axis=2).reshape(n, B_HEADS, B_HDIM, B_HDIM)


def kernel(x_prompt, x_sample, state_mlstm_C, state_mlstm_n, state_mlstm_m, state_mlstm_conv, state_rwkv_S, state_rwkv_shift, p_prompt, p_sample, norm_mix, w_in, conv_w, conv_b, b_ig, b_fg, a_norm, shift_mu, w0, w_w2, a0, w_a2, w_g2, k_k, k_a, r_k, gn_w, gn_b, w_out, norm_ffn, w_router, b_router, w_gu, b_gu, w_down, b_down, norm_ple, w_ple_gate, w_ple_proj, final_norm):
    assert norm_mix.shape[0] == 1, "single-layer trunk"
    nbp, Lp, D = x_prompt.shape
    nbs, Ls, _ = x_sample.shape
    Tp, Ts = nbp * Lp, nbs * Ls
    T = Tp + Ts
    row = lambda a: a.reshape(1, -1).astype(F32)

    wi = w_in[0]
    AW2 = 2 * A_WIDTH
    wqk = wi[:, :AW2].astype(BF16)
    wv = wi[:, AW2:AW2 + A_WIDTH].astype(BF16)
    wo = wi[:, AW2 + A_WIDTH:AW2 + 2 * A_WIDTH].astype(BF16)
    a_cols = 4 * A_WIDTH + 2 * A_HEADS
    wif = jnp.pad(wi[:, 4 * A_WIDTH:a_cols], ((0, 0), (0, LANES - 2 * A_HEADS)))
    wbg = jnp.concatenate([_pad_lora_cols(wi[:, a_cols:]), wif], axis=1).astype(BF16)
    gate_bias = jnp.pad(jnp.concatenate([b_ig[0], b_fg[0]]), (0, LANES - 2 * A_HEADS)).reshape(1, LANES)
    padw = lambda w: jnp.pad(w, ((0, LORA_PAD - w.shape[0]), (0, 0)))
    hid = jnp.arange(B_WIDTH) // B_HDIM
    bd = (hid[:, None] == hid[None, :]).astype(BF16)
    E = w_router.shape[-1]
    wr = w_router[0].T.astype(BF16)
    br = b_router[0].reshape(E, 1)

    assert Tp % ROW_TILE == 0 and Ts % ROW_TILE == 0
    xp, xsm = x_prompt.reshape(Tp, D), x_sample.reshape(Ts, D)
    qk, v, og, pb = _in_proj(xp, xsm, row(norm_mix), wqk, wv, wo, wbg)
    ifg = pb

    f0 = lambda *shp: jnp.zeros(shp, F32)
    ml_consts = (conv_w[0], row(conv_b), gate_bias, row(a_norm))
    m_s = jnp.pad(state_mlstm_m[0], ((0, 0), (0, LANES - A_HEADS))).reshape(nbs, 1, LANES)
    hap, Cp, np_, mp, convp = _mlstm(qk, v, og, ifg, f0(nbp, A_HEADS, A_HDIM, A_HDIM), f0(nbp, A_HEADS, A_HDIM),
                                     f0(nbp, 1, LANES), f0(nbp, CONV_W - 1, AW2), *ml_consts,
                                     n_seq=nbp, L=Lp, row_off=0)
    has, Cs, ns_, ms, convs = _mlstm(qk, v, og, ifg, state_mlstm_C[0], state_mlstm_n[0], m_s, state_mlstm_conv[0],
                                     *ml_consts, n_seq=nbs, L=Ls, row_off=Tp)

    rw_consts = (row(_pad_lora_cols(shift_mu[0])), row(w0), padw(w_w2[0]).astype(BF16), row(a0), padw(w_a2[0]).astype(BF16),
                 w_g2[0].astype(BF16),
                 row(k_k), row(k_a), row(r_k), row(gn_w), row(gn_b), bd)
    npair = B_HEADS // 2
    hbp, Sp, shp_ = _rwkv(pb, f0(nbp, npair, PAIR, PAIR), f0(nbp, 1, B_COLS_PAD), *rw_consts,
                          n_seq=nbp, L=Lp, row_off=0)
    hbs, Ss, shs = _rwkv(pb, _heads_to_pairs(state_rwkv_S[0]), _pad_lora_cols(state_rwkv_shift[0]), *rw_consts,
                         n_seq=nbs, L=Ls, row_off=Tp)

    wout = w_out[0].astype(BF16)
    h1, u2, ids, gates, rank, counts = _out_router(hap, has, hbp, hbs, xp, xsm, wout[:A_WIDTH], wout[A_WIDTH:],
                                                   row(norm_ffn), wr, br)

    counts = counts[:, 0]
    pcounts = (counts + MOE_BLOCK - 1) // MOE_BLOCK * MOE_BLOCK
    pend = jnp.cumsum(pcounts)
    pstart = pend - pcounts
    n_blk = -(-T * TOP_K // MOE_BLOCK) + E
    eid = jnp.arange(E, dtype=I32)
    base = jnp.sum(jnp.where(ids[None, :TOP_K] == eid[:, None, None], pstart[:, None, None], 0), axis=0)
    dest = ((base + rank[:TOP_K]) * SUB).T.reshape(-1).astype(I32)
    blk_start = jnp.arange(n_blk, dtype=I32) * MOE_BLOCK
    blk_e = jnp.minimum(jnp.sum(pend[None, :] <= blk_start[:, None], axis=1), E - 1).astype(I32)
    n_used = (pend[E - 1:] // MOE_BLOCK).astype(I32)

    xs = _scatter_rows((pstart + counts).astype(I32), pend.astype(I32), n_used, dest, u2, n_blk * MOE_BLOCK)
    later = jnp.where((eid[None, :] > eid[:, None]) & (pcounts[None, :] > 0), eid[None, :], E)
    nxt_e = jnp.min(later, axis=1)
    nxt_e = jnp.where(nxt_e < E, nxt_e, eid).astype(I32)
    ys = _experts(blk_e, n_used, nxt_e, xs, w_gu[0], b_gu[0][:, None, :], w_down[0], b_down[0][:, None, :])
    yp, ysm = _combine(dest, h1, p_prompt[0].reshape(Tp, -1), p_sample[0].reshape(Ts, -1), gates, ys, row(norm_ple),
                       w_ple_gate[0].astype(BF16), w_ple_proj[0].astype(BF16), row(final_norm))

    lead = lambda a: a[None]
    return (yp.reshape(nbp, Lp, D), ysm.reshape(nbs, Ls, D),
            lead(Cp), lead(np_), lead(mp[:, 0, :A_HEADS]), lead(convp), lead(_pairs_to_heads(Sp)), lead(_unpad_lora_cols(shp_)),
            lead(Cs), lead(ns_), lead(ms[:, 0, :A_HEADS]), lead(convs), lead(_pairs_to_heads(Ss)), lead(_unpad_lora_cols(shs)))
```

```python
import functools

import jax
import jax.numpy as jnp
from jax import lax
from jax.experimental import pallas as pl
from jax.experimental.pallas import tpu as pltpu

F32 = jnp.float32
BF16 = jnp.bfloat16
I32 = jnp.int32

EPS = 1e-6
GN_EPS = 64e-5
A_HEADS, A_HDIM = 4, 128
B_HEADS, B_HDIM = 8, 64
A_WIDTH = A_HEADS * A_HDIM
B_WIDTH = B_HEADS * B_HDIM
CONV_W = 4
LORA_PAD = 128
N_EXPERTS, TOP_K = 32, 4
SWIGLU_LIMIT, SWIGLU_ALPHA = 7.0, 1.702
LANES = 128
ROW_TILE = 256
VMEM_LIMIT = 56 * 1024 * 1024


def _cparams(sem):
    return pltpu.CompilerParams(dimension_semantics=sem, vmem_limit_bytes=VMEM_LIMIT)


def _dot(a, b):
    return lax.dot_general(a, b, (((1,), (0,)), ((), ())), preferred_element_type=F32)


def _dot_nt(a, b):
    return lax.dot_general(a, b, (((1,), (1,)), ((), ())), preferred_element_type=F32)


def _dot_tn(a, b):
    return lax.dot_general(a, b, (((0,), (0,)), ((), ())), preferred_element_type=F32)


def _split2(x):
    hi = x.astype(BF16)
    lo = (x - hi.astype(F32)).astype(BF16)
    return hi, lo


def _split3(x):
    hi = x.astype(BF16)
    r1 = x - hi.astype(F32)
    mid = r1.astype(BF16)
    lo = (r1 - mid.astype(F32)).astype(BF16)
    return hi, mid, lo


def _dot3(a, b, f=_dot):
    ah, al = _split2(a)
    bh, bl = _split2(b)
    return f(ah, bh) + (f(ah, bl) + f(al, bh))


def _dotx_l(a_exact, b, f=_dot, pieces=3):
    parts = _split3(b)[:pieces]
    acc = f(a_exact, parts[-1])
    for part in parts[-2::-1]:
        acc = f(a_exact, part) + acc
    return acc


def _rms(x, g):
    return x * lax.rsqrt(jnp.mean(x * x, axis=-1, keepdims=True) + EPS) * g


def _sigmoid(x):
    return 1.0 / (1.0 + jnp.exp(-x))


def _two_specs(ntp, n):
    return [pl.BlockSpec((ROW_TILE, n), lambda i: (jnp.minimum(i, ntp - 1), 0)),
            pl.BlockSpec((ROW_TILE, n), lambda i: (jnp.maximum(i - ntp, 0), 0))]


def _pick(i, ntp, p_ref, s_ref):
    return jnp.where(i < ntp, p_ref[...], s_ref[...])


def _in_proj_kernel(xp_ref, xs_ref, g_ref, wqk_ref, wv_ref, wo_ref, wbg_ref,
                    qk_o, v_o, og_o, pbg_o, *, ntp):
    x = _pick(pl.program_id(0), ntp, xp_ref, xs_ref)
    ub = _rms(x, g_ref[...]).astype(BF16)
    qk_o[...] = _dot(ub, wqk_ref[...])
    v_o[...] = _dot(ub, wv_ref[...])
    og_o[...] = _dot(ub, wo_ref[...])
    pbg_o[...] = _dot(ub, wbg_ref[...])


def _in_proj(xp, xs, g, wqk, wv, wo, wbg):
    D = xp.shape[1]
    T = xp.shape[0] + xs.shape[0]
    ntp = xp.shape[0] // ROW_TILE
    row = lambda n: pl.BlockSpec((ROW_TILE, n), lambda i: (i, 0))
    full = lambda a: pl.BlockSpec(a.shape, lambda i: (0, 0))
    outs = [wqk.shape[1], wv.shape[1], wo.shape[1], wbg.shape[1]]
    return pl.pallas_call(
        functools.partial(_in_proj_kernel, ntp=ntp),
        grid=(T // ROW_TILE,),
        in_specs=_two_specs(ntp, D) + [full(g), full(wqk), full(wv), full(wo), full(wbg)],
        out_specs=[row(n) for n in outs],
        out_shape=[jax.ShapeDtypeStruct((T, n), F32) for n in outs],
        compiler_params=_cparams(("arbitrary",)),
        name="in_proj",
    )(xp, xs, g, wqk, wv, wo, wbg)


MLSTM_CHUNK = 128
NEG_BIG = -1e30


def _log_sigmoid(x):
    return jnp.minimum(x, 0.0) - jnp.log1p(jnp.exp(-jnp.abs(x)))


MLSTM_GROUP = 1


def _mlstm_kernel(*refs, nv):
    G = MLSTM_GROUP
    CH = MLSTM_CHUNK
    tok = [refs[4 * j:4 * j + 4] for j in range(G)]
    (C0_ref, n0_ref, m0_ref, conv0_ref, cw_ref, cb_ref, gb_ref, an_ref,
     ha_o, C1_o, n1_o, m1_o, conv1_o, C_sc, n_sc, m_sc, cbuf) = refs[4 * G:]
    c = pl.program_id(1)

    @pl.when(c == 0)
    def _init():
        C_sc[...] = C0_ref[...]
        for j in range(G):
            n_sc[j, 0:A_HEADS] = n0_ref[j]
            m_sc[j, 0:1] = m0_ref[j]
            cbuf[j, 5:8] = conv0_ref[j]

    row = lax.broadcasted_iota(I32, (CH, LANES), 0)
    r2 = lax.broadcasted_iota(I32, (CH, CH), 0)
    c2 = lax.broadcasted_iota(I32, (CH, CH), 1)
    tri = r2 >= c2
    tril_b = jnp.where(tri, 1.0, 0.0).astype(BF16)
    sel = jnp.where(lax.broadcasted_iota(I32, (8, LANES), 0) == lax.broadcasted_iota(I32, (8, LANES), 1),
                    1.0, 0.0).astype(BF16)
    lane1 = lax.broadcasted_iota(I32, (1, LANES), 1)

    q, k, v, og, b, g, gT, conv_tail, m_row = [], [], [], [], [], [], [], [], []
    for j, (qk_ref, v_ref, og_ref, if_ref) in enumerate(tok):
        cb_j = cbuf.at[j]
        cb_j[8:8 + nv] = qk_ref[...]
        if nv < CH:
            cb_j[8 + nv:8 + CH] = jnp.zeros((CH - nv, 2 * A_WIDTH), F32)
        y = cb_ref[...] + cb_j[5:5 + CH] * cw_ref[0:1]
        for tap in range(1, CONV_W):
            y = y + cb_j[5 + tap:5 + tap + CH] * cw_ref[tap:tap + 1]
        conv_tail.append(cb_j[5 + nv:8 + nv])
        qk = y * _sigmoid(y)
        q.append(qk[:, :A_WIDTH])
        k.append(qk[:, A_WIDTH:] * (A_HDIM ** -0.5))
        if nv == CH:
            v.append(v_ref[...])
            og.append(og_ref[...])
            gp = if_ref[...] + gb_ref[...]
        else:
            zpad = lambda n: jnp.zeros((CH - nv, n), F32)
            v.append(jnp.concatenate([v_ref[...], zpad(A_WIDTH)], axis=0))
            og.append(jnp.concatenate([og_ref[...], zpad(A_WIDTH)], axis=0))
            gp = jnp.concatenate([if_ref[...] + gb_ref[...], zpad(LANES)], axis=0)
        li = gp
        lf = pltpu.roll(_log_sigmoid(gp), LANES - A_HEADS, 1)
        if nv < CH:
            li = jnp.where(row < nv, li, NEG_BIG)
            lf = jnp.where(row < nv, lf, 0.0)
        b.append(_dotx_l(tril_b, lf))
        g.append(li - b[j])
        gT.append(_dotx_l(sel, g[j], f=_dot_nt))
        m_row.append(m_sc[j, 0:1])
    qb = [x.astype(BF16) for x in q]
    kb = [x.astype(BF16) for x in k]
    vb = [x.astype(BF16) for x in v]

    U = [(j, h) for j in range(G) for h in range(A_HEADS)]
    sl = lambda h: slice(h * A_HDIM, (h + 1) * A_HDIM)
    bcol = [b[j][:, h:h + 1] for j, h in U]
    m_h = [m_row[j][:, h:h + 1] for j, h in U]
    C_h = [C_sc[j, h] for j, h in U]
    n_h = [n_sc[j, h:h + 1] for j, h in U]
    qk_t = [_dot_nt(qb[j][:, sl(h)], kb[j][:, sl(h)]) for j, h in U]
    qC = [_dot_nt(qb[j][:, sl(h)], C_h[u].astype(BF16)) for u, (j, h) in enumerate(U)]
    dm = [jnp.where(tri, bcol[u] + gT[j][h:h + 1, :], -jnp.inf) for u, (j, h) in enumerate(U)]
    inter = [bcol[u] + m_h[u] for u in range(len(U))]
    mt = [jnp.maximum(inter[u], jnp.max(dm[u], axis=1, keepdims=True)) for u in range(len(U))]
    s = [qk_t[u] * jnp.exp(dm[u] - mt[u]) for u in range(len(U))]
    wi = [jnp.exp(inter[u] - mt[u]) for u in range(len(U))]
    sv = [_dot(s[u].astype(BF16), vb[j][:, sl(h)]) for u, (j, h) in enumerate(U)]
    m_new = [mt[u][CH - 1:CH] for u in range(len(U))]
    w_end = [jnp.exp(bcol[u][CH - 1:CH] + g[j][:, h:h + 1] - m_new[u]) for u, (j, h) in enumerate(U)]
    dec = [jnp.exp(bcol[u][CH - 1:CH] + m_h[u] - m_new[u]) for u in range(len(U))]
    vk = [_dot_tn((v[j][:, sl(h)] * w_end[u]).astype(BF16), kb[j][:, sl(h)]) for u, (j, h) in enumerate(U)]
    outs = [[] for _ in range(G)]
    m_next = list(m_row)
    for u, (j, h) in enumerate(U):
        num = sv[u] + wi[u] * qC[u]
        den = jnp.sum(s[u], axis=1, keepdims=True) + wi[u] * jnp.sum(q[j][:, sl(h)] * n_h[u], axis=1, keepdims=True)
        hh = num / jnp.maximum(jnp.abs(den), jnp.exp(-mt[u]))
        hn = hh * lax.rsqrt(jnp.mean(hh * hh, axis=1, keepdims=True) + EPS)
        outs[j].append(hn * an_ref[:, sl(h)] * _sigmoid(og[j][:, sl(h)]))
        C_sc[j, h] = dec[u] * C_h[u] + vk[u]
        n_sc[j, h:h + 1] = dec[u] * n_h[u] + jnp.sum(k[j][:, sl(h)] * w_end[u], axis=0, keepdims=True)
        m_next[j] = jnp.where(lane1 == h, m_new[u], m_next[j])
    for j in range(G):
        m_sc[j, 0:1] = m_next[j]
        ha_o[j] = jnp.concatenate(outs[j], axis=1)[:nv].astype(BF16)
        cbuf[j, 5:8] = conv_tail[j]

    @pl.when(c == pl.num_programs(1) - 1)
    def _fin():
        C1_o[...] = C_sc[...]
        for j in range(G):
            n1_o[j] = n_sc[j, 0:A_HEADS]
            m1_o[j] = m_sc[j, 0:1]
            conv1_o[j] = conv_tail[j]


def _mlstm(qk, v, og, pbg, C0, n0, m0, conv0, cw, cb, gb, an, *, n_seq, L, row_off):
    G = MLSTM_GROUP
    CH = MLSTM_CHUNK
    assert n_seq % G == 0
    nv = min(L, CH)
    nch = L // nv
    off = row_off // nv
    gate_tile = pbg.shape[1] // LANES - 1
    st = lambda shp: pl.BlockSpec((G,) + shp, lambda s, c: (s,) + (0,) * len(shp))
    full = lambda a: pl.BlockSpec(a.shape, lambda s, c: (0,) * a.ndim)
    in_specs, args = [], []
    for j in range(G):
        rows = lambda s, c, j=j: off + (G * s + j) * nch + c
        in_specs += [pl.BlockSpec((nv, 2 * A_WIDTH), lambda s, c, r=rows: (r(s, c), 0)),
                     pl.BlockSpec((nv, A_WIDTH), lambda s, c, r=rows: (r(s, c), 0)),
                     pl.BlockSpec((nv, A_WIDTH), lambda s, c, r=rows: (r(s, c), 0)),
                     pl.BlockSpec((nv, LANES), lambda s, c, r=rows: (r(s, c), gate_tile))]
        args += [qk, v, og, pbg]
    in_specs += [st((A_HEADS, A_HDIM, A_HDIM)), st((A_HEADS, A_HDIM)), st((1, LANES)),
                 st((CONV_W - 1, 2 * A_WIDTH)), full(cw), full(cb), full(gb), full(an)]
    args += [C0, n0, m0, conv0, cw, cb, gb, an]
    out_shape = [jax.ShapeDtypeStruct((n_seq, L, A_WIDTH), BF16),
                 jax.ShapeDtypeStruct((n_seq, A_HEADS, A_HDIM, A_HDIM), F32),
                 jax.ShapeDtypeStruct((n_seq, A_HEADS, A_HDIM), F32),
                 jax.ShapeDtypeStruct((n_seq, 1, LANES), F32),
                 jax.ShapeDtypeStruct((n_seq, CONV_W - 1, 2 * A_WIDTH), F32)]
    out_specs = [pl.BlockSpec((G, nv, A_WIDTH), lambda s, c: (s, c, 0)),
                 st((A_HEADS, A_HDIM, A_HDIM)), st((A_HEADS, A_HDIM)),
                 st((1, LANES)), st((CONV_W - 1, 2 * A_WIDTH))]
    return pl.pallas_call(
        functools.partial(_mlstm_kernel, nv=nv),
        grid=(n_seq // G, nch),
        in_specs=in_specs, out_specs=out_specs, out_shape=out_shape,
        scratch_shapes=[pltpu.VMEM((G, A_HEADS, A_HDIM, A_HDIM), F32), pltpu.VMEM((G, 8, A_HDIM), F32),
                        pltpu.VMEM((G, 8, LANES), F32), pltpu.VMEM((G, CH + 8, 2 * A_WIDTH), F32)],
        compiler_params=_cparams(("arbitrary", "arbitrary")),
        name="mlstm",
    )(*args)


RWKV_CHUNK = 64
RWKV_NCHUNK = 4
B_COLS_PAD = 3 * B_WIDTH + 2 * LORA_PAD + 128
PAIR = 2 * B_HDIM


def _softplus(z):
    return jnp.maximum(z, 0.0) + jnp.log1p(jnp.exp(-jnp.abs(z)))


def _rwkv_kernel(pb_ref, S0_ref, sh0_ref, mu_ref, w0_ref, ww2_ref, a0_ref, wa2_ref, wg2_ref,
                 kk_ref, ka_ref, rk_ref, gw_ref, gb_ref, bd_ref,
                 hb_o, S1_o, sh1_o, S_sc, sbuf, *, nv):
    C = RWKV_CHUNK
    nc = -(-nv // C)
    TB = nc * C
    t = pl.program_id(1)

    @pl.when(t == 0)
    def _init():
        S_sc[...] = S0_ref[0]
        sbuf[7:8] = sh0_ref[0]

    sbuf[8:8 + nv] = pb_ref[...]
    if nv < TB:
        sbuf[8 + nv:8 + TB] = jnp.zeros((TB - nv, B_COLS_PAD), F32)
    pb = sbuf[8:8 + TB]
    prev = sbuf[7:7 + TB]
    last_row = sbuf[7 + nv:8 + nv]
    xm = pb + (prev - pb) * mu_ref[...]
    W = B_WIDTH
    r = xm[:, 0:W]
    kb = xm[:, W:2 * W]
    vb = xm[:, 2 * W:3 * W]
    wl = xm[:, 3 * W:3 * W + LORA_PAD]
    al = xm[:, 3 * W + LORA_PAD:3 * W + 2 * LORA_PAD]
    gl = xm[:, 3 * W + 2 * LORA_PAD:]
    def seg_sum(z):
        hw = bd_ref.shape[0]
        zb = z.astype(BF16)
        return jnp.concatenate([_dot(zb[:, i * hw:(i + 1) * hw], bd_ref[...]) for i in range(B_WIDTH // hw)], axis=1)
    w_log = -_softplus(-(w0_ref[...] + _dot(jnp.tanh(wl).astype(BF16), ww2_ref[...]))) - 0.5
    lw = -jnp.exp(w_log)
    a = _sigmoid(a0_ref[...] + _dot(al.astype(BF16), wa2_ref[...]))
    g = _dot(_sigmoid(gl).astype(BF16), wg2_ref[...])
    kkr = kb * kk_ref[...]
    kk = kkr / jnp.maximum(jnp.sqrt(seg_sum(kkr * kkr)), 1e-12)
    k = kb * (1.0 + (a - 1.0) * ka_ref[...])
    bonus = seg_sum(r * k * rk_ref[...]) * vb
    if nv < TB:
        valid = lax.broadcasted_iota(I32, (TB, W), 0) < nv
        zero = lambda z: jnp.where(valid, z, 0.0)
        lw, kk, k, vb = zero(lw), zero(kk), zero(k), zero(vb)

    lane = lax.broadcasted_iota(I32, (C, PAIR), 1)
    h0 = lane < B_HDIM
    sep = lambda z: jnp.concatenate([jnp.where(h0, z, 0.0), jnp.where(h0, 0.0, z)], axis=0)
    ri = lax.broadcasted_iota(I32, (C, C), 0)
    ci = lax.broadcasted_iota(I32, (C, C), 1)
    tril_c = jnp.where(ri >= ci, 1.0, 0.0).astype(BF16)
    r2 = lax.broadcasted_iota(I32, (PAIR, PAIR), 0)
    c2 = lax.broadcasted_iota(I32, (PAIR, PAIR), 1)
    low_s = r2 > c2
    low_i = r2 >= c2
    eye = jnp.where(r2 == c2, 1.0, 0.0)
    npair = B_HEADS // 2
    units = [(c, p) for c in range(nc) for p in range(npair)]
    cat0 = lambda x, y: jnp.concatenate([x, y], axis=0)
    cat1 = lambda x, y: jnp.concatenate([x, y], axis=1)
    bf = lambda z: z.astype(BF16)
    lhs, rhs_t, vs, wc = [], [], [], []
    bcum = [_dotx_l(tril_c, lw[c * C:(c + 1) * C], pieces=2) for c in range(nc)]
    for (c, p) in units:
        rs = slice(c * C, (c + 1) * C)
        ls = slice(p * PAIR, (p + 1) * PAIR)
        lwc = lw[rs, ls]
        b = bcum[c][:, ls]
        emb = jnp.exp(-b)
        kkc = kk[rs, ls]
        lhs.append(bf(cat0(sep(kkc * jnp.exp(b - lwc)), sep(r[rs, ls] * jnp.exp(b)))))
        rhs_t.append(bf(cat0(sep(k[rs, ls] * emb), sep(kkc * a[rs, ls] * emb))))
        vs.append(bf(sep(vb[rs, ls])))
        wc.append(jnp.exp(b[C - 1:C]))
    A = [_dot_nt(l_, r_) for l_, r_ in zip(lhs, rhs_t)]
    Ak = [jnp.where(low_s, x[:PAIR, :PAIR], 0.0) for x in A]
    Ab = [jnp.where(low_s, x[:PAIR, PAIR:], 0.0) for x in A]
    Ark = [jnp.where(low_i, x[PAIR:, :PAIR], 0.0) for x in A]
    Arb = [jnp.where(low_i, x[PAIR:, PAIR:], 0.0) for x in A]
    X2 = [_dot(bf(cat0(x, y)), v_) for x, y, v_ in zip(Ak, Ark, vs)]
    VtK = [_dot_tn(v_, r_[:PAIR]) for v_, r_ in zip(vs, rhs_t)]
    Tm = [eye - x for x in Ab]
    P = [_dot(bf(x), bf(x)) for x in Ab]
    n = 2
    while 2 * n < C:
        X = [_dot(bf(cat0(t_, p_)), bf(p_)) for t_, p_ in zip(Tm, P)]
        Tm = [t_ + x[:PAIR] for t_, x in zip(Tm, X)]
        P = [x[PAIR:] for x in X]
        n *= 2
    Tm = [t_ + _dot(bf(t_), bf(p_)) for t_, p_ in zip(Tm, P)]
    Y = [_dot(bf(t_), cat1(l_[:PAIR], bf(x[:PAIR]))) for t_, l_, x in zip(Tm, lhs, X2)]
    Z = [_dot(bf(x), bf(y)) for x, y in zip(Arb, Y)]
    Rp = [bf(l_[PAIR:].astype(F32) - z[:, :PAIR]) for l_, z in zip(lhs, Z)]
    Op = [x[PAIR:] - z[:, PAIR:] for x, z in zip(X2, Z)]
    W = [_dot_tn(bf(y), r_[PAIR:]) for y, r_ in zip(Y, rhs_t)]
    Mc = [bf(w_[:PAIR] * c_) for w_, c_ in zip(W, wc)]
    Nn = [(v_ - w_[PAIR:]) * c_ for v_, w_, c_ in zip(VtK, W, wc)]
    o_rows = []
    for c in range(nc):
        S = [S_sc[p] for p in range(npair)]
        Sb = [bf(s_) for s_ in S]
        u0 = c * npair
        O = [_dot_nt(Rp[u0 + p], Sb[p]) + Op[u0 + p] for p in range(npair)]
        for p in range(npair):
            S_sc[p] = S[p] * wc[u0 + p] - _dot(Sb[p], Mc[u0 + p]) + Nn[u0 + p]
        o_rows.append(jnp.concatenate([x[:C] + x[C:] for x in O], axis=1))
    o = o_rows[0] if nc == 1 else jnp.concatenate(o_rows, axis=0)

    mean = seg_sum(o) * (1.0 / B_HDIM)
    d = o - mean
    var = seg_sum(d * d) * (1.0 / B_HDIM)
    hb = (d * lax.rsqrt(var + GN_EPS) * gw_ref[...] + gb_ref[...] + bonus) * g
    hb_o[...] = hb[:nv].astype(BF16)
    sbuf[7:8] = last_row

    @pl.when(t == pl.num_programs(1) - 1)
    def _fin():
        S1_o[0] = S_sc[...]
        sh1_o[0] = last_row


def _rwkv(pb, S0, sh0, mu, w0, ww2, a0, wa2, wg2, k_k, k_a, r_k, gn_w, gn_b, bd, *, n_seq, L, row_off):
    nv = min(L, RWKV_CHUNK * RWKV_NCHUNK)
    nst = L // nv
    off = row_off // nv
    TB = -(-nv // RWKV_CHUNK) * RWKV_CHUNK
    tok = lambda n: pl.BlockSpec((nv, n), lambda s, t: (off + s * nst + t, 0))
    st = lambda shp: pl.BlockSpec((1,) + shp, lambda s, t: (s,) + (0,) * len(shp))
    full = lambda a: pl.BlockSpec(a.shape, lambda s, t: (0,) * a.ndim)
    consts = [mu, w0, ww2, a0, wa2, wg2, k_k, k_a, r_k, gn_w, gn_b, bd]
    npair = B_HEADS // 2
    in_specs = [tok(B_COLS_PAD), st((npair, PAIR, PAIR)), st((1, B_COLS_PAD))] + [full(a) for a in consts]
    args = [pb, S0, sh0] + consts
    out_shape = [jax.ShapeDtypeStruct((n_seq * L, B_WIDTH), BF16),
                 jax.ShapeDtypeStruct((n_seq, npair, PAIR, PAIR), F32),
                 jax.ShapeDtypeStruct((n_seq, 1, B_COLS_PAD), F32)]
    out_specs = [pl.BlockSpec((nv, B_WIDTH), lambda s, t: (s * nst + t, 0)),
                 st((npair, PAIR, PAIR)), st((1, B_COLS_PAD))]
    return pl.pallas_call(
        functools.partial(_rwkv_kernel, nv=nv),
        grid=(n_seq, nst),
        in_specs=in_specs, out_specs=out_specs, out_shape=out_shape,
        scratch_shapes=[pltpu.VMEM((npair, PAIR, PAIR), F32), pltpu.VMEM((TB + 8, B_COLS_PAD), F32)],
        compiler_params=_cparams(("arbitrary", "arbitrary")),
        name="rwkv",
    )(*args)


SUB = 8


def _rows_to_tiles(ref, x):
    rows = x.shape[0]
    for j in range(SUB):
        ref[pl.ds(j, rows, stride=SUB), :] = x[:, j * LANES:(j + 1) * LANES]


def _tiles_to_rows(ref, rows):
    return jnp.concatenate([ref[pl.ds(j, rows, stride=SUB), :] for j in range(SUB)], axis=1)


def _row_tile(ref, r8):
    return ref.at[pl.ds(pl.multiple_of(r8, SUB), SUB)]


def _out_router_kernel(hap_ref, has_ref, hbp_ref, hbs_ref, xp_ref, xs_ref, woa_ref, wob_ref, nf_ref, wr_ref, br_ref,
                       h1_o, u2_o, ids_o, gates_o, rank_o, cnt_o, carry, *, ntp):
    i = pl.program_id(0)

    @pl.when(i == 0)
    def _init():
        carry[...] = jnp.zeros_like(carry)

    ha = _pick(i, ntp, hap_ref, has_ref)
    hb = _pick(i, ntp, hbp_ref, hbs_ref)
    h1 = _pick(i, ntp, xp_ref, xs_ref) + _dot(ha, woa_ref[...]) + _dot(hb, wob_ref[...])
    h1_o[...] = h1
    u2 = _rms(h1, nf_ref[...])
    _rows_to_tiles(u2_o, u2)
    E = br_ref.shape[0]
    logits = _dot_nt(wr_ref[...], u2.astype(BF16)) + br_ref[...]
    TM = logits.shape[1]
    row_f = lax.broadcasted_iota(I32, (E, TM), 0).astype(F32)
    work = logits
    vals, idxs = [], []
    for _ in range(TOP_K):
        m = jnp.max(work, axis=0, keepdims=True)
        idx = jnp.min(jnp.where(work == m, row_f, float(E)), axis=0, keepdims=True)
        vals.append(m)
        idxs.append(idx)
        work = jnp.where(row_f == idx, -jnp.inf, work)
    es = [jnp.exp(v - vals[0]) for v in vals]
    tot = es[0] + es[1] + es[2] + es[3]
    gates = [e_ / tot for e_ in es]
    member = jnp.zeros((E, TM), F32)
    for kk in range(TOP_K):
        member = member + jnp.where(row_f == idxs[kk], 1.0, 0.0)
    r2 = lax.broadcasted_iota(I32, (TM, TM), 0)
    c2 = lax.broadcasted_iota(I32, (TM, TM), 1)
    before = _dot(member.astype(BF16), jnp.where(r2 < c2, 1.0, 0.0).astype(BF16)) + carry[:, 0:1]
    ranks = [jnp.sum(jnp.where(row_f == idxs[kk], before, 0.0), axis=0, keepdims=True) for kk in range(TOP_K)]
    pad = jnp.zeros((SUB - TOP_K, TM), F32)
    rows8 = lambda parts: jnp.concatenate(parts + [pad], axis=0)
    ids_o[...] = rows8(idxs).astype(I32)
    rank_o[...] = rows8(ranks).astype(I32)
    g8 = rows8(gates)
    sel = jnp.where(lax.broadcasted_iota(I32, (SUB, LANES), 0) == lax.broadcasted_iota(I32, (SUB, LANES), 1),
                    1.0, 0.0).astype(BF16)
    gates_o[...] = _dotx_l(sel, g8, f=lambda a, b: _dot_tn(b, a))
    carry[...] = carry[...] + jnp.sum(member, axis=1, keepdims=True)
    cnt_o[...] = carry[...].astype(I32)


def _out_router(hap, has, hbp, hbs, xp, xs, woa, wob, nf, wr, br):
    D = xp.shape[1]
    T = xp.shape[0] + xs.shape[0]
    ntp = xp.shape[0] // ROW_TILE
    nt = T // ROW_TILE
    row = lambda n: pl.BlockSpec((ROW_TILE, n), lambda i: (i, 0))
    full = lambda a: pl.BlockSpec(a.shape, lambda i: (0, 0))
    cnt = jax.ShapeDtypeStruct((br.shape[0], LANES), I32)
    return pl.pallas_call(
        functools.partial(_out_router_kernel, ntp=ntp),
        grid=(nt,),
        in_specs=_two_specs(ntp, A_WIDTH) + _two_specs(ntp, B_WIDTH) + _two_specs(ntp, D)
        + [full(woa), full(wob), full(nf), full(wr), full(br)],
        out_specs=[row(D), pl.BlockSpec((ROW_TILE * SUB, LANES), lambda i: (i, 0)),
                   pl.BlockSpec((SUB, ROW_TILE), lambda i: (0, i)), row(LANES),
                   pl.BlockSpec((SUB, ROW_TILE), lambda i: (0, i)), full(cnt)],
        out_shape=[jax.ShapeDtypeStruct((T, D), F32), jax.ShapeDtypeStruct((T * SUB, LANES), F32),
                   jax.ShapeDtypeStruct((SUB, T), I32), jax.ShapeDtypeStruct((T, LANES), F32),
                   jax.ShapeDtypeStruct((SUB, T), I32), cnt],
        scratch_shapes=[pltpu.VMEM(cnt.shape, F32)],
        compiler_params=_cparams(("arbitrary",)),
        name="out_router",
    )(hap, has, hbp, hbs, xp, xs, woa, wob, nf, wr, br)


MOE_BLOCK = 512


def _row_copy_wait(buf, sem, n):
    for _ in range(n):
        pltpu.make_async_copy(buf, buf, sem).wait()


DMA_UNROLL = 4


SCATTER_BUFS = 3


def _scatter_kernel(plo_ref, phi_ref, nu_ref, dest_ref, u_hbm, xs_o, zbuf, ubuf, sem, in_sems, out_sems):
    TM = ROW_TILE
    BLK = MOE_BLOCK * SUB
    NB = SCATTER_BUFS
    t = pl.program_id(0)
    nt = pl.num_programs(0)

    def fetch(i, slot):
        src = u_hbm.at[pl.ds(pl.multiple_of(i * (TM * SUB), TM * SUB), TM * SUB)]
        return pltpu.make_async_copy(src, ubuf.at[slot], in_sems.at[slot])

    @pl.when(t == 0)
    def _first_fetch():
        fetch(0, 0).start()

    @pl.when(t == 0)
    def _zero_unused_slots():
        zbuf[...] = jnp.zeros_like(zbuf)
        zrow = zbuf.at[pl.ds(0, SUB)]

        def per_expert(e, n):
            def per_row(rw, _):
                pltpu.make_async_copy(zrow, _row_tile(xs_o, rw * SUB), sem).start()
                return 0

            lax.fori_loop(plo_ref[e], phi_ref[e], per_row, 0)
            return n + (phi_ref[e] - plo_ref[e])

        n_rows = lax.fori_loop(0, plo_ref.shape[0], per_expert, 0)

        def wait_row(_, c):
            pltpu.make_async_copy(zrow, xs_o.at[pl.ds(0, SUB)], sem).wait()
            return c

        lax.fori_loop(0, n_rows, wait_row, 0)

        def per_block(j, _):
            cp = pltpu.make_async_copy(zbuf, xs_o.at[pl.ds(pl.multiple_of(j * BLK, BLK), BLK)], sem)
            cp.start()
            cp.wait()
            return 0

        lax.fori_loop(nu_ref[0], xs_o.shape[0] // BLK, per_block, 0)

    slot = lax.rem(t, NB)
    nslot = lax.rem(t + 1, NB)

    @pl.when(t + 1 < nt)
    def _next_fetch():
        @pl.when(t + 1 >= NB)
        def _buffer_free():
            _row_copy_wait(ubuf.at[nslot], out_sems.at[nslot], TOP_K)

        fetch(t + 1, nslot).start()

    fetch(t, slot).wait()
    cur = ubuf.at[slot]

    def body(i, _):
        src = _row_tile(cur, i * SUB)
        for kk in range(TOP_K):
            pltpu.make_async_copy(src, _row_tile(xs_o, dest_ref[i * TOP_K + kk]), out_sems.at[slot]).start(priority=kk % 2)
        return 0

    lax.fori_loop(0, TM, body, 0, unroll=DMA_UNROLL)

    @pl.when(t == nt - 1)
    def _drain():
        for s in range(NB):
            _row_copy_wait(ubuf.at[s], out_sems.at[s], TOP_K)


def _scatter_rows(pad_lo, pad_hi, n_used, dest8_flat, u2t, n_slots):
    T = u2t.shape[0] // SUB
    assert T // ROW_TILE >= SCATTER_BUFS
    grid_spec = pltpu.PrefetchScalarGridSpec(
        num_scalar_prefetch=3,
        grid=(T // ROW_TILE,),
        in_specs=[pl.BlockSpec((ROW_TILE * TOP_K,), lambda i, *_: (i,), memory_space=pltpu.SMEM),
                  pl.BlockSpec(memory_space=pl.ANY)],
        out_specs=pl.BlockSpec(memory_space=pl.ANY),
        scratch_shapes=[pltpu.VMEM((MOE_BLOCK * SUB, LANES), F32), pltpu.VMEM((SCATTER_BUFS, ROW_TILE * SUB, LANES), F32),
                        pltpu.SemaphoreType.DMA(()), pltpu.SemaphoreType.DMA((SCATTER_BUFS,)),
                        pltpu.SemaphoreType.DMA((SCATTER_BUFS,))],
    )
    return pl.pallas_call(
        _scatter_kernel,
        grid_spec=grid_spec,
        out_shape=jax.ShapeDtypeStruct((n_slots * SUB, LANES), F32),
        compiler_params=_cparams(("arbitrary",)),
        name="moe_scatter",
    )(pad_lo, pad_hi, n_used, dest8_flat, u2t)


def _expert_kernel(be_ref, nu_ref, nxt_ref, xs_ref, wgu_hbm, bgu_ref, wd_hbm, bd_ref, ys_o,
                   wgu_f, wd_f, wgu_b, wd_b, slot_ref, sems):
    j = pl.program_id(0)
    F = wd_b.shape[0]
    used = j < nu_ref[0]
    e = be_ref[j]

    def w_copies(eid, slot):
        return (pltpu.make_async_copy(wgu_hbm.at[eid], wgu_f.at[slot], sems.at[0, slot]),
                pltpu.make_async_copy(wd_hbm.at[eid], wd_f.at[slot], sems.at[1, slot]))

    @pl.when(j == 0)
    def _first_fetch():
        slot_ref[0] = 0
        for cp in w_copies(e, 0):
            cp.start()

    @pl.when(used & ((j == 0) | (e != be_ref[jnp.maximum(j - 1, 0)])))
    def _new_expert():
        slot = slot_ref[0]
        for cp in w_copies(e, slot):
            cp.wait()
        nx = nxt_ref[e]

        @pl.when(nx != e)
        def _prefetch():
            for cp in w_copies(nx, 1 - slot):
                cp.start()

        wgu_b[...] = wgu_f[slot].astype(BF16)
        wd_b[...] = wd_f[slot].astype(BF16)
        slot_ref[0] = 1 - slot

    @pl.when(used)
    def _compute():
        x = _tiles_to_rows(xs_ref, MOE_BLOCK)
        gu = _dot(x.astype(BF16), wgu_b[...]) + bgu_ref[0]
        g_ = jnp.minimum(gu[:, :F], SWIGLU_LIMIT)
        up = jnp.clip(gu[:, F:], -SWIGLU_LIMIT, SWIGLU_LIMIT)
        act = (up + 1.0) * g_ * _sigmoid(SWIGLU_ALPHA * g_)
        _rows_to_tiles(ys_o, _dot(act.astype(BF16), wd_b[...]) + bd_ref[0])

    @pl.when(j >= nu_ref[0])
    def _unused():
        ys_o[...] = jnp.zeros_like(ys_o)


def _experts(blk_e, n_used, nxt_e, xs, wgu, bgu, wd, bd):
    nb = xs.shape[0] // (MOE_BLOCK * SUB)
    E, D, F2 = wgu.shape
    F = F2 // 2
    grid_spec = pltpu.PrefetchScalarGridSpec(
        num_scalar_prefetch=3,
        grid=(nb,),
        in_specs=[pl.BlockSpec((MOE_BLOCK * SUB, LANES), lambda j, be, nu, nx: (jnp.minimum(j, nu[0] - 1), 0)),
                  pl.BlockSpec(memory_space=pl.ANY),
                  pl.BlockSpec((1, 1, F2), lambda j, be, nu, nx: (be[j], 0, 0)),
                  pl.BlockSpec(memory_space=pl.ANY),
                  pl.BlockSpec((1, 1, D), lambda j, be, nu, nx: (be[j], 0, 0))],
        out_specs=pl.BlockSpec((MOE_BLOCK * SUB, LANES), lambda j, be, nu, nx: (j, 0)),
        scratch_shapes=[pltpu.VMEM((2, D, F2), F32), pltpu.VMEM((2, F, D), F32),
                        pltpu.VMEM((D, F2), BF16), pltpu.VMEM((F, D), BF16),
                        pltpu.SMEM((1,), I32), pltpu.SemaphoreType.DMA((2, 2))],
    )
    return pl.pallas_call(
        _expert_kernel,
        grid_spec=grid_spec,
        out_shape=jax.ShapeDtypeStruct(xs.shape, F32),
        compiler_params=_cparams(("arbitrary",)),
        name="moe_experts",
    )(blk_e, n_used, nxt_e, xs, wgu, bgu, wd, bd)


def _combine_kernel(dest0_ref, destn_ref, h1_ref, pp_ref, ps_ref, gates_ref, ys_ref, npl_ref, wpg_ref, wpp_ref, fn_ref,
                    yp_o, ysm_o, gbuf, ybuf, sems, *, ntp, n_tiles):
    TM = h1_ref.shape[0]
    t = pl.program_id(0)

    def fetch(dref, i, slot):
        for kk in range(TOP_K):
            src = _row_tile(ys_ref, dref[i * TOP_K + kk])
            pltpu.make_async_copy(src, _row_tile(gbuf.at[slot, kk], i * SUB), sems.at[slot]).start(priority=kk % 2)

    @pl.when(t == 0)
    def _first_fetch():
        def body(i, _):
            fetch(dest0_ref, i, 0)
            return 0

        lax.fori_loop(0, TM, body, 0, unroll=DMA_UNROLL)

    def tile(slot):
        _row_copy_wait(gbuf.at[slot, 0], sems.at[slot], TOP_K)
        for i in range(TM):
            fetch(destn_ref, i, 1 - slot)
        gates = gates_ref[...]
        h2 = h1_ref[...]
        for kk in range(TOP_K):
            h2 = h2 + _tiles_to_rows(gbuf.at[slot, kk], TM) * gates[:, kk:kk + 1]
        u3 = _rms(h2, npl_ref[...])
        gate = _sigmoid(_dot(u3.astype(BF16), wpg_ref[...]))
        p = _pick(t, ntp, pp_ref, ps_ref)
        h3 = h2 + gate * _dot(p.astype(BF16), wpp_ref[...])
        return _rms(h3, fn_ref[...])

    for slot in range(2):
        @pl.when(t % 2 == slot)
        def _tile():
            ybuf[...] = tile(slot)

    @pl.when(t < ntp)
    def _prompt_rows():
        yp_o[...] = ybuf[...]

    @pl.when(t >= ntp)
    def _sample_rows():
        ysm_o[...] = ybuf[...]

    @pl.when(t == n_tiles - 1)
    def _drain():
        other = n_tiles % 2
        _row_copy_wait(gbuf.at[other, 0], sems.at[other], TOP_K)


def _combine(dest_flat, h1, pp, ps, gates, ys, npl, wpg, wpp, fn):
    T, D = h1.shape
    nt = T // ROW_TILE
    ntp = pp.shape[0] // ROW_TILE
    row = lambda n: pl.BlockSpec((ROW_TILE, n), lambda i: (i, 0))
    full = lambda a: pl.BlockSpec(a.shape, lambda i: (0, 0))
    idx = lambda f: pl.BlockSpec((ROW_TILE * TOP_K,), f, memory_space=pltpu.SMEM)
    return pl.pallas_call(
        functools.partial(_combine_kernel, ntp=ntp, n_tiles=nt),
        grid=(nt,),
        in_specs=[idx(lambda i: (0,)), idx(lambda i: (jnp.minimum(i + 1, nt - 1),)), row(D)]
        + _two_specs(ntp, pp.shape[1])
        + [row(LANES), pl.BlockSpec(memory_space=pl.ANY), full(npl), full(wpg), full(wpp), full(fn)],
        out_specs=_two_specs(ntp, D),
        out_shape=[jax.ShapeDtypeStruct((pp.shape[0], D), F32), jax.ShapeDtypeStruct((ps.shape[0], D), F32)],
        scratch_shapes=[pltpu.VMEM((2, TOP_K, ROW_TILE * SUB, LANES), F32), pltpu.VMEM((ROW_TILE, D), F32),
                        pltpu.SemaphoreType.DMA((2,))],
        compiler_params=_cparams(("arbitrary",)),
        name="moe_combine_ple",
    )(dest_flat, dest_flat, h1, pp, ps, gates, ys, npl, wpg, wpp, fn)


def _pad_lora_cols(x):
    W3 = 3 * B_WIDTH
    z = jnp.zeros(x.shape[:-1] + (LORA_PAD - 64,), x.dtype)
    return jnp.concatenate([x[..., :W3 + 64], z, x[..., W3 + 64:W3 + 128], z, x[..., W3 + 128:]], axis=-1)


def _unpad_lora_cols(x):
    W3 = 3 * B_WIDTH
    return jnp.concatenate([x[..., :W3 + 64], x[..., W3 + LORA_PAD:W3 + LORA_PAD + 64], x[..., W3 + 2 * LORA_PAD:]], axis=-1)


def _heads_to_pairs(S):
    n = S.shape[0]
    S = S.reshape(n, B_HEADS // 2, 2, B_HDIM, B_HDIM)
    z = jnp.zeros_like(S[:, :, 0])
    top = jnp.concatenate([S[:, :, 0], z], axis=-1)
    bot = jnp.concatenate([z, S[:, :, 1]], axis=-1)
    return jnp.concatenate([top, bot], axis=-2)


def _pairs_to_heads(Sb):
    n = Sb.shape[0]
    a = Sb[:, :, :B_HDIM, :B_HDIM]
    b = Sb[:, :, B_HDIM:, B_HDIM:]
    return jnp.stack([a, b], axis=2).reshape(n, B_HEADS, B_HDIM, B_HDIM)


def kernel(x_prompt, x_sample, state_mlstm_C, state_mlstm_n, state_mlstm_m, state_mlstm_conv, state_rwkv_S, state_rwkv_shift, p_prompt, p_sample, norm_mix, w_in, conv_w, conv_b, b_ig, b_fg, a_norm, shift_mu, w0, w_w2, a0, w_a2, w_g2, k_k, k_a, r_k, gn_w, gn_b, w_out, norm_ffn, w_router, b_router, w_gu, b_gu, w_down, b_down, norm_ple, w_ple_gate, w_ple_proj, final_norm):
    assert norm_mix.shape[0] == 1, "single-layer trunk"
    nbp, Lp, D = x_prompt.shape
    nbs, Ls, _ = x_sample.shape
    Tp, Ts = nbp * Lp, nbs * Ls
    T = Tp + Ts
    row = lambda a: a.reshape(1, -1).astype(F32)

    wi = w_in[0]
    AW2 = 2 * A_WIDTH
    wqk = wi[:, :AW2].astype(BF16)
    wv = wi[:, AW2:AW2 + A_WIDTH].astype(BF16)
    wo = wi[:, AW2 + A_WIDTH:AW2 + 2 * A_WIDTH].astype(BF16)
    a_cols = 4 * A_WIDTH + 2 * A_HEADS
    wif = jnp.pad(wi[:, 4 * A_WIDTH:a_cols], ((0, 0), (0, LANES - 2 * A_HEADS)))
    wbg = jnp.concatenate([_pad_lora_cols(wi[:, a_cols:]), wif], axis=1).astype(BF16)
    gate_bias = jnp.pad(jnp.concatenate([b_ig[0], b_fg[0]]), (0, LANES - 2 * A_HEADS)).reshape(1, LANES)
    padw = lambda w: jnp.pad(w, ((0, LORA_PAD - w.shape[0]), (0, 0)))
    hid = jnp.arange(B_WIDTH // 2) // B_HDIM
    bd = (hid[:, None] == hid[None, :]).astype(BF16)
    E = w_router.shape[-1]
    wr = w_router[0].T.astype(BF16)
    br = b_router[0].reshape(E, 1)

    assert Tp % ROW_TILE == 0 and Ts % ROW_TILE == 0
    xp, xsm = x_prompt.reshape(Tp, D), x_sample.reshape(Ts, D)
    qk, v, og, pb = _in_proj(xp, xsm, row(norm_mix), wqk, wv, wo, wbg)
    ifg = pb

    f0 = lambda *shp: jnp.zeros(shp, F32)
    ml_consts = (conv_w[0], row(conv_b), gate_bias, row(a_norm))
    m_s = jnp.pad(state_mlstm_m[0], ((0, 0), (0, LANES - A_HEADS))).reshape(nbs, 1, LANES)
    hap, Cp, np_, mp, convp = _mlstm(qk, v, og, ifg, f0(nbp, A_HEADS, A_HDIM, A_HDIM), f0(nbp, A_HEADS, A_HDIM),
                                     f0(nbp, 1, LANES), f0(nbp, CONV_W - 1, AW2), *ml_consts,
                                     n_seq=nbp, L=Lp, row_off=0)
    has, Cs, ns_, ms, convs = _mlstm(qk, v, og, ifg, state_mlstm_C[0], state_mlstm_n[0], m_s, state_mlstm_conv[0],
                                     *ml_consts, n_seq=nbs, L=Ls, row_off=Tp)

    rw_consts = (row(_pad_lora_cols(shift_mu[0])), row(w0), padw(w_w2[0]).astype(BF16), row(a0), padw(w_a2[0]).astype(BF16),
                 w_g2[0].astype(BF16),
                 row(k_k), row(k_a), row(r_k), row(gn_w), row(gn_b), bd)
    npair = B_HEADS // 2
    hbp, Sp, shp_ = _rwkv(pb, f0(nbp, npair, PAIR, PAIR), f0(nbp, 1, B_COLS_PAD), *rw_consts,
                          n_seq=nbp, L=Lp, row_off=0)
    hbs, Ss, shs = _rwkv(pb, _heads_to_pairs(state_rwkv_S[0]), _pad_lora_cols(state_rwkv_shift[0]), *rw_consts,
                         n_seq=nbs, L=Ls, row_off=Tp)

    wout = w_out[0].astype(BF16)
    hap, has = hap.reshape(Tp, A_WIDTH), has.reshape(Ts, A_WIDTH)
    h1, u2, ids, gates, rank, counts = _out_router(hap, has, hbp, hbs, xp, xsm, wout[:A_WIDTH], wout[A_WIDTH:],
                                                   row(norm_ffn), wr, br)

    counts = counts[:, 0]
    pcounts = (counts + MOE_BLOCK - 1) // MOE_BLOCK * MOE_BLOCK
    pend = jnp.cumsum(pcounts)
    pstart = pend - pcounts
    n_blk = -(-T * TOP_K // MOE_BLOCK) + E
    eid = jnp.arange(E, dtype=I32)
    base = jnp.sum(jnp.where(ids[None, :TOP_K] == eid[:, None, None], pstart[:, None, None], 0), axis=0)
    dest = ((base + rank[:TOP_K]) * SUB).T.reshape(-1).astype(I32)
    blk_start = jnp.arange(n_blk, dtype=I32) * MOE_BLOCK
    blk_e = jnp.minimum(jnp.sum(pend[None, :] <= blk_start[:, None], axis=1), E - 1).astype(I32)
    n_used = (pend[E - 1:] // MOE_BLOCK).astype(I32)

    xs = _scatter_rows((pstart + counts).astype(I32), pend.astype(I32), n_used, dest, u2, n_blk * MOE_BLOCK)
    later = jnp.where((eid[None, :] > eid[:, None]) & (pcounts[None, :] > 0), eid[None, :], E)
    nxt_e = jnp.min(later, axis=1)
    nxt_e = jnp.where(nxt_e < E, nxt_e, eid).astype(I32)
    ys = _experts(blk_e, n_used, nxt_e, xs, w_gu[0], b_gu[0][:, None, :], w_down[0], b_down[0][:, None, :])
    yp, ysm = _combine(dest, h1, p_prompt[0].reshape(Tp, -1), p_sample[0].reshape(Ts, -1), gates, ys, row(norm_ple),
                       w_ple_gate[0].astype(BF16), w_ple_proj[0].astype(BF16), row(final_norm))

    lead = lambda a: a[None]
    return (yp.reshape(nbp, Lp, D), ysm.reshape(nbs, Ls, D),
            lead(Cp), lead(np_), lead(mp[:, 0, :A_HEADS]), lead(convp), lead(_pairs_to_heads(Sp)), lead(_unpad_lora_cols(shp_)),
            lead(Cs), lead(ns_), lead(ms[:, 0, :A_HEADS]), lead(convs), lead(_pairs_to_heads(Ss)), lead(_unpad_lora_cols(shs)))
```

```python
import functools

import jax
import jax.numpy as jnp
from jax import lax
from jax.experimental import pallas as pl
from jax.experimental.pallas import tpu as pltpu

F32 = jnp.float32
BF16 = jnp.bfloat16
I32 = jnp.int32

EPS = 1e-6
GN_EPS = 64e-5
A_HEADS, A_HDIM = 4, 128
B_HEADS, B_HDIM = 8, 64
A_WIDTH = A_HEADS * A_HDIM
B_WIDTH = B_HEADS * B_HDIM
CONV_W = 4
LORA_PAD = 128
N_EXPERTS, TOP_K = 32, 4
SWIGLU_LIMIT, SWIGLU_ALPHA = 7.0, 1.702
LANES = 128
ROW_TILE = 256
VMEM_LIMIT = 56 * 1024 * 1024


def _cparams(sem):
    return pltpu.CompilerParams(dimension_semantics=sem, vmem_limit_bytes=VMEM_LIMIT)


def _dot(a, b):
    return lax.dot_general(a, b, (((1,), (0,)), ((), ())), preferred_element_type=F32)


def _dot_nt(a, b):
    return lax.dot_general(a, b, (((1,), (1,)), ((), ())), preferred_element_type=F32)


def _dot_tn(a, b):
    return lax.dot_general(a, b, (((0,), (0,)), ((), ())), preferred_element_type=F32)


def _split2(x):
    hi = x.astype(BF16)
    lo = (x - hi.astype(F32)).astype(BF16)
    return hi, lo


def _split3(x):
    hi = x.astype(BF16)
    r1 = x - hi.astype(F32)
    mid = r1.astype(BF16)
    lo = (r1 - mid.astype(F32)).astype(BF16)
    return hi, mid, lo


def _dot3(a, b, f=_dot):
    ah, al = _split2(a)
    bh, bl = _split2(b)
    return f(ah, bh) + (f(ah, bl) + f(al, bh))


def _dotx_l(a_exact, b, f=_dot, pieces=3):
    parts = _split3(b)[:pieces]
    acc = f(a_exact, parts[-1])
    for part in parts[-2::-1]:
        acc = f(a_exact, part) + acc
    return acc


def _rms(x, g):
    return x * lax.rsqrt(jnp.mean(x * x, axis=-1, keepdims=True) + EPS) * g


def _sigmoid(x):
    return 1.0 / (1.0 + jnp.exp(-x))


def _two_specs(ntp, n):
    return [pl.BlockSpec((ROW_TILE, n), lambda i: (jnp.minimum(i, ntp - 1), 0)),
            pl.BlockSpec((ROW_TILE, n), lambda i: (jnp.maximum(i - ntp, 0), 0))]


def _pick(i, ntp, p_ref, s_ref):
    return jnp.where(i < ntp, p_ref[...], s_ref[...])


def _in_proj_kernel(xp_ref, xs_ref, g_ref, wqk_ref, wv_ref, wo_ref, wbg_ref,
                    qk_o, v_o, og_o, pbg_o, *, ntp):
    x = _pick(pl.program_id(0), ntp, xp_ref, xs_ref)
    ub = _rms(x, g_ref[...]).astype(BF16)
    qk_o[...] = _dot(ub, wqk_ref[...])
    v_o[...] = _dot(ub, wv_ref[...])
    og_o[...] = _dot(ub, wo_ref[...])
    pbg_o[...] = _dot(ub, wbg_ref[...])


def _in_proj(xp, xs, g, wqk, wv, wo, wbg):
    D = xp.shape[1]
    T = xp.shape[0] + xs.shape[0]
    ntp = xp.shape[0] // ROW_TILE
    row = lambda n: pl.BlockSpec((ROW_TILE, n), lambda i: (i, 0))
    full = lambda a: pl.BlockSpec(a.shape, lambda i: (0, 0))
    outs = [wqk.shape[1], wv.shape[1], wo.shape[1], wbg.shape[1]]
    return pl.pallas_call(
        functools.partial(_in_proj_kernel, ntp=ntp),
        grid=(T // ROW_TILE,),
        in_specs=_two_specs(ntp, D) + [full(g), full(wqk), full(wv), full(wo), full(wbg)],
        out_specs=[row(n) for n in outs],
        out_shape=[jax.ShapeDtypeStruct((T, n), F32) for n in outs],
        compiler_params=_cparams(("arbitrary",)),
        name="in_proj",
    )(xp, xs, g, wqk, wv, wo, wbg)


MLSTM_CHUNK = 256
NEG_BIG = -1e30


def _log_sigmoid(x):
    return jnp.minimum(x, 0.0) - jnp.log1p(jnp.exp(-jnp.abs(x)))


MLSTM_GROUP = 1


def _mlstm_kernel(*refs, nv, CH):
    G = MLSTM_GROUP
    tok = [refs[4 * j:4 * j + 4] for j in range(G)]
    (C0_ref, n0_ref, m0_ref, conv0_ref, cw_ref, cb_ref, gb_ref, an_ref,
     ha_o, C1_o, n1_o, m1_o, conv1_o, C_sc, n_sc, m_sc, cbuf) = refs[4 * G:]
    c = pl.program_id(1)

    @pl.when(c == 0)
    def _init():
        C_sc[...] = C0_ref[...]
        for j in range(G):
            n_sc[j, 0:A_HEADS] = n0_ref[j]
            m_sc[j, 0:1] = m0_ref[j]
            cbuf[j, 5:8] = conv0_ref[j]

    row = lax.broadcasted_iota(I32, (CH, LANES), 0)
    r2 = lax.broadcasted_iota(I32, (CH, CH), 0)
    c2 = lax.broadcasted_iota(I32, (CH, CH), 1)
    tri = r2 >= c2
    tril_b = jnp.where(tri, 1.0, 0.0).astype(BF16)
    sel = jnp.where(lax.broadcasted_iota(I32, (8, LANES), 0) == lax.broadcasted_iota(I32, (8, LANES), 1),
                    1.0, 0.0).astype(BF16)
    lane1 = lax.broadcasted_iota(I32, (1, LANES), 1)

    q, k, v, og, b, g, gT, conv_tail, m_row = [], [], [], [], [], [], [], [], []
    for j, (qk_ref, v_ref, og_ref, if_ref) in enumerate(tok):
        cb_j = cbuf.at[j]
        cb_j[8:8 + nv] = qk_ref[...]
        if nv < CH:
            cb_j[8 + nv:8 + CH] = jnp.zeros((CH - nv, 2 * A_WIDTH), F32)
        y = cb_ref[...] + cb_j[5:5 + CH] * cw_ref[0:1]
        for tap in range(1, CONV_W):
            y = y + cb_j[5 + tap:5 + tap + CH] * cw_ref[tap:tap + 1]
        conv_tail.append(cb_j[5 + nv:8 + nv])
        qk = y * _sigmoid(y)
        q.append(qk[:, :A_WIDTH])
        k.append(qk[:, A_WIDTH:] * (A_HDIM ** -0.5))
        if nv == CH:
            v.append(v_ref[...])
            og.append(og_ref[...])
            gp = if_ref[...] + gb_ref[...]
        else:
            zpad = lambda n: jnp.zeros((CH - nv, n), F32)
            v.append(jnp.concatenate([v_ref[...], zpad(A_WIDTH)], axis=0))
            og.append(jnp.concatenate([og_ref[...], zpad(A_WIDTH)], axis=0))
            gp = jnp.concatenate([if_ref[...] + gb_ref[...], zpad(LANES)], axis=0)
        li = gp
        lf = pltpu.roll(_log_sigmoid(gp), LANES - A_HEADS, 1)
        if nv < CH:
            li = jnp.where(row < nv, li, NEG_BIG)
            lf = jnp.where(row < nv, lf, 0.0)
        b.append(_dotx_l(tril_b, lf))
        g.append(li - b[j])
        gT.append(_dotx_l(sel, g[j], f=_dot_nt))
        m_row.append(m_sc[j, 0:1])
    qb = [x.astype(BF16) for x in q]
    kb = [x.astype(BF16) for x in k]
    vb = [x.astype(BF16) for x in v]

    U = [(j, h) for j in range(G) for h in range(A_HEADS)]
    sl = lambda h: slice(h * A_HDIM, (h + 1) * A_HDIM)
    bcol = [b[j][:, h:h + 1] for j, h in U]
    m_h = [m_row[j][:, h:h + 1] for j, h in U]
    C_h = [C_sc[j, h] for j, h in U]
    n_h = [n_sc[j, h:h + 1] for j, h in U]
    qk_t = [_dot_nt(qb[j][:, sl(h)], kb[j][:, sl(h)]) for j, h in U]
    qC = [_dot_nt(qb[j][:, sl(h)], C_h[u].astype(BF16)) for u, (j, h) in enumerate(U)]
    dm = [jnp.where(tri, bcol[u] + gT[j][h:h + 1, :], -jnp.inf) for u, (j, h) in enumerate(U)]
    inter = [bcol[u] + m_h[u] for u in range(len(U))]
    mt = [jnp.maximum(inter[u], jnp.max(dm[u], axis=1, keepdims=True)) for u in range(len(U))]
    s = [qk_t[u] * jnp.exp(dm[u] - mt[u]) for u in range(len(U))]
    wi = [jnp.exp(inter[u] - mt[u]) for u in range(len(U))]
    sv = [_dot(s[u].astype(BF16), vb[j][:, sl(h)]) for u, (j, h) in enumerate(U)]
    m_new = [mt[u][CH - 1:CH] for u in range(len(U))]
    w_end = [jnp.exp(bcol[u][CH - 1:CH] + g[j][:, h:h + 1] - m_new[u]) for u, (j, h) in enumerate(U)]
    dec = [jnp.exp(bcol[u][CH - 1:CH] + m_h[u] - m_new[u]) for u in range(len(U))]
    vk = [_dot_tn((v[j][:, sl(h)] * w_end[u]).astype(BF16), kb[j][:, sl(h)]) for u, (j, h) in enumerate(U)]
    outs = [[] for _ in range(G)]
    m_next = list(m_row)
    for u, (j, h) in enumerate(U):
        num = sv[u] + wi[u] * qC[u]
        den = jnp.sum(s[u], axis=1, keepdims=True) + wi[u] * jnp.sum(q[j][:, sl(h)] * n_h[u], axis=1, keepdims=True)
        hh = num / jnp.maximum(jnp.abs(den), jnp.exp(-mt[u]))
        hn = hh * lax.rsqrt(jnp.mean(hh * hh, axis=1, keepdims=True) + EPS)
        outs[j].append(hn * an_ref[:, sl(h)] * _sigmoid(og[j][:, sl(h)]))
        C_sc[j, h] = dec[u] * C_h[u] + vk[u]
        n_sc[j, h:h + 1] = dec[u] * n_h[u] + jnp.sum(k[j][:, sl(h)] * w_end[u], axis=0, keepdims=True)
        m_next[j] = jnp.where(lane1 == h, m_new[u], m_next[j])
    for j in range(G):
        m_sc[j, 0:1] = m_next[j]
        ha_o[j] = jnp.concatenate(outs[j], axis=1)[:nv].astype(BF16)
        cbuf[j, 5:8] = conv_tail[j]

    @pl.when(c == pl.num_programs(1) - 1)
    def _fin():
        C1_o[...] = C_sc[...]
        for j in range(G):
            n1_o[j] = n_sc[j, 0:A_HEADS]
            m1_o[j] = m_sc[j, 0:1]
            conv1_o[j] = conv_tail[j]


def _mlstm(qk, v, og, pbg, C0, n0, m0, conv0, cw, cb, gb, an, *, n_seq, L, row_off):
    G = MLSTM_GROUP
    CH = min(L, MLSTM_CHUNK) if L % SUB == 0 else MLSTM_CHUNK
    assert n_seq % G == 0
    nv = min(L, CH)
    nch = L // nv
    off = row_off // nv
    gate_tile = pbg.shape[1] // LANES - 1
    st = lambda shp: pl.BlockSpec((G,) + shp, lambda s, c: (s,) + (0,) * len(shp))
    full = lambda a: pl.BlockSpec(a.shape, lambda s, c: (0,) * a.ndim)
    in_specs, args = [], []
    for j in range(G):
        rows = lambda s, c, j=j: off + (G * s + j) * nch + c
        in_specs += [pl.BlockSpec((nv, 2 * A_WIDTH), lambda s, c, r=rows: (r(s, c), 0)),
                     pl.BlockSpec((nv, A_WIDTH), lambda s, c, r=rows: (r(s, c), 0)),
                     pl.BlockSpec((nv, A_WIDTH), lambda s, c, r=rows: (r(s, c), 0)),
                     pl.BlockSpec((nv, LANES), lambda s, c, r=rows: (r(s, c), gate_tile))]
        args += [qk, v, og, pbg]
    in_specs += [st((A_HEADS, A_HDIM, A_HDIM)), st((A_HEADS, A_HDIM)), st((1, LANES)),
                 st((CONV_W - 1, 2 * A_WIDTH)), full(cw), full(cb), full(gb), full(an)]
    args += [C0, n0, m0, conv0, cw, cb, gb, an]
    out_shape = [jax.ShapeDtypeStruct((n_seq, L, A_WIDTH), BF16),
                 jax.ShapeDtypeStruct((n_seq, A_HEADS, A_HDIM, A_HDIM), F32),
                 jax.ShapeDtypeStruct((n_seq, A_HEADS, A_HDIM), F32),
                 jax.ShapeDtypeStruct((n_seq, 1, LANES), F32),
                 jax.ShapeDtypeStruct((n_seq, CONV_W - 1, 2 * A_WIDTH), F32)]
    out_specs = [pl.BlockSpec((G, nv, A_WIDTH), lambda s, c: (s, c, 0)),
                 st((A_HEADS, A_HDIM, A_HDIM)), st((A_HEADS, A_HDIM)),
                 st((1, LANES)), st((CONV_W - 1, 2 * A_WIDTH))]
    return pl.pallas_call(
        functools.partial(_mlstm_kernel, nv=nv, CH=CH),
        grid=(n_seq // G, nch),
        in_specs=in_specs, out_specs=out_specs, out_shape=out_shape,
        scratch_shapes=[pltpu.VMEM((G, A_HEADS, A_HDIM, A_HDIM), F32), pltpu.VMEM((G, 8, A_HDIM), F32),
                        pltpu.VMEM((G, 8, LANES), F32), pltpu.VMEM((G, CH + 8, 2 * A_WIDTH), F32)],
        compiler_params=_cparams(("arbitrary", "arbitrary")),
        name="mlstm",
    )(*args)


RWKV_CHUNK = 64
RWKV_NCHUNK = 4
B_COLS_PAD = 3 * B_WIDTH + 2 * LORA_PAD + 128
PAIR = 2 * B_HDIM


def _softplus(z):
    return jnp.maximum(z, 0.0) + jnp.log1p(jnp.exp(-jnp.abs(z)))


def _rwkv_kernel(pb_ref, S0_ref, sh0_ref, mu_ref, w0_ref, ww2_ref, a0_ref, wa2_ref, wg2_ref,
                 kk_ref, ka_ref, rk_ref, gw_ref, gb_ref, bd_ref,
                 hb_o, S1_o, sh1_o, S_sc, sbuf, *, nv):
    C = RWKV_CHUNK
    nc = -(-nv // C)
    TB = nc * C
    t = pl.program_id(1)

    @pl.when(t == 0)
    def _init():
        S_sc[...] = S0_ref[0]
        sbuf[7:8] = sh0_ref[0]

    sbuf[8:8 + nv] = pb_ref[...]
    if nv < TB:
        sbuf[8 + nv:8 + TB] = jnp.zeros((TB - nv, B_COLS_PAD), F32)
    pb = sbuf[8:8 + TB]
    prev = sbuf[7:7 + TB]
    last_row = sbuf[7 + nv:8 + nv]
    xm = pb + (prev - pb) * mu_ref[...]
    W = B_WIDTH
    r = xm[:, 0:W]
    kb = xm[:, W:2 * W]
    vb = xm[:, 2 * W:3 * W]
    wl = xm[:, 3 * W:3 * W + LORA_PAD]
    al = xm[:, 3 * W + LORA_PAD:3 * W + 2 * LORA_PAD]
    gl = xm[:, 3 * W + 2 * LORA_PAD:]
    def seg_sum(z):
        hw = bd_ref.shape[0]
        zb = z.astype(BF16)
        return jnp.concatenate([_dot(zb[:, i * hw:(i + 1) * hw], bd_ref[...]) for i in range(B_WIDTH // hw)], axis=1)
    w_log = -_softplus(-(w0_ref[...] + _dot(jnp.tanh(wl).astype(BF16), ww2_ref[...]))) - 0.5
    lw = -jnp.exp(w_log)
    a = _sigmoid(a0_ref[...] + _dot(al.astype(BF16), wa2_ref[...]))
    g = _dot(_sigmoid(gl).astype(BF16), wg2_ref[...])
    kkr = kb * kk_ref[...]
    kk = kkr / jnp.maximum(jnp.sqrt(seg_sum(kkr * kkr)), 1e-12)
    k = kb * (1.0 + (a - 1.0) * ka_ref[...])
    bonus = seg_sum(r * k * rk_ref[...]) * vb
    if nv < TB:
        valid = lax.broadcasted_iota(I32, (TB, W), 0) < nv
        zero = lambda z: jnp.where(valid, z, 0.0)
        lw, kk, k, vb = zero(lw), zero(kk), zero(k), zero(vb)

    lane = lax.broadcasted_iota(I32, (C, PAIR), 1)
    h0 = lane < B_HDIM
    sep = lambda z: jnp.concatenate([jnp.where(h0, z, 0.0), jnp.where(h0, 0.0, z)], axis=0)
    ri = lax.broadcasted_iota(I32, (C, C), 0)
    ci = lax.broadcasted_iota(I32, (C, C), 1)
    tril_c = jnp.where(ri >= ci, 1.0, 0.0).astype(BF16)
    r2 = lax.broadcasted_iota(I32, (PAIR, PAIR), 0)
    c2 = lax.broadcasted_iota(I32, (PAIR, PAIR), 1)
    low_s = r2 > c2
    low_i = r2 >= c2
    eye = jnp.where(r2 == c2, 1.0, 0.0)
    npair = B_HEADS // 2
    units = [(c, p) for c in range(nc) for p in range(npair)]
    cat0 = lambda x, y: jnp.concatenate([x, y], axis=0)
    cat1 = lambda x, y: jnp.concatenate([x, y], axis=1)
    bf = lambda z: z.astype(BF16)
    lhs, rhs_t, vs, wc = [], [], [], []
    bcum = [_dotx_l(tril_c, lw[c * C:(c + 1) * C], pieces=2) for c in range(nc)]
    for (c, p) in units:
        rs = slice(c * C, (c + 1) * C)
        ls = slice(p * PAIR, (p + 1) * PAIR)
        lwc = lw[rs, ls]
        b = bcum[c][:, ls]
        emb = jnp.exp(-b)
        kkc = kk[rs, ls]
        lhs.append(bf(cat0(sep(kkc * jnp.exp(b - lwc)), sep(r[rs, ls] * jnp.exp(b)))))
        rhs_t.append(bf(cat0(sep(k[rs, ls] * emb), sep(kkc * a[rs, ls] * emb))))
        vs.append(bf(sep(vb[rs, ls])))
        wc.append(jnp.exp(b[C - 1:C]))
    A = [_dot_nt(l_, r_) for l_, r_ in zip(lhs, rhs_t)]
    Ak = [jnp.where(low_s, x[:PAIR, :PAIR], 0.0) for x in A]
    Ab = [jnp.where(low_s, x[:PAIR, PAIR:], 0.0) for x in A]
    Ark = [jnp.where(low_i, x[PAIR:, :PAIR], 0.0) for x in A]
    Arb = [jnp.where(low_i, x[PAIR:, PAIR:], 0.0) for x in A]
    X2 = [_dot(bf(cat0(x, y)), v_) for x, y, v_ in zip(Ak, Ark, vs)]
    VtK = [_dot_tn(v_, r_[:PAIR]) for v_, r_ in zip(vs, rhs_t)]
    Tm = [eye - x for x in Ab]
    P = [_dot(bf(x), bf(x)) for x in Ab]
    n = 2
    while 2 * n < C:
        X = [_dot(bf(cat0(t_, p_)), bf(p_)) for t_, p_ in zip(Tm, P)]
        Tm = [t_ + x[:PAIR] for t_, x in zip(Tm, X)]
        P = [x[PAIR:] for x in X]
        n *= 2
    Tm = [t_ + _dot(bf(t_), bf(p_)) for t_, p_ in zip(Tm, P)]
    Y = [_dot(bf(t_), cat1(l_[:PAIR], bf(x[:PAIR]))) for t_, l_, x in zip(Tm, lhs, X2)]
    Z = [_dot(bf(x), bf(y)) for x, y in zip(Arb, Y)]
    Rp = [bf(l_[PAIR:].astype(F32) - z[:, :PAIR]) for l_, z in zip(lhs, Z)]
    Op = [x[PAIR:] - z[:, PAIR:] for x, z in zip(X2, Z)]
    W = [_dot_tn(bf(y), r_[PAIR:]) for y, r_ in zip(Y, rhs_t)]
    Mc = [bf(w_[:PAIR] * c_) for w_, c_ in zip(W, wc)]
    Nn = [(v_ - w_[PAIR:]) * c_ for v_, w_, c_ in zip(VtK, W, wc)]
    o_rows = []
    for c in range(nc):
        S = [S_sc[p] for p in range(npair)]
        Sb = [bf(s_) for s_ in S]
        u0 = c * npair
        O = [_dot_nt(Rp[u0 + p], Sb[p]) + Op[u0 + p] for p in range(npair)]
        for p in range(npair):
            S_sc[p] = S[p] * wc[u0 + p] - _dot(Sb[p], Mc[u0 + p]) + Nn[u0 + p]
        o_rows.append(jnp.concatenate([x[:C] + x[C:] for x in O], axis=1))
    o = o_rows[0] if nc == 1 else jnp.concatenate(o_rows, axis=0)

    mean = seg_sum(o) * (1.0 / B_HDIM)
    d = o - mean
    var = seg_sum(d * d) * (1.0 / B_HDIM)
    hb = (d * lax.rsqrt(var + GN_EPS) * gw_ref[...] + gb_ref[...] + bonus) * g
    hb_o[...] = hb[:nv].astype(BF16)
    sbuf[7:8] = last_row

    @pl.when(t == pl.num_programs(1) - 1)
    def _fin():
        S1_o[0] = S_sc[...]
        sh1_o[0] = last_row


def _rwkv(pb, S0, sh0, mu, w0, ww2, a0, wa2, wg2, k_k, k_a, r_k, gn_w, gn_b, bd, *, n_seq, L, row_off):
    nv = min(L, RWKV_CHUNK * RWKV_NCHUNK)
    nst = L // nv
    off = row_off // nv
    TB = -(-nv // RWKV_CHUNK) * RWKV_CHUNK
    tok = lambda n: pl.BlockSpec((nv, n), lambda s, t: (off + s * nst + t, 0))
    st = lambda shp: pl.BlockSpec((1,) + shp, lambda s, t: (s,) + (0,) * len(shp))
    full = lambda a: pl.BlockSpec(a.shape, lambda s, t: (0,) * a.ndim)
    consts = [mu, w0, ww2, a0, wa2, wg2, k_k, k_a, r_k, gn_w, gn_b, bd]
    npair = B_HEADS // 2
    in_specs = [tok(B_COLS_PAD), st((npair, PAIR, PAIR)), st((1, B_COLS_PAD))] + [full(a) for a in consts]
    args = [pb, S0, sh0] + consts
    out_shape = [jax.ShapeDtypeStruct((n_seq * L, B_WIDTH), BF16),
                 jax.ShapeDtypeStruct((n_seq, npair, PAIR, PAIR), F32),
                 jax.ShapeDtypeStruct((n_seq, 1, B_COLS_PAD), F32)]
    out_specs = [pl.BlockSpec((nv, B_WIDTH), lambda s, t: (s * nst + t, 0)),
                 st((npair, PAIR, PAIR)), st((1, B_COLS_PAD))]
    return pl.pallas_call(
        functools.partial(_rwkv_kernel, nv=nv),
        grid=(n_seq, nst),
        in_specs=in_specs, out_specs=out_specs, out_shape=out_shape,
        scratch_shapes=[pltpu.VMEM((npair, PAIR, PAIR), F32), pltpu.VMEM((TB + 8, B_COLS_PAD), F32)],
        compiler_params=_cparams(("arbitrary", "arbitrary")),
        name="rwkv",
    )(*args)


SUB = 8


def _rows_to_tiles(ref, x):
    rows = x.shape[0]
    for j in range(SUB):
        ref[pl.ds(j, rows, stride=SUB), :] = x[:, j * LANES:(j + 1) * LANES]


def _tiles_to_rows(ref, rows):
    return jnp.concatenate([ref[pl.ds(j, rows, stride=SUB), :] for j in range(SUB)], axis=1)


def _row_tile(ref, r8):
    return ref.at[pl.ds(pl.multiple_of(r8, SUB), SUB)]


def _out_router_kernel(hap_ref, has_ref, hbp_ref, hbs_ref, xp_ref, xs_ref, woa_ref, wob_ref, nf_ref, wr_ref, br_ref,
                       h1_o, u2_o, ids_o, gates_o, rank_o, cnt_o, carry, *, ntp):
    i = pl.program_id(0)

    @pl.when(i == 0)
    def _init():
        carry[...] = jnp.zeros_like(carry)

    ha = _pick(i, ntp, hap_ref, has_ref)
    hb = _pick(i, ntp, hbp_ref, hbs_ref)
    h1 = _pick(i, ntp, xp_ref, xs_ref) + _dot(ha, woa_ref[...]) + _dot(hb, wob_ref[...])
    h1_o[...] = h1
    u2 = _rms(h1, nf_ref[...])
    _rows_to_tiles(u2_o, u2)
    E = br_ref.shape[0]
    logits = _dot_nt(wr_ref[...], u2.astype(BF16)) + br_ref[...]
    TM = logits.shape[1]
    row_f = lax.broadcasted_iota(I32, (E, TM), 0).astype(F32)
    work = logits
    vals, idxs = [], []
    for _ in range(TOP_K):
        m = jnp.max(work, axis=0, keepdims=True)
        idx = jnp.min(jnp.where(work == m, row_f, float(E)), axis=0, keepdims=True)
        vals.append(m)
        idxs.append(idx)
        work = jnp.where(row_f == idx, -jnp.inf, work)
    es = [jnp.exp(v - vals[0]) for v in vals]
    tot = es[0] + es[1] + es[2] + es[3]
    gates = [e_ / tot for e_ in es]
    member = jnp.zeros((E, TM), F32)
    for kk in range(TOP_K):
        member = member + jnp.where(row_f == idxs[kk], 1.0, 0.0)
    r2 = lax.broadcasted_iota(I32, (TM, TM), 0)
    c2 = lax.broadcasted_iota(I32, (TM, TM), 1)
    before = _dot(member.astype(BF16), jnp.where(r2 < c2, 1.0, 0.0).astype(BF16)) + carry[:, 0:1]
    ranks = [jnp.sum(jnp.where(row_f == idxs[kk], before, 0.0), axis=0, keepdims=True) for kk in range(TOP_K)]
    pad = jnp.zeros((SUB - TOP_K, TM), F32)
    rows8 = lambda parts: jnp.concatenate(parts + [pad], axis=0)
    ids_o[...] = rows8(idxs).astype(I32)
    rank_o[...] = rows8(ranks).astype(I32)
    g8 = rows8(gates)
    sel = jnp.where(lax.broadcasted_iota(I32, (SUB, LANES), 0) == lax.broadcasted_iota(I32, (SUB, LANES), 1),
                    1.0, 0.0).astype(BF16)
    gates_o[...] = _dotx_l(sel, g8, f=lambda a, b: _dot_tn(b, a))
    carry[...] = carry[...] + jnp.sum(member, axis=1, keepdims=True)
    cnt_o[...] = carry[...].astype(I32)


def _out_router(hap, has, hbp, hbs, xp, xs, woa, wob, nf, wr, br):
    D = xp.shape[1]
    T = xp.shape[0] + xs.shape[0]
    ntp = xp.shape[0] // ROW_TILE
    nt = T // ROW_TILE
    row = lambda n: pl.BlockSpec((ROW_TILE, n), lambda i: (i, 0))
    full = lambda a: pl.BlockSpec(a.shape, lambda i: (0, 0))
    cnt = jax.ShapeDtypeStruct((br.shape[0], LANES), I32)
    return pl.pallas_call(
        functools.partial(_out_router_kernel, ntp=ntp),
        grid=(nt,),
        in_specs=_two_specs(ntp, A_WIDTH) + _two_specs(ntp, B_WIDTH) + _two_specs(ntp, D)
        + [full(woa), full(wob), full(nf), full(wr), full(br)],
        out_specs=[row(D), pl.BlockSpec((ROW_TILE * SUB, LANES), lambda i: (i, 0)),
                   pl.BlockSpec((SUB, ROW_TILE), lambda i: (0, i)), row(LANES),
                   pl.BlockSpec((SUB, ROW_TILE), lambda i: (0, i)), full(cnt)],
        out_shape=[jax.ShapeDtypeStruct((T, D), F32), jax.ShapeDtypeStruct((T * SUB, LANES), F32),
                   jax.ShapeDtypeStruct((SUB, T), I32), jax.ShapeDtypeStruct((T, LANES), F32),
                   jax.ShapeDtypeStruct((SUB, T), I32), cnt],
        scratch_shapes=[pltpu.VMEM(cnt.shape, F32)],
        compiler_params=_cparams(("arbitrary",)),
        name="out_router",
    )(hap, has, hbp, hbs, xp, xs, woa, wob, nf, wr, br)


MOE_BLOCK = 512


def _row_copy_wait(buf, sem, n):
    for _ in range(n):
        pltpu.make_async_copy(buf, buf, sem).wait()


DMA_UNROLL = 4


SCATTER_BUFS = 3


def _scatter_kernel(plo_ref, phi_ref, nu_ref, dest_ref, u_hbm, xs_o, zbuf, ubuf, sem, in_sems, out_sems):
    TM = ROW_TILE
    BLK = MOE_BLOCK * SUB
    NB = SCATTER_BUFS
    t = pl.program_id(0)
    nt = pl.num_programs(0)

    def fetch(i, slot):
        src = u_hbm.at[pl.ds(pl.multiple_of(i * (TM * SUB), TM * SUB), TM * SUB)]
        return pltpu.make_async_copy(src, ubuf.at[slot], in_sems.at[slot])

    @pl.when(t == 0)
    def _first_fetch():
        fetch(0, 0).start()

    @pl.when(t == 0)
    def _zero_unused_slots():
        zbuf[...] = jnp.zeros_like(zbuf)
        zrow = zbuf.at[pl.ds(0, SUB)]

        def per_expert(e, n):
            def per_row(rw, _):
                pltpu.make_async_copy(zrow, _row_tile(xs_o, rw * SUB), sem).start()
                return 0

            lax.fori_loop(plo_ref[e], phi_ref[e], per_row, 0)
            return n + (phi_ref[e] - plo_ref[e])

        n_rows = lax.fori_loop(0, plo_ref.shape[0], per_expert, 0)

        def wait_row(_, c):
            pltpu.make_async_copy(zrow, xs_o.at[pl.ds(0, SUB)], sem).wait()
            return c

        lax.fori_loop(0, n_rows, wait_row, 0)

        def per_block(j, _):
            cp = pltpu.make_async_copy(zbuf, xs_o.at[pl.ds(pl.multiple_of(j * BLK, BLK), BLK)], sem)
            cp.start()
            cp.wait()
            return 0

        lax.fori_loop(nu_ref[0], xs_o.shape[0] // BLK, per_block, 0)

    slot = lax.rem(t, NB)
    nslot = lax.rem(t + 1, NB)

    @pl.when(t + 1 < nt)
    def _next_fetch():
        @pl.when(t + 1 >= NB)
        def _buffer_free():
            _row_copy_wait(ubuf.at[nslot], out_sems.at[nslot], TOP_K)

        fetch(t + 1, nslot).start()

    fetch(t, slot).wait()
    cur = ubuf.at[slot]

    def body(i, _):
        src = _row_tile(cur, i * SUB)
        for kk in range(TOP_K):
            pltpu.make_async_copy(src, _row_tile(xs_o, dest_ref[i * TOP_K + kk]), out_sems.at[slot]).start(priority=kk % 2)
        return 0

    lax.fori_loop(0, TM, body, 0, unroll=DMA_UNROLL)

    @pl.when(t == nt - 1)
    def _drain():
        for s in range(NB):
            _row_copy_wait(ubuf.at[s], out_sems.at[s], TOP_K)


def _scatter_rows(pad_lo, pad_hi, n_used, dest8_flat, u2t, n_slots):
    T = u2t.shape[0] // SUB
    assert T // ROW_TILE >= SCATTER_BUFS
    grid_spec = pltpu.PrefetchScalarGridSpec(
        num_scalar_prefetch=3,
        grid=(T // ROW_TILE,),
        in_specs=[pl.BlockSpec((ROW_TILE * TOP_K,), lambda i, *_: (i,), memory_space=pltpu.SMEM),
                  pl.BlockSpec(memory_space=pl.ANY)],
        out_specs=pl.BlockSpec(memory_space=pl.ANY),
        scratch_shapes=[pltpu.VMEM((MOE_BLOCK * SUB, LANES), F32), pltpu.VMEM((SCATTER_BUFS, ROW_TILE * SUB, LANES), F32),
                        pltpu.SemaphoreType.DMA(()), pltpu.SemaphoreType.DMA((SCATTER_BUFS,)),
                        pltpu.SemaphoreType.DMA((SCATTER_BUFS,))],
    )
    return pl.pallas_call(
        _scatter_kernel,
        grid_spec=grid_spec,
        out_shape=jax.ShapeDtypeStruct((n_slots * SUB, LANES), F32),
        compiler_params=_cparams(("arbitrary",)),
        name="moe_scatter",
    )(pad_lo, pad_hi, n_used, dest8_flat, u2t)


def _expert_kernel(be_ref, nu_ref, nxt_ref, xs_ref, wgu_hbm, bgu_ref, wd_hbm, bd_ref, ys_o,
                   wgu_f, wd_f, wgu_b, wd_b, slot_ref, sems):
    j = pl.program_id(0)
    F = wd_b.shape[0]
    used = j < nu_ref[0]
    e = be_ref[j]

    def w_copies(eid, slot):
        return (pltpu.make_async_copy(wgu_hbm.at[eid], wgu_f.at[slot], sems.at[0, slot]),
                pltpu.make_async_copy(wd_hbm.at[eid], wd_f.at[slot], sems.at[1, slot]))

    @pl.when(j == 0)
    def _first_fetch():
        slot_ref[0] = 0
        for cp in w_copies(e, 0):
            cp.start()

    @pl.when(used & ((j == 0) | (e != be_ref[jnp.maximum(j - 1, 0)])))
    def _new_expert():
        slot = slot_ref[0]
        for cp in w_copies(e, slot):
            cp.wait()
        nx = nxt_ref[e]

        @pl.when(nx != e)
        def _prefetch():
            for cp in w_copies(nx, 1 - slot):
                cp.start()

        wgu_b[...] = wgu_f[slot].astype(BF16)
        wd_b[...] = wd_f[slot].astype(BF16)
        slot_ref[0] = 1 - slot

    @pl.when(used)
    def _compute():
        x = _tiles_to_rows(xs_ref, MOE_BLOCK)
        gu = _dot(x.astype(BF16), wgu_b[...]) + bgu_ref[0]
        g_ = jnp.minimum(gu[:, :F], SWIGLU_LIMIT)
        up = jnp.clip(gu[:, F:], -SWIGLU_LIMIT, SWIGLU_LIMIT)
        act = (up + 1.0) * g_ * _sigmoid(SWIGLU_ALPHA * g_)
        _rows_to_tiles(ys_o, _dot(act.astype(BF16), wd_b[...]) + bd_ref[0])

    @pl.when(j >= nu_ref[0])
    def _unused():
        ys_o[...] = jnp.zeros_like(ys_o)


def _experts(blk_e, n_used, nxt_e, xs, wgu, bgu, wd, bd):
    nb = xs.shape[0] // (MOE_BLOCK * SUB)
    E, D, F2 = wgu.shape
    F = F2 // 2
    grid_spec = pltpu.PrefetchScalarGridSpec(
        num_scalar_prefetch=3,
        grid=(nb,),
        in_specs=[pl.BlockSpec((MOE_BLOCK * SUB, LANES), lambda j, be, nu, nx: (jnp.minimum(j, nu[0] - 1), 0)),
                  pl.BlockSpec(memory_space=pl.ANY),
                  pl.BlockSpec((1, 1, F2), lambda j, be, nu, nx: (be[j], 0, 0)),
                  pl.BlockSpec(memory_space=pl.ANY),
                  pl.BlockSpec((1, 1, D), lambda j, be, nu, nx: (be[j], 0, 0))],
        out_specs=pl.BlockSpec((MOE_BLOCK * SUB, LANES), lambda j, be, nu, nx: (j, 0)),
        scratch_shapes=[pltpu.VMEM((2, D, F2), F32), pltpu.VMEM((2, F, D), F32),
                        pltpu.VMEM((D, F2), BF16), pltpu.VMEM((F, D), BF16),
                        pltpu.SMEM((1,), I32), pltpu.SemaphoreType.DMA((2, 2))],
    )
    return pl.pallas_call(
        _expert_kernel,
        grid_spec=grid_spec,
        out_shape=jax.ShapeDtypeStruct(xs.shape, F32),
        compiler_params=_cparams(("arbitrary",)),
        name="moe_experts",
    )(blk_e, n_used, nxt_e, xs, wgu, bgu, wd, bd)


def _combine_kernel(dest0_ref, destn_ref, h1_ref, pp_ref, ps_ref, gates_ref, ys_ref, npl_ref, wpg_ref, wpp_ref, fn_ref,
                    yp_o, ysm_o, gbuf, ybuf, sems, *, ntp, n_tiles):
    TM = h1_ref.shape[0]
    t = pl.program_id(0)

    def fetch(dref, i, slot):
        for kk in range(TOP_K):
            src = _row_tile(ys_ref, dref[i * TOP_K + kk])
            pltpu.make_async_copy(src, _row_tile(gbuf.at[slot, kk], i * SUB), sems.at[slot]).start(priority=kk % 2)

    @pl.when(t == 0)
    def _first_fetch():
        def body(i, _):
            fetch(dest0_ref, i, 0)
            return 0

        lax.fori_loop(0, TM, body, 0, unroll=DMA_UNROLL)

    def tile(slot):
        _row_copy_wait(gbuf.at[slot, 0], sems.at[slot], TOP_K)
        for i in range(TM):
            fetch(destn_ref, i, 1 - slot)
        gates = gates_ref[...]
        h2 = h1_ref[...]
        for kk in range(TOP_K):
            h2 = h2 + _tiles_to_rows(gbuf.at[slot, kk], TM) * gates[:, kk:kk + 1]
        u3 = _rms(h2, npl_ref[...])
        gate = _sigmoid(_dot(u3.astype(BF16), wpg_ref[...]))
        p = _pick(t, ntp, pp_ref, ps_ref)
        h3 = h2 + gate * _dot(p.astype(BF16), wpp_ref[...])
        return _rms(h3, fn_ref[...])

    for slot in range(2):
        @pl.when(t % 2 == slot)
        def _tile():
            ybuf[...] = tile(slot)

    @pl.when(t < ntp)
    def _prompt_rows():
        yp_o[...] = ybuf[...]

    @pl.when(t >= ntp)
    def _sample_rows():
        ysm_o[...] = ybuf[...]

    @pl.when(t == n_tiles - 1)
    def _drain():
        other = n_tiles % 2
        _row_copy_wait(gbuf.at[other, 0], sems.at[other], TOP_K)


def _combine(dest_flat, h1, pp, ps, gates, ys, npl, wpg, wpp, fn):
    T, D = h1.shape
    nt = T // ROW_TILE
    ntp = pp.shape[0] // ROW_TILE
    row = lambda n: pl.BlockSpec((ROW_TILE, n), lambda i: (i, 0))
    full = lambda a: pl.BlockSpec(a.shape, lambda i: (0, 0))
    idx = lambda f: pl.BlockSpec((ROW_TILE * TOP_K,), f, memory_space=pltpu.SMEM)
    return pl.pallas_call(
        functools.partial(_combine_kernel, ntp=ntp, n_tiles=nt),
        grid=(nt,),
        in_specs=[idx(lambda i: (0,)), idx(lambda i: (jnp.minimum(i + 1, nt - 1),)), row(D)]
        + _two_specs(ntp, pp.shape[1])
        + [row(LANES), pl.BlockSpec(memory_space=pl.ANY), full(npl), full(wpg), full(wpp), full(fn)],
        out_specs=_two_specs(ntp, D),
        out_shape=[jax.ShapeDtypeStruct((pp.shape[0], D), F32), jax.ShapeDtypeStruct((ps.shape[0], D), F32)],
        scratch_shapes=[pltpu.VMEM((2, TOP_K, ROW_TILE * SUB, LANES), F32), pltpu.VMEM((ROW_TILE, D), F32),
                        pltpu.SemaphoreType.DMA((2,))],
        compiler_params=_cparams(("arbitrary",)),
        name="moe_combine_ple",
    )(dest_flat, dest_flat, h1, pp, ps, gates, ys, npl, wpg, wpp, fn)


def _pad_lora_cols(x):
    W3 = 3 * B_WIDTH
    z = jnp.zeros(x.shape[:-1] + (LORA_PAD - 64,), x.dtype)
    return jnp.concatenate([x[..., :W3 + 64], z, x[..., W3 + 64:W3 + 128], z, x[..., W3 + 128:]], axis=-1)


def _unpad_lora_cols(x):
    W3 = 3 * B_WIDTH
    return jnp.concatenate([x[..., :W3 + 64], x[..., W3 + LORA_PAD:W3 + LORA_PAD + 64], x[..., W3 + 2 * LORA_PAD:]], axis=-1)


def _heads_to_pairs(S):
    n = S.shape[0]
    S = S.reshape(n, B_HEADS // 2, 2, B_HDIM, B_HDIM)
    z = jnp.zeros_like(S[:, :, 0])
    top = jnp.concatenate([S[:, :, 0], z], axis=-1)
    bot = jnp.concatenate([z, S[:, :, 1]], axis=-1)
    return jnp.concatenate([top, bot], axis=-2)


def _pairs_to_heads(Sb):
    n = Sb.shape[0]
    a = Sb[:, :, :B_HDIM, :B_HDIM]
    b = Sb[:, :, B_HDIM:, B_HDIM:]
    return jnp.stack([a, b], axis=2).reshape(n, B_HEADS, B_HDIM, B_HDIM)


def kernel(x_prompt, x_sample, state_mlstm_C, state_mlstm_n, state_mlstm_m, state_mlstm_conv, state_rwkv_S, state_rwkv_shift, p_prompt, p_sample, norm_mix, w_in, conv_w, conv_b, b_ig, b_fg, a_norm, shift_mu, w0, w_w2, a0, w_a2, w_g2, k_k, k_a, r_k, gn_w, gn_b, w_out, norm_ffn, w_router, b_router, w_gu, b_gu, w_down, b_down, norm_ple, w_ple_gate, w_ple_proj, final_norm):
    assert norm_mix.shape[0] == 1, "single-layer trunk"
    nbp, Lp, D = x_prompt.shape
    nbs, Ls, _ = x_sample.shape
    Tp, Ts = nbp * Lp, nbs * Ls
    T = Tp + Ts
    row = lambda a: a.reshape(1, -1).astype(F32)

    wi = w_in[0]
    AW2 = 2 * A_WIDTH
    wqk = wi[:, :AW2].astype(BF16)
    wv = wi[:, AW2:AW2 + A_WIDTH].astype(BF16)
    wo = wi[:, AW2 + A_WIDTH:AW2 + 2 * A_WIDTH].astype(BF16)
    a_cols = 4 * A_WIDTH + 2 * A_HEADS
    wif = jnp.pad(wi[:, 4 * A_WIDTH:a_cols], ((0, 0), (0, LANES - 2 * A_HEADS)))
    wbg = jnp.concatenate([_pad_lora_cols(wi[:, a_cols:]), wif], axis=1).astype(BF16)
    gate_bias = jnp.pad(jnp.concatenate([b_ig[0], b_fg[0]]), (0, LANES - 2 * A_HEADS)).reshape(1, LANES)
    padw = lambda w: jnp.pad(w, ((0, LORA_PAD - w.shape[0]), (0, 0)))
    hid = jnp.arange(B_WIDTH // 2) // B_HDIM
    bd = (hid[:, None] == hid[None, :]).astype(BF16)
    E = w_router.shape[-1]
    wr = w_router[0].T.astype(BF16)
    br = b_router[0].reshape(E, 1)

    assert Tp % ROW_TILE == 0 and Ts % ROW_TILE == 0
    xp, xsm = x_prompt.reshape(Tp, D), x_sample.reshape(Ts, D)
    qk, v, og, pb = _in_proj(xp, xsm, row(norm_mix), wqk, wv, wo, wbg)
    ifg = pb

    f0 = lambda *shp: jnp.zeros(shp, F32)
    ml_consts = (conv_w[0], row(conv_b), gate_bias, row(a_norm))
    m_s = jnp.pad(state_mlstm_m[0], ((0, 0), (0, LANES - A_HEADS))).reshape(nbs, 1, LANES)
    hap, Cp, np_, mp, convp = _mlstm(qk, v, og, ifg, f0(nbp, A_HEADS, A_HDIM, A_HDIM), f0(nbp, A_HEADS, A_HDIM),
                                     f0(nbp, 1, LANES), f0(nbp, CONV_W - 1, AW2), *ml_consts,
                                     n_seq=nbp, L=Lp, row_off=0)
    has, Cs, ns_, ms, convs = _mlstm(qk, v, og, ifg, state_mlstm_C[0], state_mlstm_n[0], m_s, state_mlstm_conv[0],
                                     *ml_consts, n_seq=nbs, L=Ls, row_off=Tp)

    rw_consts = (row(_pad_lora_cols(shift_mu[0])), row(w0), padw(w_w2[0]).astype(BF16), row(a0), padw(w_a2[0]).astype(BF16),
                 w_g2[0].astype(BF16),
                 row(k_k), row(k_a), row(r_k), row(gn_w), row(gn_b), bd)
    npair = B_HEADS // 2
    hbp, Sp, shp_ = _rwkv(pb, f0(nbp, npair, PAIR, PAIR), f0(nbp, 1, B_COLS_PAD), *rw_consts,
                          n_seq=nbp, L=Lp, row_off=0)
    hbs, Ss, shs = _rwkv(pb, _heads_to_pairs(state_rwkv_S[0]), _pad_lora_cols(state_rwkv_shift[0]), *rw_consts,
                         n_seq=nbs, L=Ls, row_off=Tp)

    wout = w_out[0].astype(BF16)
    hap, has = hap.reshape(Tp, A_WIDTH), has.reshape(Ts, A_WIDTH)
    h1, u2, ids, gates, rank, counts = _out_router(hap, has, hbp, hbs, xp, xsm, wout[:A_WIDTH], wout[A_WIDTH:],
                                                   row(norm_ffn), wr, br)

    counts = counts[:, 0]
    pcounts = (counts + MOE_BLOCK - 1) // MOE_BLOCK * MOE_BLOCK
    pend = jnp.cumsum(pcounts)
    pstart = pend - pcounts
    n_blk = -(-T * TOP_K // MOE_BLOCK) + E
    eid = jnp.arange(E, dtype=I32)
    base = jnp.sum(jnp.where(ids[None, :TOP_K] == eid[:, None, None], pstart[:, None, None], 0), axis=0)
    dest = ((base + rank[:TOP_K]) * SUB).T.reshape(-1).astype(I32)
    blk_start = jnp.arange(n_blk, dtype=I32) * MOE_BLOCK
    blk_e = jnp.minimum(jnp.sum(pend[None, :] <= blk_start[:, None], axis=1), E - 1).astype(I32)
    n_used = (pend[E - 1:] // MOE_BLOCK).astype(I32)

    xs = _scatter_rows((pstart + counts).astype(I32), pend.astype(I32), n_used, dest, u2, n_blk * MOE_BLOCK)
    later = jnp.where((eid[None, :] > eid[:, None]) & (pcounts[None, :] > 0), eid[None, :], E)
    nxt_e = jnp.min(later, axis=1)
    nxt_e = jnp.where(nxt_e < E, nxt_e, eid).astype(I32)
    ys = _experts(blk_e, n_used, nxt_e, xs, w_gu[0], b_gu[0][:, None, :], w_down[0], b_down[0][:, None, :])
    yp, ysm = _combine(dest, h1, p_prompt[0].reshape(Tp, -1), p_sample[0].reshape(Ts, -1), gates, ys, row(norm_ple),
                       w_ple_gate[0].astype(BF16), w_ple_proj[0].astype(BF16), row(final_norm))

    lead = lambda a: a[None]
    return (yp.reshape(nbp, Lp, D), ysm.reshape(nbs, Ls, D),
            lead(Cp), lead(np_), lead(mp[:, 0, :A_HEADS]), lead(convp), lead(_pairs_to_heads(Sp)), lead(_unpad_lora_cols(shp_)),
            lead(Cs), lead(ns_), lead(ms[:, 0, :A_HEADS]), lead(convs), lead(_pairs_to_heads(Ss)), lead(_unpad_lora_cols(shs)))
```

```python
import functools

import jax
import jax.numpy as jnp
from jax import lax
from jax.experimental import pallas as pl
from jax.experimental.pallas import tpu as pltpu

F32 = jnp.float32
BF16 = jnp.bfloat16
I32 = jnp.int32

EPS = 1e-6
GN_EPS = 64e-5
A_HEADS, A_HDIM = 4, 128
B_HEADS, B_HDIM = 8, 64
A_WIDTH = A_HEADS * A_HDIM
B_WIDTH = B_HEADS * B_HDIM
CONV_W = 4
LORA_PAD = 128
N_EXPERTS, TOP_K = 32, 4
SWIGLU_LIMIT, SWIGLU_ALPHA = 7.0, 1.702
LANES = 128
ROW_TILE = 256
VMEM_LIMIT = 56 * 1024 * 1024


def _cparams(sem):
    return pltpu.CompilerParams(dimension_semantics=sem, vmem_limit_bytes=VMEM_LIMIT)


def _dot(a, b):
    return lax.dot_general(a, b, (((1,), (0,)), ((), ())), preferred_element_type=F32)


def _dot_nt(a, b):
    return lax.dot_general(a, b, (((1,), (1,)), ((), ())), preferred_element_type=F32)


def _dot_tn(a, b):
    return lax.dot_general(a, b, (((0,), (0,)), ((), ())), preferred_element_type=F32)


def _split2(x):
    hi = x.astype(BF16)
    lo = (x - hi.astype(F32)).astype(BF16)
    return hi, lo


def _split3(x):
    hi = x.astype(BF16)
    r1 = x - hi.astype(F32)
    mid = r1.astype(BF16)
    lo = (r1 - mid.astype(F32)).astype(BF16)
    return hi, mid, lo


def _dot3(a, b, f=_dot):
    ah, al = _split2(a)
    bh, bl = _split2(b)
    return f(ah, bh) + (f(ah, bl) + f(al, bh))


def _dotx_l(a_exact, b, f=_dot, pieces=3):
    parts = _split3(b)[:pieces]
    acc = f(a_exact, parts[-1])
    for part in parts[-2::-1]:
        acc = f(a_exact, part) + acc
    return acc


def _rms(x, g):
    return x * lax.rsqrt(jnp.mean(x * x, axis=-1, keepdims=True) + EPS) * g


def _sigmoid(x):
    return 1.0 / (1.0 + jnp.exp(-x))


def _two_specs(ntp, n):
    return [pl.BlockSpec((ROW_TILE, n), lambda i: (jnp.minimum(i, ntp - 1), 0)),
            pl.BlockSpec((ROW_TILE, n), lambda i: (jnp.maximum(i - ntp, 0), 0))]


def _pick(i, ntp, p_ref, s_ref):
    return jnp.where(i < ntp, p_ref[...], s_ref[...])


def _in_proj_kernel(xp_ref, xs_ref, g_ref, wqk_ref, wv_ref, wo_ref, wbg_ref,
                    qk_o, v_o, og_o, pbg_o, *, ntp):
    x = _pick(pl.program_id(0), ntp, xp_ref, xs_ref)
    ub = _rms(x, g_ref[...]).astype(BF16)
    qk_o[...] = _dot(ub, wqk_ref[...])
    v_o[...] = _dot(ub, wv_ref[...])
    og_o[...] = _dot(ub, wo_ref[...])
    pbg_o[...] = _dot(ub, wbg_ref[...])


def _in_proj(xp, xs, g, wqk, wv, wo, wbg):
    D = xp.shape[1]
    T = xp.shape[0] + xs.shape[0]
    ntp = xp.shape[0] // ROW_TILE
    row = lambda n: pl.BlockSpec((ROW_TILE, n), lambda i: (i, 0))
    full = lambda a: pl.BlockSpec(a.shape, lambda i: (0, 0))
    outs = [wqk.shape[1], wv.shape[1], wo.shape[1], wbg.shape[1]]
    return pl.pallas_call(
        functools.partial(_in_proj_kernel, ntp=ntp),
        grid=(T // ROW_TILE,),
        in_specs=_two_specs(ntp, D) + [full(g), full(wqk), full(wv), full(wo), full(wbg)],
        out_specs=[row(n) for n in outs],
        out_shape=[jax.ShapeDtypeStruct((T, n), F32) for n in outs],
        compiler_params=_cparams(("arbitrary",)),
        name="in_proj",
    )(xp, xs, g, wqk, wv, wo, wbg)


MLSTM_CHUNK = 256
NEG_BIG = -1e30


def _log_sigmoid(x):
    return jnp.minimum(x, 0.0) - jnp.log1p(jnp.exp(-jnp.abs(x)))


MLSTM_GROUP = 1


def _mlstm_kernel(*refs, nv, CH):
    G = MLSTM_GROUP
    tok = [refs[4 * j:4 * j + 4] for j in range(G)]
    (C0_ref, n0_ref, m0_ref, conv0_ref, cw_ref, cb_ref, gb_ref, an_ref,
     ha_o, C1_o, n1_o, m1_o, conv1_o, C_sc, n_sc, m_sc, cbuf) = refs[4 * G:]
    c = pl.program_id(1)

    @pl.when(c == 0)
    def _init():
        C_sc[...] = C0_ref[...]
        for j in range(G):
            n_sc[j, 0:A_HEADS] = n0_ref[j]
            m_sc[j, 0:1] = m0_ref[j]
            cbuf[j, 5:8] = conv0_ref[j]

    row = lax.broadcasted_iota(I32, (CH, LANES), 0)
    r2 = lax.broadcasted_iota(I32, (CH, CH), 0)
    c2 = lax.broadcasted_iota(I32, (CH, CH), 1)
    tri = r2 >= c2
    tril_b = jnp.where(tri, 1.0, 0.0).astype(BF16)
    sel = jnp.where(lax.broadcasted_iota(I32, (8, LANES), 0) == lax.broadcasted_iota(I32, (8, LANES), 1),
                    1.0, 0.0).astype(BF16)
    lane1 = lax.broadcasted_iota(I32, (1, LANES), 1)

    q, k, v, og, b, g, gT, conv_tail, m_row = [], [], [], [], [], [], [], [], []
    for j, (qk_ref, v_ref, og_ref, if_ref) in enumerate(tok):
        cb_j = cbuf.at[j]
        cb_j[8:8 + nv] = qk_ref[...]
        if nv < CH:
            cb_j[8 + nv:8 + CH] = jnp.zeros((CH - nv, 2 * A_WIDTH), F32)
        y = cb_ref[...] + cb_j[5:5 + CH] * cw_ref[0:1]
        for tap in range(1, CONV_W):
            y = y + cb_j[5 + tap:5 + tap + CH] * cw_ref[tap:tap + 1]
        conv_tail.append(cb_j[5 + nv:8 + nv])
        qk = y * _sigmoid(y)
        q.append(qk[:, :A_WIDTH])
        k.append(qk[:, A_WIDTH:] * (A_HDIM ** -0.5))
        if nv == CH:
            v.append(v_ref[...])
            og.append(og_ref[...])
            gp = if_ref[...] + gb_ref[...]
        else:
            zpad = lambda n: jnp.zeros((CH - nv, n), F32)
            v.append(jnp.concatenate([v_ref[...], zpad(A_WIDTH)], axis=0))
            og.append(jnp.concatenate([og_ref[...], zpad(A_WIDTH)], axis=0))
            gp = jnp.concatenate([if_ref[...] + gb_ref[...], zpad(LANES)], axis=0)
        li = gp
        lf = pltpu.roll(_log_sigmoid(gp), LANES - A_HEADS, 1)
        if nv < CH:
            li = jnp.where(row < nv, li, NEG_BIG)
            lf = jnp.where(row < nv, lf, 0.0)
        b.append(_dotx_l(tril_b, lf))
        g.append(li - b[j])
        gT.append(_dotx_l(sel, g[j], f=_dot_nt))
        m_row.append(m_sc[j, 0:1])
    qb = [x.astype(BF16) for x in q]
    kb = [x.astype(BF16) for x in k]
    vb = [x.astype(BF16) for x in v]

    U = [(j, h) for j in range(G) for h in range(A_HEADS)]
    sl = lambda h: slice(h * A_HDIM, (h + 1) * A_HDIM)
    bcol = [b[j][:, h:h + 1] for j, h in U]
    m_h = [m_row[j][:, h:h + 1] for j, h in U]
    C_h = [C_sc[j, h] for j, h in U]
    n_h = [n_sc[j, h:h + 1] for j, h in U]
    qk_t = [_dot_nt(qb[j][:, sl(h)], kb[j][:, sl(h)]) for j, h in U]
    qC = [_dot_nt(qb[j][:, sl(h)], C_h[u].astype(BF16)) for u, (j, h) in enumerate(U)]
    dm = [jnp.where(tri, bcol[u] + gT[j][h:h + 1, :], -jnp.inf) for u, (j, h) in enumerate(U)]
    inter = [bcol[u] + m_h[u] for u in range(len(U))]
    mt = [jnp.maximum(inter[u], jnp.max(dm[u], axis=1, keepdims=True)) for u in range(len(U))]
    s = [qk_t[u] * jnp.exp(dm[u] - mt[u]) for u in range(len(U))]
    wi = [jnp.exp(inter[u] - mt[u]) for u in range(len(U))]
    sv = [_dot(s[u].astype(BF16), vb[j][:, sl(h)]) for u, (j, h) in enumerate(U)]
    m_new = [mt[u][CH - 1:CH] for u in range(len(U))]
    w_end = [jnp.exp(bcol[u][CH - 1:CH] + g[j][:, h:h + 1] - m_new[u]) for u, (j, h) in enumerate(U)]
    dec = [jnp.exp(bcol[u][CH - 1:CH] + m_h[u] - m_new[u]) for u in range(len(U))]
    vk = [_dot_tn((v[j][:, sl(h)] * w_end[u]).astype(BF16), kb[j][:, sl(h)]) for u, (j, h) in enumerate(U)]
    outs = [[] for _ in range(G)]
    m_next = list(m_row)
    for u, (j, h) in enumerate(U):
        num = sv[u] + wi[u] * qC[u]
        den = jnp.sum(s[u], axis=1, keepdims=True) + wi[u] * jnp.sum(q[j][:, sl(h)] * n_h[u], axis=1, keepdims=True)
        hh = num / jnp.maximum(jnp.abs(den), jnp.exp(-mt[u]))
        hn = hh * lax.rsqrt(jnp.mean(hh * hh, axis=1, keepdims=True) + EPS)
        outs[j].append(hn * an_ref[:, sl(h)] * _sigmoid(og[j][:, sl(h)]))
        C_sc[j, h] = dec[u] * C_h[u] + vk[u]
        n_sc[j, h:h + 1] = dec[u] * n_h[u] + jnp.sum(k[j][:, sl(h)] * w_end[u], axis=0, keepdims=True)
        m_next[j] = jnp.where(lane1 == h, m_new[u], m_next[j])
    for j in range(G):
        m_sc[j, 0:1] = m_next[j]
        ha_o[j] = jnp.concatenate(outs[j], axis=1)[:nv].astype(BF16)
        cbuf[j, 5:8] = conv_tail[j]

    @pl.when(c == pl.num_programs(1) - 1)
    def _fin():
        C1_o[...] = C_sc[...]
        for j in range(G):
            n1_o[j] = n_sc[j, 0:A_HEADS]
            m1_o[j] = m_sc[j, 0:1]
            conv1_o[j] = conv_tail[j]


def _mlstm(qk, v, og, pbg, C0, n0, m0, conv0, cw, cb, gb, an, *, n_seq, L, row_off):
    G = MLSTM_GROUP
    CH = min(L, MLSTM_CHUNK) if L % SUB == 0 else MLSTM_CHUNK
    assert n_seq % G == 0
    nv = min(L, CH)
    nch = L // nv
    off = row_off // nv
    gate_tile = pbg.shape[1] // LANES - 1
    st = lambda shp: pl.BlockSpec((G,) + shp, lambda s, c: (s,) + (0,) * len(shp))
    full = lambda a: pl.BlockSpec(a.shape, lambda s, c: (0,) * a.ndim)
    in_specs, args = [], []
    for j in range(G):
        rows = lambda s, c, j=j: off + (G * s + j) * nch + c
        in_specs += [pl.BlockSpec((nv, 2 * A_WIDTH), lambda s, c, r=rows: (r(s, c), 0)),
                     pl.BlockSpec((nv, A_WIDTH), lambda s, c, r=rows: (r(s, c), 0)),
                     pl.BlockSpec((nv, A_WIDTH), lambda s, c, r=rows: (r(s, c), 0)),
                     pl.BlockSpec((nv, LANES), lambda s, c, r=rows: (r(s, c), gate_tile))]
        args += [qk, v, og, pbg]
    in_specs += [st((A_HEADS, A_HDIM, A_HDIM)), st((A_HEADS, A_HDIM)), st((1, LANES)),
                 st((CONV_W - 1, 2 * A_WIDTH)), full(cw), full(cb), full(gb), full(an)]
    args += [C0, n0, m0, conv0, cw, cb, gb, an]
    out_shape = [jax.ShapeDtypeStruct((n_seq, L, A_WIDTH), BF16),
                 jax.ShapeDtypeStruct((n_seq, A_HEADS, A_HDIM, A_HDIM), F32),
                 jax.ShapeDtypeStruct((n_seq, A_HEADS, A_HDIM), F32),
                 jax.ShapeDtypeStruct((n_seq, 1, LANES), F32),
                 jax.ShapeDtypeStruct((n_seq, CONV_W - 1, 2 * A_WIDTH), F32)]
    out_specs = [pl.BlockSpec((G, nv, A_WIDTH), lambda s, c: (s, c, 0)),
                 st((A_HEADS, A_HDIM, A_HDIM)), st((A_HEADS, A_HDIM)),
                 st((1, LANES)), st((CONV_W - 1, 2 * A_WIDTH))]
    return pl.pallas_call(
        functools.partial(_mlstm_kernel, nv=nv, CH=CH),
        grid=(n_seq // G, nch),
        in_specs=in_specs, out_specs=out_specs, out_shape=out_shape,
        scratch_shapes=[pltpu.VMEM((G, A_HEADS, A_HDIM, A_HDIM), F32), pltpu.VMEM((G, 8, A_HDIM), F32),
                        pltpu.VMEM((G, 8, LANES), F32), pltpu.VMEM((G, CH + 8, 2 * A_WIDTH), F32)],
        compiler_params=_cparams(("arbitrary", "arbitrary")),
        name="mlstm",
    )(*args)


RWKV_CHUNK = 64
RWKV_NCHUNK = 8
B_COLS_PAD = 3 * B_WIDTH + 2 * LORA_PAD + 128
PAIR = 2 * B_HDIM


def _softplus(z):
    return jnp.maximum(z, 0.0) + jnp.log1p(jnp.exp(-jnp.abs(z)))


def _rwkv_kernel(pb_ref, S0_ref, sh0_ref, mu_ref, w0_ref, ww2_ref, a0_ref, wa2_ref, wg2_ref,
                 kk_ref, ka_ref, rk_ref, gw_ref, gb_ref, bd_ref,
                 hb_o, S1_o, sh1_o, S_sc, sbuf, *, nv):
    C = RWKV_CHUNK
    nc = -(-nv // C)
    TB = nc * C
    t = pl.program_id(1)

    @pl.when(t == 0)
    def _init():
        S_sc[...] = S0_ref[0]
        sbuf[7:8] = sh0_ref[0]

    sbuf[8:8 + nv] = pb_ref[...]
    if nv < TB:
        sbuf[8 + nv:8 + TB] = jnp.zeros((TB - nv, B_COLS_PAD), F32)
    pb = sbuf[8:8 + TB]
    prev = sbuf[7:7 + TB]
    last_row = sbuf[7 + nv:8 + nv]
    xm = pb + (prev - pb) * mu_ref[...]
    W = B_WIDTH
    r = xm[:, 0:W]
    kb = xm[:, W:2 * W]
    vb = xm[:, 2 * W:3 * W]
    wl = xm[:, 3 * W:3 * W + LORA_PAD]
    al = xm[:, 3 * W + LORA_PAD:3 * W + 2 * LORA_PAD]
    gl = xm[:, 3 * W + 2 * LORA_PAD:]
    def seg_sum(z):
        hw = bd_ref.shape[0]
        zb = z.astype(BF16)
        return jnp.concatenate([_dot(zb[:, i * hw:(i + 1) * hw], bd_ref[...]) for i in range(B_WIDTH // hw)], axis=1)
    w_log = -_softplus(-(w0_ref[...] + _dot(jnp.tanh(wl).astype(BF16), ww2_ref[...]))) - 0.5
    lw = -jnp.exp(w_log)
    a = _sigmoid(a0_ref[...] + _dot(al.astype(BF16), wa2_ref[...]))
    g = _dot(_sigmoid(gl).astype(BF16), wg2_ref[...])
    kkr = kb * kk_ref[...]
    kk = kkr / jnp.maximum(jnp.sqrt(seg_sum(kkr * kkr)), 1e-12)
    k = kb * (1.0 + (a - 1.0) * ka_ref[...])
    bonus = seg_sum(r * k * rk_ref[...]) * vb
    if nv < TB:
        valid = lax.broadcasted_iota(I32, (TB, W), 0) < nv
        zero = lambda z: jnp.where(valid, z, 0.0)
        lw, kk, k, vb = zero(lw), zero(kk), zero(k), zero(vb)

    lane = lax.broadcasted_iota(I32, (C, PAIR), 1)
    h0 = lane < B_HDIM
    sep = lambda z: jnp.concatenate([jnp.where(h0, z, 0.0), jnp.where(h0, 0.0, z)], axis=0)
    ri = lax.broadcasted_iota(I32, (C, C), 0)
    ci = lax.broadcasted_iota(I32, (C, C), 1)
    tril_c = jnp.where(ri >= ci, 1.0, 0.0).astype(BF16)
    r2 = lax.broadcasted_iota(I32, (PAIR, PAIR), 0)
    c2 = lax.broadcasted_iota(I32, (PAIR, PAIR), 1)
    low_s = r2 > c2
    low_i = r2 >= c2
    eye = jnp.where(r2 == c2, 1.0, 0.0)
    npair = B_HEADS // 2
    units = [(c, p) for c in range(nc) for p in range(npair)]
    cat0 = lambda x, y: jnp.concatenate([x, y], axis=0)
    cat1 = lambda x, y: jnp.concatenate([x, y], axis=1)
    bf = lambda z: z.astype(BF16)
    lhs, rhs_t, vs, wc = [], [], [], []
    bcum = [_dotx_l(tril_c, lw[c * C:(c + 1) * C], pieces=2) for c in range(nc)]
    for (c, p) in units:
        rs = slice(c * C, (c + 1) * C)
        ls = slice(p * PAIR, (p + 1) * PAIR)
        lwc = lw[rs, ls]
        b = bcum[c][:, ls]
        emb = jnp.exp(-b)
        kkc = kk[rs, ls]
        lhs.append(bf(cat0(sep(kkc * jnp.exp(b - lwc)), sep(r[rs, ls] * jnp.exp(b)))))
        rhs_t.append(bf(cat0(sep(k[rs, ls] * emb), sep(kkc * a[rs, ls] * emb))))
        vs.append(bf(sep(vb[rs, ls])))
        wc.append(jnp.exp(b[C - 1:C]))
    A = [_dot_nt(l_, r_) for l_, r_ in zip(lhs, rhs_t)]
    Ak = [jnp.where(low_s, x[:PAIR, :PAIR], 0.0) for x in A]
    Ab = [jnp.where(low_s, x[:PAIR, PAIR:], 0.0) for x in A]
    Ark = [jnp.where(low_i, x[PAIR:, :PAIR], 0.0) for x in A]
    Arb = [jnp.where(low_i, x[PAIR:, PAIR:], 0.0) for x in A]
    X2 = [_dot(bf(cat0(x, y)), v_) for x, y, v_ in zip(Ak, Ark, vs)]
    VtK = [_dot_tn(v_, r_[:PAIR]) for v_, r_ in zip(vs, rhs_t)]
    Tm = [eye - x for x in Ab]
    P = [_dot(bf(x), bf(x)) for x in Ab]
    n = 2
    while 2 * n < C:
        X = [_dot(bf(cat0(t_, p_)), bf(p_)) for t_, p_ in zip(Tm, P)]
        Tm = [t_ + x[:PAIR] for t_, x in zip(Tm, X)]
        P = [x[PAIR:] for x in X]
        n *= 2
    Tm = [t_ + _dot(bf(t_), bf(p_)) for t_, p_ in zip(Tm, P)]
    Y = [_dot(bf(t_), cat1(l_[:PAIR], bf(x[:PAIR]))) for t_, l_, x in zip(Tm, lhs, X2)]
    Z = [_dot(bf(x), bf(y)) for x, y in zip(Arb, Y)]
    Rp = [bf(l_[PAIR:].astype(F32) - z[:, :PAIR]) for l_, z in zip(lhs, Z)]
    Op = [x[PAIR:] - z[:, PAIR:] for x, z in zip(X2, Z)]
    W = [_dot_tn(bf(y), r_[PAIR:]) for y, r_ in zip(Y, rhs_t)]
    Mc = [bf(w_[:PAIR] * c_) for w_, c_ in zip(W, wc)]
    Nn = [(v_ - w_[PAIR:]) * c_ for v_, w_, c_ in zip(VtK, W, wc)]
    o_rows = []
    for c in range(nc):
        S = [S_sc[p] for p in range(npair)]
        Sb = [bf(s_) for s_ in S]
        u0 = c * npair
        O = [_dot_nt(Rp[u0 + p], Sb[p]) + Op[u0 + p] for p in range(npair)]
        for p in range(npair):
            S_sc[p] = S[p] * wc[u0 + p] - _dot(Sb[p], Mc[u0 + p]) + Nn[u0 + p]
        o_rows.append(jnp.concatenate([x[:C] + x[C:] for x in O], axis=1))
    o = o_rows[0] if nc == 1 else jnp.concatenate(o_rows, axis=0)

    mean = seg_sum(o) * (1.0 / B_HDIM)
    d = o - mean
    var = seg_sum(d * d) * (1.0 / B_HDIM)
    hb = (d * lax.rsqrt(var + GN_EPS) * gw_ref[...] + gb_ref[...] + bonus) * g
    hb_o[...] = hb[:nv].astype(BF16)
    sbuf[7:8] = last_row

    @pl.when(t == pl.num_programs(1) - 1)
    def _fin():
        S1_o[0] = S_sc[...]
        sh1_o[0] = last_row


def _rwkv(pb, S0, sh0, mu, w0, ww2, a0, wa2, wg2, k_k, k_a, r_k, gn_w, gn_b, bd, *, n_seq, L, row_off):
    nv = min(L, RWKV_CHUNK * RWKV_NCHUNK)
    nst = L // nv
    off = row_off // nv
    TB = -(-nv // RWKV_CHUNK) * RWKV_CHUNK
    tok = lambda n: pl.BlockSpec((nv, n), lambda s, t: (off + s * nst + t, 0))
    st = lambda shp: pl.BlockSpec((1,) + shp, lambda s, t: (s,) + (0,) * len(shp))
    full = lambda a: pl.BlockSpec(a.shape, lambda s, t: (0,) * a.ndim)
    consts = [mu, w0, ww2, a0, wa2, wg2, k_k, k_a, r_k, gn_w, gn_b, bd]
    npair = B_HEADS // 2
    in_specs = [tok(B_COLS_PAD), st((npair, PAIR, PAIR)), st((1, B_COLS_PAD))] + [full(a) for a in consts]
    args = [pb, S0, sh0] + consts
    out_shape = [jax.ShapeDtypeStruct((n_seq * L, B_WIDTH), BF16),
                 jax.ShapeDtypeStruct((n_seq, npair, PAIR, PAIR), F32),
                 jax.ShapeDtypeStruct((n_seq, 1, B_COLS_PAD), F32)]
    out_specs = [pl.BlockSpec((nv, B_WIDTH), lambda s, t: (s * nst + t, 0)),
                 st((npair, PAIR, PAIR)), st((1, B_COLS_PAD))]
    return pl.pallas_call(
        functools.partial(_rwkv_kernel, nv=nv),
        grid=(n_seq, nst),
        in_specs=in_specs, out_specs=out_specs, out_shape=out_shape,
        scratch_shapes=[pltpu.VMEM((npair, PAIR, PAIR), F32), pltpu.VMEM((TB + 8, B_COLS_PAD), F32)],
        compiler_params=_cparams(("arbitrary", "arbitrary")),
        name="rwkv",
    )(*args)


SUB = 8


def _rows_to_tiles(ref, x):
    rows = x.shape[0]
    for j in range(SUB):
        ref[pl.ds(j, rows, stride=SUB), :] = x[:, j * LANES:(j + 1) * LANES]


def _tiles_to_rows(ref, rows):
    return jnp.concatenate([ref[pl.ds(j, rows, stride=SUB), :] for j in range(SUB)], axis=1)


def _row_tile(ref, r8):
    return ref.at[pl.ds(pl.multiple_of(r8, SUB), SUB)]


def _out_router_kernel(hap_ref, has_ref, hbp_ref, hbs_ref, xp_ref, xs_ref, woa_ref, wob_ref, nf_ref, wr_ref, br_ref,
                       h1_o, u2_o, ids_o, gates_o, rank_o, cnt_o, carry, *, ntp):
    i = pl.program_id(0)

    @pl.when(i == 0)
    def _init():
        carry[...] = jnp.zeros_like(carry)

    ha = _pick(i, ntp, hap_ref, has_ref)
    hb = _pick(i, ntp, hbp_ref, hbs_ref)
    h1 = _pick(i, ntp, xp_ref, xs_ref) + _dot(ha, woa_ref[...]) + _dot(hb, wob_ref[...])
    h1_o[...] = h1
    u2 = _rms(h1, nf_ref[...])
    _rows_to_tiles(u2_o, u2)
    E = br_ref.shape[0]
    logits = _dot_nt(wr_ref[...], u2.astype(BF16)) + br_ref[...]
    TM = logits.shape[1]
    row_f = lax.broadcasted_iota(I32, (E, TM), 0).astype(F32)
    work = logits
    vals, idxs = [], []
    for _ in range(TOP_K):
        m = jnp.max(work, axis=0, keepdims=True)
        idx = jnp.min(jnp.where(work == m, row_f, float(E)), axis=0, keepdims=True)
        vals.append(m)
        idxs.append(idx)
        work = jnp.where(row_f == idx, -jnp.inf, work)
    es = [jnp.exp(v - vals[0]) for v in vals]
    tot = es[0] + es[1] + es[2] + es[3]
    gates = [e_ / tot for e_ in es]
    member = jnp.zeros((E, TM), F32)
    for kk in range(TOP_K):
        member = member + jnp.where(row_f == idxs[kk], 1.0, 0.0)
    r2 = lax.broadcasted_iota(I32, (TM, TM), 0)
    c2 = lax.broadcasted_iota(I32, (TM, TM), 1)
    before = _dot(member.astype(BF16), jnp.where(r2 < c2, 1.0, 0.0).astype(BF16)) + carry[:, 0:1]
    ranks = [jnp.sum(jnp.where(row_f == idxs[kk], before, 0.0), axis=0, keepdims=True) for kk in range(TOP_K)]
    pad = jnp.zeros((SUB - TOP_K, TM), F32)
    rows8 = lambda parts: jnp.concatenate(parts + [pad], axis=0)
    ids_o[...] = rows8(idxs).astype(I32)
    rank_o[...] = rows8(ranks).astype(I32)
    g8 = rows8(gates)
    sel = jnp.where(lax.broadcasted_iota(I32, (SUB, LANES), 0) == lax.broadcasted_iota(I32, (SUB, LANES), 1),
                    1.0, 0.0).astype(BF16)
    gates_o[...] = _dotx_l(sel, g8, f=lambda a, b: _dot_tn(b, a))
    carry[...] = carry[...] + jnp.sum(member, axis=1, keepdims=True)
    cnt_o[...] = carry[...].astype(I32)


def _out_router(hap, has, hbp, hbs, xp, xs, woa, wob, nf, wr, br):
    D = xp.shape[1]
    T = xp.shape[0] + xs.shape[0]
    ntp = xp.shape[0] // ROW_TILE
    nt = T // ROW_TILE
    row = lambda n: pl.BlockSpec((ROW_TILE, n), lambda i: (i, 0))
    full = lambda a: pl.BlockSpec(a.shape, lambda i: (0, 0))
    cnt = jax.ShapeDtypeStruct((br.shape[0], LANES), I32)
    return pl.pallas_call(
        functools.partial(_out_router_kernel, ntp=ntp),
        grid=(nt,),
        in_specs=_two_specs(ntp, A_WIDTH) + _two_specs(ntp, B_WIDTH) + _two_specs(ntp, D)
        + [full(woa), full(wob), full(nf), full(wr), full(br)],
        out_specs=[row(D), pl.BlockSpec((ROW_TILE * SUB, LANES), lambda i: (i, 0)),
                   pl.BlockSpec((SUB, ROW_TILE), lambda i: (0, i)), row(LANES),
                   pl.BlockSpec((SUB, ROW_TILE), lambda i: (0, i)), full(cnt)],
        out_shape=[jax.ShapeDtypeStruct((T, D), F32), jax.ShapeDtypeStruct((T * SUB, LANES), F32),
                   jax.ShapeDtypeStruct((SUB, T), I32), jax.ShapeDtypeStruct((T, LANES), F32),
                   jax.ShapeDtypeStruct((SUB, T), I32), cnt],
        scratch_shapes=[pltpu.VMEM(cnt.shape, F32)],
        compiler_params=_cparams(("arbitrary",)),
        name="out_router",
    )(hap, has, hbp, hbs, xp, xs, woa, wob, nf, wr, br)


MOE_BLOCK = 512


def _row_copy_wait(buf, sem, n):
    for _ in range(n):
        pltpu.make_async_copy(buf, buf, sem).wait()


DMA_UNROLL = 4


SCATTER_BUFS = 3


def _scatter_kernel(plo_ref, phi_ref, nu_ref, dest_ref, u_hbm, xs_o, zbuf, ubuf, sem, in_sems, out_sems):
    TM = ROW_TILE
    BLK = MOE_BLOCK * SUB
    NB = SCATTER_BUFS
    t = pl.program_id(0)
    nt = pl.num_programs(0)

    def fetch(i, slot):
        src = u_hbm.at[pl.ds(pl.multiple_of(i * (TM * SUB), TM * SUB), TM * SUB)]
        return pltpu.make_async_copy(src, ubuf.at[slot], in_sems.at[slot])

    @pl.when(t == 0)
    def _first_fetch():
        fetch(0, 0).start()

    @pl.when(t == 0)
    def _zero_unused_slots():
        zbuf[...] = jnp.zeros_like(zbuf)
        zrow = zbuf.at[pl.ds(0, SUB)]

        def per_expert(e, n):
            def per_row(rw, _):
                pltpu.make_async_copy(zrow, _row_tile(xs_o, rw * SUB), sem).start()
                return 0

            lax.fori_loop(plo_ref[e], phi_ref[e], per_row, 0)
            return n + (phi_ref[e] - plo_ref[e])

        n_rows = lax.fori_loop(0, plo_ref.shape[0], per_expert, 0)

        def wait_row(_, c):
            pltpu.make_async_copy(zrow, xs_o.at[pl.ds(0, SUB)], sem).wait()
            return c

        lax.fori_loop(0, n_rows, wait_row, 0)

        def per_block(j, _):
            cp = pltpu.make_async_copy(zbuf, xs_o.at[pl.ds(pl.multiple_of(j * BLK, BLK), BLK)], sem)
            cp.start()
            cp.wait()
            return 0

        lax.fori_loop(nu_ref[0], xs_o.shape[0] // BLK, per_block, 0)

    slot = lax.rem(t, NB)
    nslot = lax.rem(t + 1, NB)

    @pl.when(t + 1 < nt)
    def _next_fetch():
        @pl.when(t + 1 >= NB)
        def _buffer_free():
            _row_copy_wait(ubuf.at[nslot], out_sems.at[nslot], TOP_K)

        fetch(t + 1, nslot).start()

    fetch(t, slot).wait()
    cur = ubuf.at[slot]

    def body(i, _):
        src = _row_tile(cur, i * SUB)
        for kk in range(TOP_K):
            pltpu.make_async_copy(src, _row_tile(xs_o, dest_ref[i * TOP_K + kk]), out_sems.at[slot]).start(priority=kk % 2)
        return 0

    lax.fori_loop(0, TM, body, 0, unroll=DMA_UNROLL)

    @pl.when(t == nt - 1)
    def _drain():
        for s in range(NB):
            _row_copy_wait(ubuf.at[s], out_sems.at[s], TOP_K)


def _scatter_rows(pad_lo, pad_hi, n_used, dest8_flat, u2t, n_slots):
    T = u2t.shape[0] // SUB
    assert T // ROW_TILE >= SCATTER_BUFS
    grid_spec = pltpu.PrefetchScalarGridSpec(
        num_scalar_prefetch=3,
        grid=(T // ROW_TILE,),
        in_specs=[pl.BlockSpec((ROW_TILE * TOP_K,), lambda i, *_: (i,), memory_space=pltpu.SMEM),
                  pl.BlockSpec(memory_space=pl.ANY)],
        out_specs=pl.BlockSpec(memory_space=pl.ANY),
        scratch_shapes=[pltpu.VMEM((MOE_BLOCK * SUB, LANES), F32), pltpu.VMEM((SCATTER_BUFS, ROW_TILE * SUB, LANES), F32),
                        pltpu.SemaphoreType.DMA(()), pltpu.SemaphoreType.DMA((SCATTER_BUFS,)),
                        pltpu.SemaphoreType.DMA((SCATTER_BUFS,))],
    )
    return pl.pallas_call(
        _scatter_kernel,
        grid_spec=grid_spec,
        out_shape=jax.ShapeDtypeStruct((n_slots * SUB, LANES), F32),
        compiler_params=_cparams(("arbitrary",)),
        name="moe_scatter",
    )(pad_lo, pad_hi, n_used, dest8_flat, u2t)


def _expert_kernel(be_ref, nu_ref, nxt_ref, xs_ref, wgu_hbm, bgu_ref, wd_hbm, bd_ref, ys_o,
                   wgu_f, wd_f, wgu_b, wd_b, slot_ref, sems):
    j = pl.program_id(0)
    F = wd_b.shape[0]
    used = j < nu_ref[0]
    e = be_ref[j]

    def w_copies(eid, slot):
        return (pltpu.make_async_copy(wgu_hbm.at[eid], wgu_f.at[slot], sems.at[0, slot]),
                pltpu.make_async_copy(wd_hbm.at[eid], wd_f.at[slot], sems.at[1, slot]))

    @pl.when(j == 0)
    def _first_fetch():
        slot_ref[0] = 0
        for cp in w_copies(e, 0):
            cp.start()

    @pl.when(used & ((j == 0) | (e != be_ref[jnp.maximum(j - 1, 0)])))
    def _new_expert():
        slot = slot_ref[0]
        for cp in w_copies(e, slot):
            cp.wait()
        nx = nxt_ref[e]

        @pl.when(nx != e)
        def _prefetch():
            for cp in w_copies(nx, 1 - slot):
                cp.start()

        wgu_b[...] = wgu_f[slot].astype(BF16)
        wd_b[...] = wd_f[slot].astype(BF16)
        slot_ref[0] = 1 - slot

    @pl.when(used)
    def _compute():
        x = _tiles_to_rows(xs_ref, MOE_BLOCK)
        gu = _dot(x.astype(BF16), wgu_b[...]) + bgu_ref[0]
        g_ = jnp.minimum(gu[:, :F], SWIGLU_LIMIT)
        up = jnp.clip(gu[:, F:], -SWIGLU_LIMIT, SWIGLU_LIMIT)
        act = (up + 1.0) * g_ * _sigmoid(SWIGLU_ALPHA * g_)
        _rows_to_tiles(ys_o, _dot(act.astype(BF16), wd_b[...]) + bd_ref[0])

    @pl.when(j >= nu_ref[0])
    def _unused():
        ys_o[...] = jnp.zeros_like(ys_o)


def _experts(blk_e, n_used, nxt_e, xs, wgu, bgu, wd, bd):
    nb = xs.shape[0] // (MOE_BLOCK * SUB)
    E, D, F2 = wgu.shape
    F = F2 // 2
    grid_spec = pltpu.PrefetchScalarGridSpec(
        num_scalar_prefetch=3,
        grid=(nb,),
        in_specs=[pl.BlockSpec((MOE_BLOCK * SUB, LANES), lambda j, be, nu, nx: (jnp.minimum(j, nu[0] - 1), 0)),
                  pl.BlockSpec(memory_space=pl.ANY),
                  pl.BlockSpec((1, 1, F2), lambda j, be, nu, nx: (be[j], 0, 0)),
                  pl.BlockSpec(memory_space=pl.ANY),
                  pl.BlockSpec((1, 1, D), lambda j, be, nu, nx: (be[j], 0, 0))],
        out_specs=pl.BlockSpec((MOE_BLOCK * SUB, LANES), lambda j, be, nu, nx: (j, 0)),
        scratch_shapes=[pltpu.VMEM((2, D, F2), F32), pltpu.VMEM((2, F, D), F32),
                        pltpu.VMEM((D, F2), BF16), pltpu.VMEM((F, D), BF16),
                        pltpu.SMEM((1,), I32), pltpu.SemaphoreType.DMA((2, 2))],
    )
    return pl.pallas_call(
        _expert_kernel,
        grid_spec=grid_spec,
        out_shape=jax.ShapeDtypeStruct(xs.shape, F32),
        compiler_params=_cparams(("arbitrary",)),
        name="moe_experts",
    )(blk_e, n_used, nxt_e, xs, wgu, bgu, wd, bd)


def _combine_kernel(dest0_ref, destn_ref, h1_ref, pp_ref, ps_ref, gates_ref, ys_ref, npl_ref, wpg_ref, wpp_ref, fn_ref,
                    yp_o, ysm_o, gbuf, ybuf, sems, *, ntp, n_tiles):
    TM = h1_ref.shape[0]
    t = pl.program_id(0)

    def fetch(dref, i, slot):
        for kk in range(TOP_K):
            src = _row_tile(ys_ref, dref[i * TOP_K + kk])
            pltpu.make_async_copy(src, _row_tile(gbuf.at[slot, kk], i * SUB), sems.at[slot]).start(priority=kk % 2)

    @pl.when(t == 0)
    def _first_fetch():
        def body(i, _):
            fetch(dest0_ref, i, 0)
            return 0

        lax.fori_loop(0, TM, body, 0, unroll=DMA_UNROLL)

    def tile(slot):
        _row_copy_wait(gbuf.at[slot, 0], sems.at[slot], TOP_K)
        for i in range(TM):
            fetch(destn_ref, i, 1 - slot)
        gates = gates_ref[...]
        h2 = h1_ref[...]
        for kk in range(TOP_K):
            h2 = h2 + _tiles_to_rows(gbuf.at[slot, kk], TM) * gates[:, kk:kk + 1]
        u3 = _rms(h2, npl_ref[...])
        gate = _sigmoid(_dot(u3.astype(BF16), wpg_ref[...]))
        p = _pick(t, ntp, pp_ref, ps_ref)
        h3 = h2 + gate * _dot(p.astype(BF16), wpp_ref[...])
        return _rms(h3, fn_ref[...])

    for slot in range(2):
        @pl.when(t % 2 == slot)
        def _tile():
            ybuf[...] = tile(slot)

    @pl.when(t < ntp)
    def _prompt_rows():
        yp_o[...] = ybuf[...]

    @pl.when(t >= ntp)
    def _sample_rows():
        ysm_o[...] = ybuf[...]

    @pl.when(t == n_tiles - 1)
    def _drain():
        other = n_tiles % 2
        _row_copy_wait(gbuf.at[other, 0], sems.at[other], TOP_K)


def _combine(dest_flat, h1, pp, ps, gates, ys, npl, wpg, wpp, fn):
    T, D = h1.shape
    nt = T // ROW_TILE
    ntp = pp.shape[0] // ROW_TILE
    row = lambda n: pl.BlockSpec((ROW_TILE, n), lambda i: (i, 0))
    full = lambda a: pl.BlockSpec(a.shape, lambda i: (0, 0))
    idx = lambda f: pl.BlockSpec((ROW_TILE * TOP_K,), f, memory_space=pltpu.SMEM)
    return pl.pallas_call(
        functools.partial(_combine_kernel, ntp=ntp, n_tiles=nt),
        grid=(nt,),
        in_specs=[idx(lambda i: (0,)), idx(lambda i: (jnp.minimum(i + 1, nt - 1),)), row(D)]
        + _two_specs(ntp, pp.shape[1])
        + [row(LANES), pl.BlockSpec(memory_space=pl.ANY), full(npl), full(wpg), full(wpp), full(fn)],
        out_specs=_two_specs(ntp, D),
        out_shape=[jax.ShapeDtypeStruct((pp.shape[0], D), F32), jax.ShapeDtypeStruct((ps.shape[0], D), F32)],
        scratch_shapes=[pltpu.VMEM((2, TOP_K, ROW_TILE * SUB, LANES), F32), pltpu.VMEM((ROW_TILE, D), F32),
                        pltpu.SemaphoreType.DMA((2,))],
        compiler_params=_cparams(("arbitrary",)),
        name="moe_combine_ple",
    )(dest_flat, dest_flat, h1, pp, ps, gates, ys, npl, wpg, wpp, fn)


def _pad_lora_cols(x):
    W3 = 3 * B_WIDTH
    z = jnp.zeros(x.shape[:-1] + (LORA_PAD - 64,), x.dtype)
    return jnp.concatenate([x[..., :W3 + 64], z, x[..., W3 + 64:W3 + 128], z, x[..., W3 + 128:]], axis=-1)


def _unpad_lora_cols(x):
    W3 = 3 * B_WIDTH
    return jnp.concatenate([x[..., :W3 + 64], x[..., W3 + LORA_PAD:W3 + LORA_PAD + 64], x[..., W3 + 2 * LORA_PAD:]], axis=-1)


def _heads_to_pairs(S):
    n = S.shape[0]
    S = S.reshape(n, B_HEADS // 2, 2, B_HDIM, B_HDIM)
    z = jnp.zeros_like(S[:, :, 0])
    top = jnp.concatenate([S[:, :, 0], z], axis=-1)
    bot = jnp.concatenate([z, S[:, :, 1]], axis=-1)
    return jnp.concatenate([top, bot], axis=-2)


def _pairs_to_heads(Sb):
    n = Sb.shape[0]
    a = Sb[:, :, :B_HDIM, :B_HDIM]
    b = Sb[:, :, B_HDIM:, B_HDIM:]
    return jnp.stack([a, b], axis=2).reshape(n, B_HEADS, B_HDIM, B_HDIM)


def kernel(x_prompt, x_sample, state_mlstm_C, state_mlstm_n, state_mlstm_m, state_mlstm_conv, state_rwkv_S, state_rwkv_shift, p_prompt, p_sample, norm_mix, w_in, conv_w, conv_b, b_ig, b_fg, a_norm, shift_mu, w0, w_w2, a0, w_a2, w_g2, k_k, k_a, r_k, gn_w, gn_b, w_out, norm_ffn, w_router, b_router, w_gu, b_gu, w_down, b_down, norm_ple, w_ple_gate, w_ple_proj, final_norm):
    assert norm_mix.shape[0] == 1, "single-layer trunk"
    nbp, Lp, D = x_prompt.shape
    nbs, Ls, _ = x_sample.shape
    Tp, Ts = nbp * Lp, nbs * Ls
    T = Tp + Ts
    row = lambda a: a.reshape(1, -1).astype(F32)

    wi = w_in[0]
    AW2 = 2 * A_WIDTH
    wqk = wi[:, :AW2].astype(BF16)
    wv = wi[:, AW2:AW2 + A_WIDTH].astype(BF16)
    wo = wi[:, AW2 + A_WIDTH:AW2 + 2 * A_WIDTH].astype(BF16)
    a_cols = 4 * A_WIDTH + 2 * A_HEADS
    wif = jnp.pad(wi[:, 4 * A_WIDTH:a_cols], ((0, 0), (0, LANES - 2 * A_HEADS)))
    wbg = jnp.concatenate([_pad_lora_cols(wi[:, a_cols:]), wif], axis=1).astype(BF16)
    gate_bias = jnp.pad(jnp.concatenate([b_ig[0], b_fg[0]]), (0, LANES - 2 * A_HEADS)).reshape(1, LANES)
    padw = lambda w: jnp.pad(w, ((0, LORA_PAD - w.shape[0]), (0, 0)))
    hid = jnp.arange(B_WIDTH // 2) // B_HDIM
    bd = (hid[:, None] == hid[None, :]).astype(BF16)
    E = w_router.shape[-1]
    wr = w_router[0].T.astype(BF16)
    br = b_router[0].reshape(E, 1)

    assert Tp % ROW_TILE == 0 and Ts % ROW_TILE == 0
    xp, xsm = x_prompt.reshape(Tp, D), x_sample.reshape(Ts, D)
    qk, v, og, pb = _in_proj(xp, xsm, row(norm_mix), wqk, wv, wo, wbg)
    ifg = pb

    f0 = lambda *shp: jnp.zeros(shp, F32)
    ml_consts = (conv_w[0], row(conv_b), gate_bias, row(a_norm))
    m_s = jnp.pad(state_mlstm_m[0], ((0, 0), (0, LANES - A_HEADS))).reshape(nbs, 1, LANES)
    hap, Cp, np_, mp, convp = _mlstm(qk, v, og, ifg, f0(nbp, A_HEADS, A_HDIM, A_HDIM), f0(nbp, A_HEADS, A_HDIM),
                                     f0(nbp, 1, LANES), f0(nbp, CONV_W - 1, AW2), *ml_consts,
                                     n_seq=nbp, L=Lp, row_off=0)
    has, Cs, ns_, ms, convs = _mlstm(qk, v, og, ifg, state_mlstm_C[0], state_mlstm_n[0], m_s, state_mlstm_conv[0],
                                     *ml_consts, n_seq=nbs, L=Ls, row_off=Tp)

    rw_consts = (row(_pad_lora_cols(shift_mu[0])), row(w0), padw(w_w2[0]).astype(BF16), row(a0), padw(w_a2[0]).astype(BF16),
                 w_g2[0].astype(BF16),
                 row(k_k), row(k_a), row(r_k), row(gn_w), row(gn_b), bd)
    npair = B_HEADS // 2
    hbp, Sp, shp_ = _rwkv(pb, f0(nbp, npair, PAIR, PAIR), f0(nbp, 1, B_COLS_PAD), *rw_consts,
                          n_seq=nbp, L=Lp, row_off=0)
    hbs, Ss, shs = _rwkv(pb, _heads_to_pairs(state_rwkv_S[0]), _pad_lora_cols(state_rwkv_shift[0]), *rw_consts,
                         n_seq=nbs, L=Ls, row_off=Tp)

    wout = w_out[0].astype(BF16)
    hap, has = hap.reshape(Tp, A_WIDTH), has.reshape(Ts, A_WIDTH)
    h1, u2, ids, gates, rank, counts = _out_router(hap, has, hbp, hbs, xp, xsm, wout[:A_WIDTH], wout[A_WIDTH:],
                                                   row(norm_ffn), wr, br)

    counts = counts[:, 0]
    pcounts = (counts + MOE_BLOCK - 1) // MOE_BLOCK * MOE_BLOCK
    pend = jnp.cumsum(pcounts)
    pstart = pend - pcounts
    n_blk = -(-T * TOP_K // MOE_BLOCK) + E
    eid = jnp.arange(E, dtype=I32)
    base = jnp.sum(jnp.where(ids[None, :TOP_K] == eid[:, None, None], pstart[:, None, None], 0), axis=0)
    dest = ((base + rank[:TOP_K]) * SUB).T.reshape(-1).astype(I32)
    blk_start = jnp.arange(n_blk, dtype=I32) * MOE_BLOCK
    blk_e = jnp.minimum(jnp.sum(pend[None, :] <= blk_start[:, None], axis=1), E - 1).astype(I32)
    n_used = (pend[E - 1:] // MOE_BLOCK).astype(I32)

    xs = _scatter_rows((pstart + counts).astype(I32), pend.astype(I32), n_used, dest, u2, n_blk * MOE_BLOCK)
    later = jnp.where((eid[None, :] > eid[:, None]) & (pcounts[None, :] > 0), eid[None, :], E)
    nxt_e = jnp.min(later, axis=1)
    nxt_e = jnp.where(nxt_e < E, nxt_e, eid).astype(I32)
    ys = _experts(blk_e, n_used, nxt_e, xs, w_gu[0], b_gu[0][:, None, :], w_down[0], b_down[0][:, None, :])
    yp, ysm = _combine(dest, h1, p_prompt[0].reshape(Tp, -1), p_sample[0].reshape(Ts, -1), gates, ys, row(norm_ple),
                       w_ple_gate[0].astype(BF16), w_ple_proj[0].astype(BF16), row(final_norm))

    lead = lambda a: a[None]
    return (yp.reshape(nbp, Lp, D), ysm.reshape(nbs, Ls, D),
            lead(Cp), lead(np_), lead(mp[:, 0, :A_HEADS]), lead(convp), lead(_pairs_to_heads(Sp)), lead(_unpad_lora_cols(shp_)),
            lead(Cs), lead(ns_), lead(ms[:, 0, :A_HEADS]), lead(convs), lead(_pairs_to_heads(Ss)), lead(_unpad_lora_cols(shs)))
```
